```python
import math
import jax, jax.numpy as jnp
from jax import lax
import numpy as np

D_MODEL = 1024
BATCH = 8
SEQ = 2048
DEPTH = 1

D_MIX = D_MODEL
ATT_WIDTH = D_MIX // 2
POOL_WIDTH = D_MIX - ATT_WIDTH
HEAD_DIM = 64
N_HEADS = ATT_WIDTH // HEAD_DIM
IDX_HEADS = 4
IDX_DIM = 64
IDX_ROPE_DIM = 32
INDEX_TOPK = 256
POOL_WINDOWS = (2, 4, 8, 16)
N_POOL_GROUPS = len(POOL_WINDOWS)
POOL_CH = POOL_WIDTH // N_POOL_GROUPS
D_FF = 2816
ROPE_THETA = 10000.0
NORM_EPS = 1e-6
Q_BLOCK = 128
IN_SPLITS = (ATT_WIDTH, ATT_WIDTH, ATT_WIDTH, IDX_HEADS * IDX_DIM, IDX_DIM, IDX_HEADS, POOL_WIDTH)
IN_COLS = sum(IN_SPLITS)

kernel_name = "hybrid_dsa_multiscale_pool_macaron"


def rms_norm(x, g):
    xf = x.astype(jnp.float32)
    y = xf * lax.rsqrt(jnp.mean(xf * xf, axis=-1, keepdims=True) + NORM_EPS)
    return (y * g.astype(jnp.float32)).astype(x.dtype)


def rope(x, pos, rot_dim):
    half = rot_dim // 2
    inv = ROPE_THETA ** (-jnp.arange(half, dtype=jnp.float32) / half)
    ang = pos.astype(jnp.float32)[:, None] * inv[None, :]
    cos = jnp.cos(ang)[:, None, :]
    sin = jnp.sin(ang)[:, None, :]
    xr = x[..., :rot_dim].astype(jnp.float32)
    x1, x2 = xr[..., :half], xr[..., half:]
    rot = jnp.concatenate([x1 * cos - x2 * sin, x2 * cos + x1 * sin], axis=-1).astype(x.dtype)
    return jnp.concatenate([rot, x[..., rot_dim:]], axis=-1)


def swiglu(x, w_gate, w_up, w_down):
    return (jax.nn.silu(x @ w_gate) * (x @ w_up)) @ w_down


def dsa_attention(q, k, v, q_idx, k_idx, w_idx, top_k):
    B, S, H, Dh = q.shape
    nb = S // Q_BLOCK
    key_pos = jnp.arange(S)
    scale = Dh ** -0.5

    def to_blocks(a):
        return a.reshape((B, nb, Q_BLOCK) + a.shape[2:]).swapaxes(0, 1)

    def block(args):
        qb, qib, wb, posb = args
        rel = jax.nn.relu(jnp.einsum('bthd,bsd->bths', qib, k_idx).astype(jnp.float32))
        score = jnp.einsum('bth,bths->bts', wb.astype(jnp.float32), rel)
        causal = key_pos[None, :] <= posb[:, None]
        score = jnp.where(causal[None], score, -jnp.inf)
        _, idx = lax.top_k(score, top_k)
        valid = idx <= posb[None, :, None]
        k_sel = jax.vmap(lambda kk, ii: kk[ii])(k, idx)
        v_sel = jax.vmap(lambda vv, ii: vv[ii])(v, idx)
        logits = jnp.einsum('bthd,btkhd->bhtk', qb, k_sel).astype(jnp.float32) * scale
        logits = jnp.where(valid[:, None], logits, -jnp.inf)
        p = jax.nn.softmax(logits, axis=-1).astype(v.dtype)
        return jnp.einsum('bhtk,btkhd->bthd', p, v_sel)

    pos_blocks = jnp.arange(S).reshape(nb, Q_BLOCK)
    out = lax.map(block, (to_blocks(q), to_blocks(q_idx), to_blocks(w_idx), pos_blocks))
    return out.swapaxes(0, 1).reshape(B, S, H * Dh)


def multiscale_pool(u):
    B, S, _ = u.shape
    uf = u.astype(jnp.float32).reshape(B, S, N_POOL_GROUPS, POOL_CH)
    c = jnp.concatenate([jnp.zeros((B, 1, N_POOL_GROUPS, POOL_CH), jnp.float32),
                         jnp.cumsum(uf, axis=1)], axis=1)
    t = jnp.arange(S)
    outs = []
    for g, w in enumerate(POOL_WINDOWS):
        cg = c[:, :, g]
        c_lo = jnp.concatenate([jnp.zeros((B, w - 1, POOL_CH), jnp.float32), cg[:, :S + 1 - w]], axis=1)
        count = jnp.minimum(t + 1, w).astype(jnp.float32)[None, :, None]
        outs.append((cg[:, 1:] - c_lo) / count - uf[:, :, g])
    return jnp.stack(outs, axis=2).astype(u.dtype)


def setup_inputs(seed: int = 0) -> dict:
    key = jax.random.key(seed)
    ks = jax.random.split(key, 20)
    f32 = jnp.float32

    def nrm(k, shape, fan_in):
        return jax.random.normal(k, shape, f32) * fan_in ** -0.5

    def gain(k, shape):
        return 1.0 + 0.02 * jax.random.normal(k, shape, f32)

    L = DEPTH
    return {
        "x": jax.random.normal(ks[0], (BATCH, SEQ, D_MODEL), f32),
        "ffn1_norm": gain(ks[1], (L, D_MODEL)),
        "ffn1_w_gate": nrm(ks[2], (L, D_MODEL, D_FF), D_MODEL),
        "ffn1_w_up": nrm(ks[3], (L, D_MODEL, D_FF), D_MODEL),
        "ffn1_w_down": nrm(ks[4], (L, D_FF, D_MODEL), D_FF),
        "mix_norm": gain(ks[5], (L, D_MODEL)),
        "w_in": nrm(ks[6], (L, D_MODEL, IN_COLS), D_MODEL),
        "q_norm": gain(ks[7], (L, HEAD_DIM)),
        "k_norm": gain(ks[8], (L, HEAD_DIM)),
        "pool_w": nrm(ks[9], (L, N_POOL_GROUPS, POOL_CH, POOL_CH), POOL_CH),
        "pool_scale": 0.5 + 0.05 * jax.random.normal(ks[10], (L, POOL_WIDTH), f32),
        "w_out": nrm(ks[11], (L, D_MIX, D_MODEL), D_MIX),
        "ffn2_norm": gain(ks[12], (L, D_MODEL)),
        "ffn2_w_gate": nrm(ks[13], (L, D_MODEL, D_FF), D_MODEL),
        "ffn2_w_up": nrm(ks[14], (L, D_MODEL, D_FF), D_MODEL),
        "ffn2_w_down": nrm(ks[15], (L, D_FF, D_MODEL), D_FF),
    }


def reference(x, ffn1_norm, ffn1_w_gate, ffn1_w_up, ffn1_w_down, mix_norm, w_in,
              q_norm, k_norm, pool_w, pool_scale, w_out,
              ffn2_norm, ffn2_w_gate, ffn2_w_up, ffn2_w_down):
    B, S, _ = x.shape
    top_k = min(INDEX_TOPK, S // 4)
    pos = jnp.arange(S)
    split_points = list(np.cumsum(IN_SPLITS)[:-1])
    idx_scale = (IDX_HEADS ** -0.5) * (IDX_DIM ** -0.5)

    for l in range(DEPTH):
        x = x + 0.5 * swiglu(rms_norm(x, ffn1_norm[l]), ffn1_w_gate[l], ffn1_w_up[l], ffn1_w_down[l])

        h = rms_norm(x, mix_norm[l])
        proj = h @ w_in[l]
        q, k, v, qi, ki, wi, u = jnp.split(proj, split_points, axis=-1)
        q = rope(rms_norm(q.reshape(B, S, N_HEADS, HEAD_DIM), q_norm[l]), pos, HEAD_DIM)
        k = rope(rms_norm(k.reshape(B, S, N_HEADS, HEAD_DIM), k_norm[l]), pos, HEAD_DIM)
        v = v.reshape(B, S, N_HEADS, HEAD_DIM)
        qi = rope(qi.reshape(B, S, IDX_HEADS, IDX_DIM), pos, IDX_ROPE_DIM)
        ki = rope(ki.reshape(B, S, 1, IDX_DIM), pos, IDX_ROPE_DIM)[:, :, 0]
        wi = wi * idx_scale

        attn = dsa_attention(q, k, v, qi, ki, wi, top_k)

        pooled = jnp.einsum('bsgc,gcd->bsgd', multiscale_pool(u), pool_w[l])
        pooled = pooled.reshape(B, S, POOL_WIDTH) * pool_scale[l]

        x = x + jnp.concatenate([attn, pooled], axis=-1) @ w_out[l]

        x = x + 0.5 * swiglu(rms_norm(x, ffn2_norm[l]), ffn2_w_gate[l], ffn2_w_up[l], ffn2_w_down[l])
    return x
```

```python
import functools
import math

import jax
import jax.numpy as jnp
import numpy as np
from jax import lax
from jax.experimental import pallas as pl
from jax.experimental.pallas import tpu as pltpu

HEAD_DIM = 64
N_HEADS = 8
ATT_WIDTH = N_HEADS * HEAD_DIM
IDX_HEADS = 4
IDX_DIM = 64
IDX_ROPE_DIM = 32
INDEX_TOPK = 256
POOL_WINDOWS = (2, 4, 8, 16)
POOL_CH = 128
POOL_WIDTH = POOL_CH * len(POOL_WINDOWS)
ROPE_THETA = 10000.0
NORM_EPS = 1e-6
WI_ROWS = 8

VMEM_LIMIT_BYTES = 56 * 1024 * 1024

F32 = jnp.float32
BF16 = jnp.bfloat16
NT_DIMS = (((1,), (1,)), ((), ()))


def _rms(x, g):
    ms = jnp.mean(x * x, axis=-1, keepdims=True)
    return x * lax.rsqrt(ms + NORM_EPS) * g


def _ffn_kernel(x_ref, g_ref, wg_ref, wu_ref, wd_ref, o_ref, h_ref, acc_ref):
    j = pl.program_id(1)

    @pl.when(j == 0)
    def _():
        h_ref[...] = _rms(x_ref[...], g_ref[...]).astype(BF16)
        acc_ref[...] = jnp.zeros_like(acc_ref)

    h = h_ref[...]
    gate = jnp.dot(h, wg_ref[...], preferred_element_type=F32)
    up = jnp.dot(h, wu_ref[...], preferred_element_type=F32)
    act = (gate * jax.nn.sigmoid(gate) * up).astype(BF16)
    acc_ref[...] += jnp.dot(act, wd_ref[...], preferred_element_type=F32)

    @pl.when(j == pl.num_programs(1) - 1)
    def _():
        o_ref[...] = x_ref[...] + 0.5 * acc_ref[...]


def _ffn(x, g, wg, wu, wd, *, tm=512, tf=256):
    n, d = x.shape
    f = wg.shape[1]
    return pl.pallas_call(
        _ffn_kernel,
        grid=(n // tm, f // tf),
        in_specs=[
            pl.BlockSpec((tm, d), lambda i, j: (i, 0)),
            pl.BlockSpec((1, d), lambda i, j: (0, 0)),
            pl.BlockSpec((d, tf), lambda i, j: (0, j)),
            pl.BlockSpec((d, tf), lambda i, j: (0, j)),
            pl.BlockSpec((tf, d), lambda i, j: (j, 0)),
        ],
        out_specs=pl.BlockSpec((tm, d), lambda i, j: (i, 0)),
        out_shape=jax.ShapeDtypeStruct((n, d), F32),
        scratch_shapes=[pltpu.VMEM((tm, d), BF16), pltpu.VMEM((tm, d), F32)],
        compiler_params=pltpu.CompilerParams(
            dimension_semantics=("parallel", "arbitrary"),
            vmem_limit_bytes=VMEM_LIMIT_BYTES),
        name="ffn",
    )(x, g, wg, wu, wd)


def _rope_rows(x, cos, sin, half):
    x1, x2 = x[:half], x[half:2 * half]
    return x1 * cos - x2 * sin, x2 * cos + x1 * sin


def _proj_kernel(x_ref, g_ref, wt_ref, wu_ref, gq_ref, gk_ref, cos_ref, sin_ref,
                 cosi_ref, sini_ref,
                 qT_ref, kp_ref, vT_ref, qiT_ref, ki_ref, wiT_ref, u_ref, *, idx_scale):
    h = _rms(x_ref[...], g_ref[...]).astype(BF16)
    pT = lax.dot_general(wt_ref[...], h, NT_DIMS, preferred_element_type=F32)
    u_ref[...] = jnp.dot(h, wu_ref[...], preferred_element_type=F32)

    cos, sin = cos_ref[...], sin_ref[...]
    gq, gk = gq_ref[...], gk_ref[...]
    half = HEAD_DIM // 2

    def head_norm_rope(rows, gain):
        ms = jnp.mean(rows * rows, axis=0, keepdims=True)
        return _rope_rows(rows * lax.rsqrt(ms + NORM_EPS) * gain, cos, sin, half)

    for hd in range(N_HEADS):
        r0 = hd * HEAD_DIM
        a, b = head_norm_rope(pT[r0:r0 + HEAD_DIM], gq)
        qT_ref[0, r0:r0 + half, :] = a.astype(BF16)
        qT_ref[0, r0 + half:r0 + HEAD_DIM, :] = b.astype(BF16)

    k0 = ATT_WIDTH
    for pair in range(N_HEADS // 2):
        parts = []
        for e in range(2):
            r0 = k0 + (2 * pair + e) * HEAD_DIM
            parts.extend(head_norm_rope(pT[r0:r0 + HEAD_DIM], gk))
        kp_ref[0, pair] = jnp.concatenate(parts, axis=0).T.astype(BF16)

    v0 = 2 * ATT_WIDTH
    vT_ref[0] = pT[v0:v0 + ATT_WIDTH].astype(BF16)

    cosi, sini = cosi_ref[...], sini_ref[...]
    ihalf = IDX_ROPE_DIM // 2
    qi0 = 3 * ATT_WIDTH
    for hi in range(IDX_HEADS):
        r0 = qi0 + hi * IDX_DIM
        a, b = _rope_rows(pT[r0:r0 + IDX_DIM], cosi, sini, ihalf)
        qiT_ref[0, r0 - qi0:r0 - qi0 + ihalf, :] = a.astype(BF16)
        qiT_ref[0, r0 - qi0 + ihalf:r0 - qi0 + IDX_ROPE_DIM, :] = b.astype(BF16)
        qiT_ref[0, r0 - qi0 + IDX_ROPE_DIM:r0 - qi0 + IDX_DIM, :] = (
            pT[r0 + IDX_ROPE_DIM:r0 + IDX_DIM].astype(BF16))

    ki0 = qi0 + IDX_HEADS * IDX_DIM
    a, b = _rope_rows(pT[ki0:ki0 + IDX_DIM], cosi, sini, ihalf)
    ki = jnp.concatenate([a, b, pT[ki0 + IDX_ROPE_DIM:ki0 + IDX_DIM]], axis=0)
    zeros = jnp.zeros_like(ki)
    ki_ref[0, 0] = jnp.concatenate([ki, zeros], axis=0).T.astype(BF16)
    ki_ref[0, 1] = jnp.concatenate([zeros, ki], axis=0).T.astype(BF16)

    wi0 = ki0 + IDX_DIM
    wiT_ref[0] = pT[wi0:wi0 + WI_ROWS] * idx_scale


def _proj(x1, g, wt, wu, gq, gk, cos, sin, cosi, sini, *, batch, seq, idx_scale, tt=512):
    n, d = x1.shape
    rows = wt.shape[0]
    nt = seq // tt
    tok = lambda b, t: (b * nt + t, 0)
    const = lambda b, t: (0, 0)
    featmaj = lambda b, t: (b, 0, t)
    out_shape = (
        jax.ShapeDtypeStruct((batch, ATT_WIDTH, seq), BF16),
        jax.ShapeDtypeStruct((batch, N_HEADS // 2, seq, 2 * HEAD_DIM), BF16),
        jax.ShapeDtypeStruct((batch, ATT_WIDTH, seq), BF16),
        jax.ShapeDtypeStruct((batch, IDX_HEADS * IDX_DIM, seq), BF16),
        jax.ShapeDtypeStruct((batch, 2, seq, 2 * IDX_DIM), BF16),
        jax.ShapeDtypeStruct((batch, WI_ROWS, seq), F32),
        jax.ShapeDtypeStruct((n, POOL_WIDTH), F32),
    )
    out_specs = (
        pl.BlockSpec((1, ATT_WIDTH, tt), featmaj),
        pl.BlockSpec((1, N_HEADS // 2, tt, 2 * HEAD_DIM), lambda b, t: (b, 0, t, 0)),
        pl.BlockSpec((1, ATT_WIDTH, tt), featmaj),
        pl.BlockSpec((1, IDX_HEADS * IDX_DIM, tt), featmaj),
        pl.BlockSpec((1, 2, tt, 2 * IDX_DIM), lambda b, t: (b, 0, t, 0)),
        pl.BlockSpec((1, WI_ROWS, tt), featmaj),
        pl.BlockSpec((tt, POOL_WIDTH), tok),
    )
    in_specs = [
        pl.BlockSpec((tt, d), tok),
        pl.BlockSpec((1, d), const),
        pl.BlockSpec((rows, d), const),
        pl.BlockSpec((d, POOL_WIDTH), const),
        pl.BlockSpec((HEAD_DIM, 1), const),
        pl.BlockSpec((HEAD_DIM, 1), const),
        pl.BlockSpec((HEAD_DIM // 2, tt), lambda b, t: (0, t)),
        pl.BlockSpec((HEAD_DIM // 2, tt), lambda b, t: (0, t)),
        pl.BlockSpec((IDX_ROPE_DIM // 2, tt), lambda b, t: (0, t)),
        pl.BlockSpec((IDX_ROPE_DIM // 2, tt), lambda b, t: (0, t)),
    ]
    return pl.pallas_call(
        functools.partial(_proj_kernel, idx_scale=idx_scale),
        grid=(batch, nt),
        in_specs=in_specs,
        out_specs=out_specs,
        out_shape=out_shape,
        compiler_params=pltpu.CompilerParams(
            dimension_semantics=("parallel", "parallel"),
            vmem_limit_bytes=VMEM_LIMIT_BYTES),
        name="proj",
    )(x1, g, wt, wu, gq, gk, cos, sin, cosi, sini)


INT_MIN = np.int32(-2 ** 31)


def _attn_kernel(qT_ref, kp_ref, vT_ref, qiT_ref, ki_ref, wiT_ref, o_ref,
                 key_ref, bias_ref, *, top_k, scale):
    seq = kp_ref.shape[2]
    tq = qT_ref.shape[2]
    q_pos = pl.program_id(1) * tq + lax.broadcasted_iota(jnp.int32, (seq, tq), 1)
    k_pos = lax.broadcasted_iota(jnp.int32, (seq, tq), 0)
    causal = k_pos <= q_pos

    wi = wiT_ref[0]
    score = jnp.zeros((seq, tq), F32)
    for hi in range(IDX_HEADS):
        pair, e = divmod(hi, 2)
        rel = jnp.dot(ki_ref[0, e], qiT_ref[0, 2 * IDX_DIM * pair:2 * IDX_DIM * (pair + 1), :],
                      preferred_element_type=F32)
        score = score + wi[hi:hi + 1, :] * jnp.maximum(rel, 0.0)
    score = jnp.where(causal, score + 0.0, -jnp.inf)

    bits = pltpu.bitcast(score, jnp.int32)
    key_ref[...] = jnp.where(bits < 0, bits ^ np.int32(0x7FFFFFFF), bits)

    def count(pred):
        return jnp.sum(jnp.where(pred, 1, 0), axis=0, keepdims=True)

    def thr_step(i, tu):
        cand = tu | lax.shift_left(jnp.int32(1), 31 - i)
        cnt = count(key_ref[...] >= (cand ^ INT_MIN))
        return jnp.where(cnt >= top_k, cand, tu)

    tu = lax.fori_loop(0, 32, thr_step, jnp.zeros((1, tq), jnp.int32))
    thr = tu ^ INT_MIN
    keys = key_ref[...]
    need = top_k - count(keys > thr)
    tie_pos = jnp.where(keys == thr, k_pos, seq)
    key_ref[...] = tie_pos

    def tie_step(i, c):
        cand = c + lax.shift_left(jnp.int32(1), seq.bit_length() - 2 - i)
        cnt = count(key_ref[...] < cand)
        return jnp.where(cnt < need, cand, c)

    cut = lax.fori_loop(0, seq.bit_length() - 1, tie_step, jnp.zeros((1, tq), jnp.int32))
    causal_bias = jnp.where(causal, 0.0, -jnp.inf)
    bias_ref[...] = jnp.where(keys > thr, causal_bias,
                              jnp.where(tie_pos <= cut, causal_bias, -jnp.inf))

    half_rows = lax.broadcasted_iota(jnp.int32, (2 * HEAD_DIM, tq), 0) < HEAD_DIM
    outs = []
    for hd in range(N_HEADS):
        pair, e = divmod(hd, 2)
        qpair = qT_ref[0, 2 * HEAD_DIM * pair:2 * HEAD_DIM * (pair + 1), :]
        qz = jnp.where(half_rows if e == 0 else jnp.logical_not(half_rows), qpair,
                       jnp.zeros_like(qpair))
        logits = jnp.dot(kp_ref[0, pair], qz, preferred_element_type=F32) * scale + bias_ref[...]
        m = jnp.max(logits, axis=0, keepdims=True)
        p = jnp.exp(logits - m)
        l = jnp.sum(p, axis=0, keepdims=True)
        oT = jnp.dot(vT_ref[0, hd * HEAD_DIM:(hd + 1) * HEAD_DIM, :], p.astype(BF16),
                     preferred_element_type=F32)
        outs.append(oT / l)
    o_ref[0] = jnp.concatenate(outs, axis=0).T.astype(BF16)


def _attn(qT, kp, vT, qiT, ki, wiT, *, top_k, tq=128):
    batch, _, seq = qT.shape
    qblk = lambda b, i: (b, 0, i)
    full3 = lambda b, i: (b, 0, 0)
    full4 = lambda b, i: (b, 0, 0, 0)
    return pl.pallas_call(
        functools.partial(_attn_kernel, top_k=top_k, scale=HEAD_DIM ** -0.5),
        grid=(batch, seq // tq),
        in_specs=[
            pl.BlockSpec((1, ATT_WIDTH, tq), qblk),
            pl.BlockSpec((1, N_HEADS // 2, seq, 2 * HEAD_DIM), full4),
            pl.BlockSpec((1, ATT_WIDTH, seq), full3),
            pl.BlockSpec((1, IDX_HEADS * IDX_DIM, tq), qblk),
            pl.BlockSpec((1, 2, seq, 2 * IDX_DIM), full4),
            pl.BlockSpec((1, WI_ROWS, tq), qblk),
        ],
        out_specs=pl.BlockSpec((1, tq, ATT_WIDTH), lambda b, i: (b, i, 0)),
        out_shape=jax.ShapeDtypeStruct((batch, seq, ATT_WIDTH), BF16),
        scratch_shapes=[pltpu.VMEM((seq, tq), jnp.int32), pltpu.VMEM((seq, tq), F32)],
        compiler_params=pltpu.CompilerParams(
            dimension_semantics=("parallel", "arbitrary"),
            vmem_limit_bytes=VMEM_LIMIT_BYTES),
        name="attn",
    )(qT, kp, vT, qiT, ki, wiT)


def _pool_kernel(u_ref, pw_ref, ps_ref, o_ref):
    seq = u_ref.shape[1]
    t = lax.broadcasted_iota(jnp.int32, (seq, POOL_CH), 0)
    for g, w in enumerate(POOL_WINDOWS):
        ug = u_ref[0, :, g * POOL_CH:(g + 1) * POOL_CH]
        s, sh = ug, 1
        while sh < w:
            s = s + jnp.where(t >= sh, pltpu.roll(s, sh, axis=0), 0.0)
            sh *= 2
        pooled = s / jnp.minimum(t + 1, w).astype(F32) - ug
        mixed = jnp.dot(pooled.astype(BF16), pw_ref[g], preferred_element_type=F32)
        o_ref[0, :, g * POOL_CH:(g + 1) * POOL_CH] = (
            mixed * ps_ref[:, g * POOL_CH:(g + 1) * POOL_CH]).astype(BF16)


def _pool(u, pw, ps):
    batch, seq, width = u.shape
    return pl.pallas_call(
        _pool_kernel,
        grid=(batch,),
        in_specs=[
            pl.BlockSpec((1, seq, width), lambda b: (b, 0, 0)),
            pl.BlockSpec(pw.shape, lambda b: (0, 0, 0)),
            pl.BlockSpec((1, width), lambda b: (0, 0)),
        ],
        out_specs=pl.BlockSpec((1, seq, width), lambda b: (b, 0, 0)),
        out_shape=jax.ShapeDtypeStruct((batch, seq, width), BF16),
        compiler_params=pltpu.CompilerParams(
            dimension_semantics=("parallel",), vmem_limit_bytes=VMEM_LIMIT_BYTES),
        name="pool",
    )(u, pw, ps)


def _outproj_kernel(x_ref, a_ref, p_ref, wa_ref, wp_ref, o_ref):
    o_ref[...] = (x_ref[...]
                  + jnp.dot(a_ref[...], wa_ref[...], preferred_element_type=F32)
                  + jnp.dot(p_ref[...], wp_ref[...], preferred_element_type=F32))


def _outproj(x1, attn, pooled, wa, wp, *, tm=512):
    n, d = x1.shape
    return pl.pallas_call(
        _outproj_kernel,
        grid=(n // tm,),
        in_specs=[
            pl.BlockSpec((tm, d), lambda i: (i, 0)),
            pl.BlockSpec((tm, attn.shape[1]), lambda i: (i, 0)),
            pl.BlockSpec((tm, pooled.shape[1]), lambda i: (i, 0)),
            pl.BlockSpec(wa.shape, lambda i: (0, 0)),
            pl.BlockSpec(wp.shape, lambda i: (0, 0)),
        ],
        out_specs=pl.BlockSpec((tm, d), lambda i: (i, 0)),
        out_shape=jax.ShapeDtypeStruct((n, d), F32),
        compiler_params=pltpu.CompilerParams(
            dimension_semantics=("parallel",), vmem_limit_bytes=VMEM_LIMIT_BYTES),
        name="outproj",
    )(x1, attn, pooled, wa, wp)


def _rope_tables(seq, rot_dim):
    half = rot_dim // 2
    inv = ROPE_THETA ** (-jnp.arange(half, dtype=F32) / half)
    ang = inv[:, None] * jnp.arange(seq, dtype=F32)[None, :]
    return jnp.cos(ang), jnp.sin(ang)


def kernel(x, ffn1_norm, ffn1_w_gate, ffn1_w_up, ffn1_w_down, mix_norm, w_in, q_norm, k_norm,
           pool_w, pool_scale, w_out, ffn2_norm, ffn2_w_gate, ffn2_w_up, ffn2_w_down):
    batch, seq, d = x.shape
    depth = w_in.shape[0]
    top_k = min(INDEX_TOPK, seq // 4)
    idx_scale = (IDX_HEADS ** -0.5) * (IDX_DIM ** -0.5)
    cos, sin = _rope_tables(seq, HEAD_DIM)
    cosi, sini = _rope_tables(seq, IDX_ROPE_DIM)
    n_feat = 3 * ATT_WIDTH + IDX_HEADS * IDX_DIM + IDX_DIM
    xf = x.reshape(batch * seq, d)

    for l in range(depth):
        xf = _ffn(xf, ffn1_norm[l][None], ffn1_w_gate[l].astype(BF16), ffn1_w_up[l].astype(BF16),
                  ffn1_w_down[l].astype(BF16))

        w = w_in[l]
        w_feat = jnp.pad(w[:, :n_feat + IDX_HEADS], ((0, 0), (0, WI_ROWS - IDX_HEADS)))
        wt = w_feat.T.astype(BF16)
        wu = w[:, n_feat + IDX_HEADS:].astype(BF16)
        qT, kp, vT, qiT, ki, wiT, u = _proj(
            xf, mix_norm[l][None], wt, wu, q_norm[l][:, None], k_norm[l][:, None],
            cos, sin, cosi, sini, batch=batch, seq=seq, idx_scale=idx_scale)

        attn = _attn(qT, kp, vT, qiT, ki, wiT, top_k=top_k)
        pooled = _pool(u.reshape(batch, seq, POOL_WIDTH), pool_w[l].astype(BF16), pool_scale[l][None])

        wo = w_out[l].astype(BF16)
        xf = _outproj(xf, attn.reshape(batch * seq, ATT_WIDTH), pooled.reshape(batch * seq, POOL_WIDTH),
                      wo[:ATT_WIDTH], wo[ATT_WIDTH:])

        xf = _ffn(xf, ffn2_norm[l][None], ffn2_w_gate[l].astype(BF16), ffn2_w_up[l].astype(BF16),
                  ffn2_w_down[l].astype(BF16))
    return xf.reshape(batch, seq, d)
```

```python
import functools
import math

import jax
import jax.numpy as jnp
import numpy as np
from jax import lax
from jax.experimental import pallas as pl
from jax.experimental.pallas import tpu as pltpu

HEAD_DIM = 64
N_HEADS = 8
ATT_WIDTH = N_HEADS * HEAD_DIM
IDX_HEADS = 4
IDX_DIM = 64
IDX_ROPE_DIM = 32
INDEX_TOPK = 256
POOL_WINDOWS = (2, 4, 8, 16)
POOL_CH = 128
POOL_WIDTH = POOL_CH * len(POOL_WINDOWS)
ROPE_THETA = 10000.0
NORM_EPS = 1e-6
WI_ROWS = 8

VMEM_LIMIT_BYTES = 56 * 1024 * 1024

F32 = jnp.float32
BF16 = jnp.bfloat16
NT_DIMS = (((1,), (1,)), ((), ()))


def _rms(x, g):
    ms = jnp.mean(x * x, axis=-1, keepdims=True)
    return x * lax.rsqrt(ms + NORM_EPS) * g


def _ffn_kernel(x_ref, g_ref, wg_ref, wu_ref, wd_ref, o_ref, h_ref, acc_ref):
    j = pl.program_id(1)

    @pl.when(j == 0)
    def _():
        h_ref[...] = _rms(x_ref[...], g_ref[...]).astype(BF16)
        acc_ref[...] = jnp.zeros_like(acc_ref)

    h = h_ref[...]
    gate = jnp.dot(h, wg_ref[...], preferred_element_type=F32)
    up = jnp.dot(h, wu_ref[...], preferred_element_type=F32)
    act = (gate * jax.nn.sigmoid(gate) * up).astype(BF16)
    acc_ref[...] += jnp.dot(act, wd_ref[...], preferred_element_type=F32)

    @pl.when(j == pl.num_programs(1) - 1)
    def _():
        o_ref[...] = x_ref[...] + 0.5 * acc_ref[...]


def _ffn(x, g, wg, wu, wd, *, tm=512, tf=256):
    n, d = x.shape
    f = wg.shape[1]
    return pl.pallas_call(
        _ffn_kernel,
        grid=(n // tm, f // tf),
        in_specs=[
            pl.BlockSpec((tm, d), lambda i, j: (i, 0)),
            pl.BlockSpec((1, d), lambda i, j: (0, 0)),
            pl.BlockSpec((d, tf), lambda i, j: (0, j)),
            pl.BlockSpec((d, tf), lambda i, j: (0, j)),
            pl.BlockSpec((tf, d), lambda i, j: (j, 0)),
        ],
        out_specs=pl.BlockSpec((tm, d), lambda i, j: (i, 0)),
        out_shape=jax.ShapeDtypeStruct((n, d), F32),
        scratch_shapes=[pltpu.VMEM((tm, d), BF16), pltpu.VMEM((tm, d), F32)],
        compiler_params=pltpu.CompilerParams(
            dimension_semantics=("parallel", "arbitrary"),
            vmem_limit_bytes=VMEM_LIMIT_BYTES),
        name="ffn",
    )(x, g, wg, wu, wd)


def _rope_rows(x, cos, sin, half):
    x1, x2 = x[:half], x[half:2 * half]
    return x1 * cos - x2 * sin, x2 * cos + x1 * sin


def _proj_kernel(x_ref, g_ref, wt_ref, wu_ref, gq_ref, gk_ref, cos_ref, sin_ref,
                 cosi_ref, sini_ref,
                 qT_ref, kp_ref, vT_ref, qiT_ref, ki_ref, wiT_ref, u_ref, *, idx_scale):
    h = _rms(x_ref[...], g_ref[...]).astype(BF16)
    pT = lax.dot_general(wt_ref[...], h, NT_DIMS, preferred_element_type=F32)
    u_ref[...] = jnp.dot(h, wu_ref[...], preferred_element_type=F32)

    cos, sin = cos_ref[...], sin_ref[...]
    gq, gk = gq_ref[...], gk_ref[...]
    half = HEAD_DIM // 2

    def head_norm_rope(rows, gain):
        ms = jnp.mean(rows * rows, axis=0, keepdims=True)
        return _rope_rows(rows * lax.rsqrt(ms + NORM_EPS) * gain, cos, sin, half)

    for hd in range(N_HEADS):
        r0 = hd * HEAD_DIM
        a, b = head_norm_rope(pT[r0:r0 + HEAD_DIM], gq)
        qT_ref[0, r0:r0 + half, :] = a.astype(BF16)
        qT_ref[0, r0 + half:r0 + HEAD_DIM, :] = b.astype(BF16)

    k0 = ATT_WIDTH
    for pair in range(N_HEADS // 2):
        parts = []
        for e in range(2):
            r0 = k0 + (2 * pair + e) * HEAD_DIM
            parts.extend(head_norm_rope(pT[r0:r0 + HEAD_DIM], gk))
        kp_ref[0, pair] = jnp.concatenate(parts, axis=0).T.astype(BF16)

    v0 = 2 * ATT_WIDTH
    vT_ref[0] = pT[v0:v0 + ATT_WIDTH].astype(BF16)

    cosi, sini = cosi_ref[...], sini_ref[...]
    ihalf = IDX_ROPE_DIM // 2
    qi0 = 3 * ATT_WIDTH
    for hi in range(IDX_HEADS):
        r0 = qi0 + hi * IDX_DIM
        a, b = _rope_rows(pT[r0:r0 + IDX_DIM], cosi, sini, ihalf)
        qiT_ref[0, r0 - qi0:r0 - qi0 + ihalf, :] = a.astype(BF16)
        qiT_ref[0, r0 - qi0 + ihalf:r0 - qi0 + IDX_ROPE_DIM, :] = b.astype(BF16)
        qiT_ref[0, r0 - qi0 + IDX_ROPE_DIM:r0 - qi0 + IDX_DIM, :] = (
            pT[r0 + IDX_ROPE_DIM:r0 + IDX_DIM].astype(BF16))

    ki0 = qi0 + IDX_HEADS * IDX_DIM
    a, b = _rope_rows(pT[ki0:ki0 + IDX_DIM], cosi, sini, ihalf)
    ki = jnp.concatenate([a, b, pT[ki0 + IDX_ROPE_DIM:ki0 + IDX_DIM]], axis=0)
    zeros = jnp.zeros_like(ki)
    ki_ref[0, 0] = jnp.concatenate([ki, zeros], axis=0).T.astype(BF16)
    ki_ref[0, 1] = jnp.concatenate([zeros, ki], axis=0).T.astype(BF16)

    wi0 = ki0 + IDX_DIM
    wiT_ref[0] = pT[wi0:wi0 + WI_ROWS] * idx_scale


def _proj(x1, g, wt, wu, gq, gk, cos, sin, cosi, sini, *, batch, seq, idx_scale, tt=512):
    n, d = x1.shape
    rows = wt.shape[0]
    nt = seq // tt
    tok = lambda b, t: (b * nt + t, 0)
    const = lambda b, t: (0, 0)
    featmaj = lambda b, t: (b, 0, t)
    out_shape = (
        jax.ShapeDtypeStruct((batch, ATT_WIDTH, seq), BF16),
        jax.ShapeDtypeStruct((batch, N_HEADS // 2, seq, 2 * HEAD_DIM), BF16),
        jax.ShapeDtypeStruct((batch, ATT_WIDTH, seq), BF16),
        jax.ShapeDtypeStruct((batch, IDX_HEADS * IDX_DIM, seq), BF16),
        jax.ShapeDtypeStruct((batch, 2, seq, 2 * IDX_DIM), BF16),
        jax.ShapeDtypeStruct((batch, WI_ROWS, seq), F32),
        jax.ShapeDtypeStruct((n, POOL_WIDTH), F32),
    )
    out_specs = (
        pl.BlockSpec((1, ATT_WIDTH, tt), featmaj),
        pl.BlockSpec((1, N_HEADS // 2, tt, 2 * HEAD_DIM), lambda b, t: (b, 0, t, 0)),
        pl.BlockSpec((1, ATT_WIDTH, tt), featmaj),
        pl.BlockSpec((1, IDX_HEADS * IDX_DIM, tt), featmaj),
        pl.BlockSpec((1, 2, tt, 2 * IDX_DIM), lambda b, t: (b, 0, t, 0)),
        pl.BlockSpec((1, WI_ROWS, tt), featmaj),
        pl.BlockSpec((tt, POOL_WIDTH), tok),
    )
    in_specs = [
        pl.BlockSpec((tt, d), tok),
        pl.BlockSpec((1, d), const),
        pl.BlockSpec((rows, d), const),
        pl.BlockSpec((d, POOL_WIDTH), const),
        pl.BlockSpec((HEAD_DIM, 1), const),
        pl.BlockSpec((HEAD_DIM, 1), const),
        pl.BlockSpec((HEAD_DIM // 2, tt), lambda b, t: (0, t)),
        pl.BlockSpec((HEAD_DIM // 2, tt), lambda b, t: (0, t)),
        pl.BlockSpec((IDX_ROPE_DIM // 2, tt), lambda b, t: (0, t)),
        pl.BlockSpec((IDX_ROPE_DIM // 2, tt), lambda b, t: (0, t)),
    ]
    return pl.pallas_call(
        functools.partial(_proj_kernel, idx_scale=idx_scale),
        grid=(batch, nt),
        in_specs=in_specs,
        out_specs=out_specs,
        out_shape=out_shape,
        compiler_params=pltpu.CompilerParams(
            dimension_semantics=("parallel", "parallel"),
            vmem_limit_bytes=VMEM_LIMIT_BYTES),
        name="proj",
    )(x1, g, wt, wu, gq, gk, cos, sin, cosi, sini)


INT_MIN = np.int32(-2 ** 31)
COUNT_ROWS = 64


def _count(pred):
    ones = jnp.where(pred, 1, 0)
    rows, cols = ones.shape
    return ones.reshape(rows // COUNT_ROWS, COUNT_ROWS, cols).sum(axis=0).sum(axis=0, keepdims=True)


def _attn_kernel(qT_ref, kp_ref, vT_ref, qiT_ref, ki_ref, wiT_ref, o_ref,
                 key_ref, bias_ref, *, top_k, scale, first_block):
    seq = kp_ref.shape[2]
    tq = qT_ref.shape[2]
    q_pos = (first_block + pl.program_id(1)) * tq + lax.broadcasted_iota(jnp.int32, (seq, tq), 1)
    k_pos = lax.broadcasted_iota(jnp.int32, (seq, tq), 0)
    causal = k_pos <= q_pos

    wi = wiT_ref[0]
    score = jnp.zeros((seq, tq), F32)
    for hi in range(IDX_HEADS):
        pair, e = divmod(hi, 2)
        rel = jnp.dot(ki_ref[0, e], qiT_ref[0, 2 * IDX_DIM * pair:2 * IDX_DIM * (pair + 1), :],
                      preferred_element_type=F32)
        score = score + wi[hi:hi + 1, :] * jnp.maximum(rel, 0.0)
    score = jnp.where(causal, score + 0.0, -jnp.inf)

    bits = pltpu.bitcast(score, jnp.int32)
    key_ref[...] = jnp.where(bits < 0, bits ^ np.int32(0x7FFFFFFF), bits)

    count = _count

    def thr_step(i, tu):
        cand = tu | lax.shift_left(jnp.int32(1), 31 - i)
        cnt = count(key_ref[...] >= (cand ^ INT_MIN))
        return jnp.where(cnt >= top_k, cand, tu)

    tu = lax.fori_loop(0, 32, thr_step, jnp.zeros((1, tq), jnp.int32))
    thr = tu ^ INT_MIN
    keys = key_ref[...]
    need = top_k - count(keys > thr)
    tie_pos = jnp.where(keys == thr, k_pos, seq)
    key_ref[...] = tie_pos

    def tie_step(i, c):
        cand = c + lax.shift_left(jnp.int32(1), seq.bit_length() - 2 - i)
        cnt = count(key_ref[...] < cand)
        return jnp.where(cnt < need, cand, c)

    cut = lax.fori_loop(0, seq.bit_length() - 1, tie_step, jnp.zeros((1, tq), jnp.int32))
    causal_bias = jnp.where(causal, 0.0, -jnp.inf)
    bias_ref[...] = jnp.where(keys > thr, causal_bias,
                              jnp.where(tie_pos <= cut, causal_bias, -jnp.inf))

    half_rows = lax.broadcasted_iota(jnp.int32, (2 * HEAD_DIM, tq), 0) < HEAD_DIM
    outs = []
    for hd in range(N_HEADS):
        pair, e = divmod(hd, 2)
        qpair = qT_ref[0, 2 * HEAD_DIM * pair:2 * HEAD_DIM * (pair + 1), :]
        qz = jnp.where(half_rows if e == 0 else jnp.logical_not(half_rows), qpair,
                       jnp.zeros_like(qpair))
        logits = jnp.dot(kp_ref[0, pair], qz, preferred_element_type=F32) * scale + bias_ref[...]
        m = jnp.max(logits, axis=0, keepdims=True)
        p = jnp.exp(logits - m)
        l = jnp.sum(p, axis=0, keepdims=True)
        oT = jnp.dot(vT_ref[0, hd * HEAD_DIM:(hd + 1) * HEAD_DIM, :], p.astype(BF16),
                     preferred_element_type=F32)
        outs.append(oT / l)
    o_ref[0] = jnp.concatenate(outs, axis=0).T.astype(BF16)


def _attn_class(qT, kp, vT, qiT, ki, wiT, *, top_k, tq, first_block, n_blocks):
    batch = qT.shape[0]
    seq = (first_block + n_blocks) * tq
    qblk = lambda b, i: (b, 0, first_block + i)
    full3 = lambda b, i: (b, 0, 0)
    full4 = lambda b, i: (b, 0, 0, 0)
    return pl.pallas_call(
        functools.partial(_attn_kernel, top_k=top_k, scale=HEAD_DIM ** -0.5, first_block=first_block),
        grid=(batch, n_blocks),
        in_specs=[
            pl.BlockSpec((1, ATT_WIDTH, tq), qblk),
            pl.BlockSpec((1, N_HEADS // 2, seq, 2 * HEAD_DIM), full4),
            pl.BlockSpec((1, ATT_WIDTH, seq), full3),
            pl.BlockSpec((1, IDX_HEADS * IDX_DIM, tq), qblk),
            pl.BlockSpec((1, 2, seq, 2 * IDX_DIM), full4),
            pl.BlockSpec((1, WI_ROWS, tq), qblk),
        ],
        out_specs=pl.BlockSpec((1, tq, ATT_WIDTH), lambda b, i: (b, i, 0)),
        out_shape=jax.ShapeDtypeStruct((batch, n_blocks * tq, ATT_WIDTH), BF16),
        scratch_shapes=[pltpu.VMEM((seq, tq), jnp.int32), pltpu.VMEM((seq, tq), F32)],
        compiler_params=pltpu.CompilerParams(
            dimension_semantics=("parallel", "arbitrary"),
            vmem_limit_bytes=VMEM_LIMIT_BYTES),
        name=f"attn_k{seq}",
    )(qT, kp, vT, qiT, ki, wiT)


def _attn(qT, kp, vT, qiT, ki, wiT, *, top_k, tq=256, blocks_per_class=2):
    seq = qT.shape[2]
    parts = [
        _attn_class(qT, kp, vT, qiT, ki, wiT, top_k=top_k, tq=tq, first_block=fb,
                    n_blocks=blocks_per_class)
        for fb in range(0, seq // tq, blocks_per_class)
    ]
    return jnp.concatenate(parts, axis=1)


def _pool_kernel(u_ref, pw_ref, ps_ref, o_ref):
    seq = u_ref.shape[1]
    t = lax.broadcasted_iota(jnp.int32, (seq, POOL_CH), 0)
    for g, w in enumerate(POOL_WINDOWS):
        ug = u_ref[0, :, g * POOL_CH:(g + 1) * POOL_CH]
        s, sh = ug, 1
        while sh < w:
            s = s + jnp.where(t >= sh, pltpu.roll(s, sh, axis=0), 0.0)
            sh *= 2
        pooled = s / jnp.minimum(t + 1, w).astype(F32) - ug
        mixed = jnp.dot(pooled.astype(BF16), pw_ref[g], preferred_element_type=F32)
        o_ref[0, :, g * POOL_CH:(g + 1) * POOL_CH] = (
            mixed * ps_ref[:, g * POOL_CH:(g + 1) * POOL_CH]).astype(BF16)


def _pool(u, pw, ps):
    batch, seq, width = u.shape
    return pl.pallas_call(
        _pool_kernel,
        grid=(batch,),
        in_specs=[
            pl.BlockSpec((1, seq, width), lambda b: (b, 0, 0)),
            pl.BlockSpec(pw.shape, lambda b: (0, 0, 0)),
            pl.BlockSpec((1, width), lambda b: (0, 0)),
        ],
        out_specs=pl.BlockSpec((1, seq, width), lambda b: (b, 0, 0)),
        out_shape=jax.ShapeDtypeStruct((batch, seq, width), BF16),
        compiler_params=pltpu.CompilerParams(
            dimension_semantics=("parallel",), vmem_limit_bytes=VMEM_LIMIT_BYTES),
        name="pool",
    )(u, pw, ps)


def _outproj_kernel(x_ref, a_ref, p_ref, wa_ref, wp_ref, o_ref):
    o_ref[...] = (x_ref[...]
                  + jnp.dot(a_ref[...], wa_ref[...], preferred_element_type=F32)
                  + jnp.dot(p_ref[...], wp_ref[...], preferred_element_type=F32))


def _outproj(x1, attn, pooled, wa, wp, *, tm=512):
    n, d = x1.shape
    return pl.pallas_call(
        _outproj_kernel,
        grid=(n // tm,),
        in_specs=[
            pl.BlockSpec((tm, d), lambda i: (i, 0)),
            pl.BlockSpec((tm, attn.shape[1]), lambda i: (i, 0)),
            pl.BlockSpec((tm, pooled.shape[1]), lambda i: (i, 0)),
            pl.BlockSpec(wa.shape, lambda i: (0, 0)),
            pl.BlockSpec(wp.shape, lambda i: (0, 0)),
        ],
        out_specs=pl.BlockSpec((tm, d), lambda i: (i, 0)),
        out_shape=jax.ShapeDtypeStruct((n, d), F32),
        compiler_params=pltpu.CompilerParams(
            dimension_semantics=("parallel",), vmem_limit_bytes=VMEM_LIMIT_BYTES),
        name="outproj",
    )(x1, attn, pooled, wa, wp)


def _rope_tables(seq, rot_dim):
    half = rot_dim // 2
    inv = ROPE_THETA ** (-jnp.arange(half, dtype=F32) / half)
    ang = inv[:, None] * jnp.arange(seq, dtype=F32)[None, :]
    return jnp.cos(ang), jnp.sin(ang)


def kernel(x, ffn1_norm, ffn1_w_gate, ffn1_w_up, ffn1_w_down, mix_norm, w_in, q_norm, k_norm,
           pool_w, pool_scale, w_out, ffn2_norm, ffn2_w_gate, ffn2_w_up, ffn2_w_down):
    batch, seq, d = x.shape
    depth = w_in.shape[0]
    top_k = min(INDEX_TOPK, seq // 4)
    idx_scale = (IDX_HEADS ** -0.5) * (IDX_DIM ** -0.5)
    cos, sin = _rope_tables(seq, HEAD_DIM)
    cosi, sini = _rope_tables(seq, IDX_ROPE_DIM)
    n_feat = 3 * ATT_WIDTH + IDX_HEADS * IDX_DIM + IDX_DIM
    xf = x.reshape(batch * seq, d)

    for l in range(depth):
        xf = _ffn(xf, ffn1_norm[l][None], ffn1_w_gate[l].astype(BF16), ffn1_w_up[l].astype(BF16),
                  ffn1_w_down[l].astype(BF16))

        w = w_in[l]
        w_feat = jnp.pad(w[:, :n_feat + IDX_HEADS], ((0, 0), (0, WI_ROWS - IDX_HEADS)))
        wt = w_feat.T.astype(BF16)
        wu = w[:, n_feat + IDX_HEADS:].astype(BF16)
        qT, kp, vT, qiT, ki, wiT, u = _proj(
            xf, mix_norm[l][None], wt, wu, q_norm[l][:, None], k_norm[l][:, None],
            cos, sin, cosi, sini, batch=batch, seq=seq, idx_scale=idx_scale)

        attn = _attn(qT, kp, vT, qiT, ki, wiT, top_k=top_k)
        pooled = _pool(u.reshape(batch, seq, POOL_WIDTH), pool_w[l].astype(BF16), pool_scale[l][None])

        wo = w_out[l].astype(BF16)
        xf = _outproj(xf, attn.reshape(batch * seq, ATT_WIDTH), pooled.reshape(batch * seq, POOL_WIDTH),
                      wo[:ATT_WIDTH], wo[ATT_WIDTH:])

        xf = _ffn(xf, ffn2_norm[l][None], ffn2_w_gate[l].astype(BF16), ffn2_w_up[l].astype(BF16),
                  ffn2_w_down[l].astype(BF16))
    return xf.reshape(batch, seq, d)
```

```python
import functools
import math

import jax
import jax.numpy as jnp
import numpy as np
from jax import lax
from jax.experimental import pallas as pl
from jax.experimental.pallas import tpu as pltpu

HEAD_DIM = 64
N_HEADS = 8
ATT_WIDTH = N_HEADS * HEAD_DIM
IDX_HEADS = 4
IDX_DIM = 64
IDX_ROPE_DIM = 32
INDEX_TOPK = 256
POOL_WINDOWS = (2, 4, 8, 16)
POOL_CH = 128
POOL_WIDTH = POOL_CH * len(POOL_WINDOWS)
ROPE_THETA = 10000.0
NORM_EPS = 1e-6
WI_ROWS = 8
V_ROWS = HEAD_DIM + 16

VMEM_LIMIT_BYTES = 56 * 1024 * 1024

F32 = jnp.float32
BF16 = jnp.bfloat16
NT_DIMS = (((1,), (1,)), ((), ()))


def _rms(x, g):
    ms = jnp.mean(x * x, axis=-1, keepdims=True)
    return x * lax.rsqrt(ms + NORM_EPS) * g


def _ffn_kernel(x_ref, g_ref, wg_ref, wu_ref, wd_ref, o_ref, act_ref, *, tf):
    x = x_ref[...]
    h = _rms(x, g_ref[...]).astype(BF16)
    for c in range(wg_ref.shape[1] // tf):
        cols = slice(c * tf, (c + 1) * tf)
        gate = jnp.dot(h, wg_ref[:, cols], preferred_element_type=F32)
        up = jnp.dot(h, wu_ref[:, cols], preferred_element_type=F32)
        act_ref[:, cols] = (gate * jax.nn.sigmoid(gate) * up).astype(BF16)
    o_ref[...] = x + 0.5 * jnp.dot(act_ref[...], wd_ref[...], preferred_element_type=F32)


def _resident(shape):
    return pl.BlockSpec(shape, lambda *_: (0,) * len(shape), pipeline_mode=pl.Buffered(1))


def _ffn(x, g, wg, wu, wd, *, tm=512, tf=256):
    n, d = x.shape
    f = wg.shape[1]
    return pl.pallas_call(
        functools.partial(_ffn_kernel, tf=tf),
        grid=(n // tm,),
        in_specs=[
            pl.BlockSpec((tm, d), lambda i: (i, 0)),
            _resident((1, d)),
            _resident((d, f)),
            _resident((d, f)),
            _resident((f, d)),
        ],
        out_specs=pl.BlockSpec((tm, d), lambda i: (i, 0)),
        out_shape=jax.ShapeDtypeStruct((n, d), F32),
        scratch_shapes=[pltpu.VMEM((tm, f), BF16)],
        compiler_params=pltpu.CompilerParams(
            dimension_semantics=("parallel",),
            vmem_limit_bytes=VMEM_LIMIT_BYTES),
        name="ffn",
    )(x, g, wg, wu, wd)


def _rope_rows(x, cos, sin, half):
    x1, x2 = x[:half], x[half:2 * half]
    return x1 * cos - x2 * sin, x2 * cos + x1 * sin


def _proj_kernel(x_ref, g_ref, wt_ref, wu_ref, gq_ref, gk_ref, cos_ref, sin_ref,
                 cosi_ref, sini_ref,
                 qT_ref, kp_ref, vT_ref, qiT_ref, ki_ref, wiT_ref, u_ref, *, idx_scale, q_scale):
    h = _rms(x_ref[...], g_ref[...]).astype(BF16)
    pT = lax.dot_general(wt_ref[...], h, NT_DIMS, preferred_element_type=F32)
    u_ref[...] = jnp.dot(h, wu_ref[...], preferred_element_type=F32)

    cos, sin = cos_ref[...], sin_ref[...]
    gq, gk = gq_ref[...], gk_ref[...]
    half = HEAD_DIM // 2

    def head_norm_rope(rows, gain):
        ms = jnp.mean(rows * rows, axis=0, keepdims=True)
        return _rope_rows(rows * lax.rsqrt(ms + NORM_EPS) * gain, cos, sin, half)

    for hd in range(N_HEADS):
        r0 = hd * HEAD_DIM
        a, b = head_norm_rope(pT[r0:r0 + HEAD_DIM], gq)
        qT_ref[0, r0:r0 + half, :] = (a * q_scale).astype(BF16)
        qT_ref[0, r0 + half:r0 + HEAD_DIM, :] = (b * q_scale).astype(BF16)

    k0 = ATT_WIDTH
    for pair in range(N_HEADS // 2):
        parts = []
        for e in range(2):
            r0 = k0 + (2 * pair + e) * HEAD_DIM
            parts.extend(head_norm_rope(pT[r0:r0 + HEAD_DIM], gk))
        kp_ref[0, pair] = jnp.concatenate(parts, axis=0).T.astype(BF16)

    v0 = 2 * ATT_WIDTH
    tt = pT.shape[1]
    ones_rows = jnp.where(lax.broadcasted_iota(jnp.int32, (V_ROWS - HEAD_DIM, tt), 0) == 0,
                          1.0, 0.0).astype(BF16)
    for hd in range(N_HEADS):
        r0 = v0 + hd * HEAD_DIM
        vT_ref[0, hd, :HEAD_DIM, :] = pT[r0:r0 + HEAD_DIM].astype(BF16)
        vT_ref[0, hd, HEAD_DIM:, :] = ones_rows

    cosi, sini = cosi_ref[...], sini_ref[...]
    ihalf = IDX_ROPE_DIM // 2
    qi0 = 3 * ATT_WIDTH
    for hi in range(IDX_HEADS):
        r0 = qi0 + hi * IDX_DIM
        a, b = _rope_rows(pT[r0:r0 + IDX_DIM], cosi, sini, ihalf)
        qiT_ref[0, r0 - qi0:r0 - qi0 + ihalf, :] = a.astype(BF16)
        qiT_ref[0, r0 - qi0 + ihalf:r0 - qi0 + IDX_ROPE_DIM, :] = b.astype(BF16)
        qiT_ref[0, r0 - qi0 + IDX_ROPE_DIM:r0 - qi0 + IDX_DIM, :] = (
            pT[r0 + IDX_ROPE_DIM:r0 + IDX_DIM].astype(BF16))

    ki0 = qi0 + IDX_HEADS * IDX_DIM
    a, b = _rope_rows(pT[ki0:ki0 + IDX_DIM], cosi, sini, ihalf)
    ki = jnp.concatenate([a, b, pT[ki0 + IDX_ROPE_DIM:ki0 + IDX_DIM]], axis=0)
    zeros = jnp.zeros_like(ki)
    ki_ref[0, 0] = jnp.concatenate([ki, zeros], axis=0).T.astype(BF16)
    ki_ref[0, 1] = jnp.concatenate([zeros, ki], axis=0).T.astype(BF16)

    wi0 = ki0 + IDX_DIM
    wiT_ref[0] = pT[wi0:wi0 + WI_ROWS] * idx_scale


def _proj(x1, g, wt, wu, gq, gk, cos, sin, cosi, sini, *, batch, seq, idx_scale, tt=512):
    n, d = x1.shape
    rows = wt.shape[0]
    nt = seq // tt
    tok = lambda b, t: (b * nt + t, 0)
    const = lambda b, t: (0, 0)
    featmaj = lambda b, t: (b, 0, t)
    out_shape = (
        jax.ShapeDtypeStruct((batch, ATT_WIDTH, seq), BF16),
        jax.ShapeDtypeStruct((batch, N_HEADS // 2, seq, 2 * HEAD_DIM), BF16),
        jax.ShapeDtypeStruct((batch, N_HEADS, V_ROWS, seq), BF16),
        jax.ShapeDtypeStruct((batch, IDX_HEADS * IDX_DIM, seq), BF16),
        jax.ShapeDtypeStruct((batch, 2, seq, 2 * IDX_DIM), BF16),
        jax.ShapeDtypeStruct((batch, WI_ROWS, seq), F32),
        jax.ShapeDtypeStruct((n, POOL_WIDTH), F32),
    )
    out_specs = (
        pl.BlockSpec((1, ATT_WIDTH, tt), featmaj),
        pl.BlockSpec((1, N_HEADS // 2, tt, 2 * HEAD_DIM), lambda b, t: (b, 0, t, 0)),
        pl.BlockSpec((1, N_HEADS, V_ROWS, tt), lambda b, t: (b, 0, 0, t)),
        pl.BlockSpec((1, IDX_HEADS * IDX_DIM, tt), featmaj),
        pl.BlockSpec((1, 2, tt, 2 * IDX_DIM), lambda b, t: (b, 0, t, 0)),
        pl.BlockSpec((1, WI_ROWS, tt), featmaj),
        pl.BlockSpec((tt, POOL_WIDTH), tok),
    )
    in_specs = [
        pl.BlockSpec((tt, d), tok),
        pl.BlockSpec((1, d), const),
        pl.BlockSpec((rows, d), const),
        pl.BlockSpec((d, POOL_WIDTH), const),
        pl.BlockSpec((HEAD_DIM, 1), const),
        pl.BlockSpec((HEAD_DIM, 1), const),
        pl.BlockSpec((HEAD_DIM // 2, tt), lambda b, t: (0, t)),
        pl.BlockSpec((HEAD_DIM // 2, tt), lambda b, t: (0, t)),
        pl.BlockSpec((IDX_ROPE_DIM // 2, tt), lambda b, t: (0, t)),
        pl.BlockSpec((IDX_ROPE_DIM // 2, tt), lambda b, t: (0, t)),
    ]
    return pl.pallas_call(
        functools.partial(_proj_kernel, idx_scale=idx_scale,
                          q_scale=HEAD_DIM ** -0.5 * math.log2(math.e)),
        grid=(batch, nt),
        in_specs=in_specs,
        out_specs=out_specs,
        out_shape=out_shape,
        compiler_params=pltpu.CompilerParams(
            dimension_semantics=("parallel", "parallel"),
            vmem_limit_bytes=VMEM_LIMIT_BYTES),
        name="proj",
    )(x1, g, wt, wu, gq, gk, cos, sin, cosi, sini)


INT_MIN = np.int32(-2 ** 31)
COUNT_ROWS = 64


def _count(pred):
    ones = jnp.where(pred, 1, 0)
    rows, cols = ones.shape
    return ones.reshape(rows // COUNT_ROWS, COUNT_ROWS, cols).sum(axis=0).sum(axis=0, keepdims=True)


def _attn_kernel(qT_ref, kp_ref, vT_ref, qiT_ref, ki_ref, wiT_ref, o_ref,
                 key_ref, bias_ref, *, top_k, first_block):
    seq = kp_ref.shape[2]
    tq = qT_ref.shape[2]
    q_pos = (first_block + pl.program_id(1)) * tq + lax.broadcasted_iota(jnp.int32, (seq, tq), 1)
    k_pos = lax.broadcasted_iota(jnp.int32, (seq, tq), 0)
    causal = k_pos <= q_pos

    wi = wiT_ref[0]
    score = jnp.zeros((seq, tq), F32)
    for hi in range(IDX_HEADS):
        pair, e = divmod(hi, 2)
        rel = jnp.dot(ki_ref[0, e], qiT_ref[0, 2 * IDX_DIM * pair:2 * IDX_DIM * (pair + 1), :],
                      preferred_element_type=F32)
        score = score + wi[hi:hi + 1, :] * jnp.maximum(rel, 0.0)
    score = jnp.where(causal, score + 0.0, -jnp.inf)

    bits = pltpu.bitcast(score, jnp.int32)
    key_ref[...] = jnp.where(bits < 0, bits ^ np.int32(0x7FFFFFFF), bits)

    count = _count

    def thr_step(i, tu):
        cand = tu | lax.shift_left(jnp.int32(1), 31 - i)
        cnt = count(key_ref[...] >= (cand ^ INT_MIN))
        return jnp.where(cnt >= top_k, cand, tu)

    tu = lax.fori_loop(0, 32, thr_step, jnp.zeros((1, tq), jnp.int32))
    thr = tu ^ INT_MIN
    keys = key_ref[...]
    need = top_k - count(keys > thr)
    tie_pos = jnp.where(keys == thr, k_pos, seq)

    def tie_search():
        key_ref[...] = tie_pos

        def tie_step(i, c):
            cand = c + lax.shift_left(jnp.int32(1), seq.bit_length() - 2 - i)
            cnt = count(key_ref[...] < cand)
            return jnp.where(cnt < need, cand, c)

        return lax.fori_loop(0, seq.bit_length() - 1, tie_step, jnp.zeros((1, tq), jnp.int32))

    surplus = jnp.max(count(keys >= thr)) > top_k
    cut = lax.cond(surplus, tie_search, lambda: jnp.full((1, tq), seq, jnp.int32))
    causal_bias = jnp.where(causal, 0.0, -jnp.inf)
    bias_ref[...] = jnp.where(keys > thr, causal_bias,
                              jnp.where(tie_pos <= cut, causal_bias, -jnp.inf)).astype(BF16)

    half_rows = lax.broadcasted_iota(jnp.int32, (2 * HEAD_DIM, tq), 0) < HEAD_DIM
    outs = []
    for hd in range(N_HEADS):
        pair, e = divmod(hd, 2)
        qpair = qT_ref[0, 2 * HEAD_DIM * pair:2 * HEAD_DIM * (pair + 1), :]
        qz = jnp.where(half_rows if e == 0 else jnp.logical_not(half_rows), qpair,
                       jnp.zeros_like(qpair))
        logits = jnp.dot(kp_ref[0, pair], qz, preferred_element_type=F32).astype(BF16) + bias_ref[...]
        p = jnp.exp2(logits - jnp.max(logits, axis=0, keepdims=True))
        o_aug = jnp.dot(vT_ref[0, hd], p, preferred_element_type=F32)
        outs.append(o_aug[:HEAD_DIM] / o_aug[HEAD_DIM:HEAD_DIM + 1])
    o_ref[0] = jnp.concatenate(outs, axis=0).T.astype(BF16)


def _attn_class(qT, kp, vT, qiT, ki, wiT, *, top_k, tq, first_block, n_blocks):
    batch = qT.shape[0]
    seq = (first_block + n_blocks) * tq
    qblk = lambda b, i: (b, 0, first_block + i)
    full3 = lambda b, i: (b, 0, 0)
    full4 = lambda b, i: (b, 0, 0, 0)
    return pl.pallas_call(
        functools.partial(_attn_kernel, top_k=top_k, first_block=first_block),
        grid=(batch, n_blocks),
        in_specs=[
            pl.BlockSpec((1, ATT_WIDTH, tq), qblk),
            pl.BlockSpec((1, N_HEADS // 2, seq, 2 * HEAD_DIM), full4),
            pl.BlockSpec((1, N_HEADS, V_ROWS, seq), full4),
            pl.BlockSpec((1, IDX_HEADS * IDX_DIM, tq), qblk),
            pl.BlockSpec((1, 2, seq, 2 * IDX_DIM), full4),
            pl.BlockSpec((1, WI_ROWS, tq), qblk),
        ],
        out_specs=pl.BlockSpec((1, tq, ATT_WIDTH), lambda b, i: (b, i, 0)),
        out_shape=jax.ShapeDtypeStruct((batch, n_blocks * tq, ATT_WIDTH), BF16),
        scratch_shapes=[pltpu.VMEM((seq, tq), jnp.int32), pltpu.VMEM((seq, tq), BF16)],
        compiler_params=pltpu.CompilerParams(
            dimension_semantics=("parallel", "arbitrary"),
            vmem_limit_bytes=VMEM_LIMIT_BYTES),
        name=f"attn_k{seq}",
    )(qT, kp, vT, qiT, ki, wiT)


def _attn(qT, kp, vT, qiT, ki, wiT, *, top_k, tq=256, blocks_per_class=2):
    seq = qT.shape[2]
    parts = [
        _attn_class(qT, kp, vT, qiT, ki, wiT, top_k=top_k, tq=tq, first_block=fb,
                    n_blocks=blocks_per_class)
        for fb in range(0, seq // tq, blocks_per_class)
    ]
    return jnp.concatenate(parts, axis=1)


def _pool_kernel(u_ref, pw_ref, ps_ref, o_ref):
    seq = u_ref.shape[1]
    t = lax.broadcasted_iota(jnp.int32, (seq, POOL_CH), 0)
    for g, w in enumerate(POOL_WINDOWS):
        ug = u_ref[0, :, g * POOL_CH:(g + 1) * POOL_CH]
        s, sh = ug, 1
        while sh < w:
            s = s + jnp.where(t >= sh, pltpu.roll(s, sh, axis=0), 0.0)
            sh *= 2
        pooled = s / jnp.minimum(t + 1, w).astype(F32) - ug
        mixed = jnp.dot(pooled.astype(BF16), pw_ref[g], preferred_element_type=F32)
        o_ref[0, :, g * POOL_CH:(g + 1) * POOL_CH] = (
            mixed * ps_ref[:, g * POOL_CH:(g + 1) * POOL_CH]).astype(BF16)


def _pool(u, pw, ps):
    batch, seq, width = u.shape
    return pl.pallas_call(
        _pool_kernel,
        grid=(batch,),
        in_specs=[
            pl.BlockSpec((1, seq, width), lambda b: (b, 0, 0)),
            pl.BlockSpec(pw.shape, lambda b: (0, 0, 0)),
            pl.BlockSpec((1, width), lambda b: (0, 0)),
        ],
        out_specs=pl.BlockSpec((1, seq, width), lambda b: (b, 0, 0)),
        out_shape=jax.ShapeDtypeStruct((batch, seq, width), BF16),
        compiler_params=pltpu.CompilerParams(
            dimension_semantics=("parallel",), vmem_limit_bytes=VMEM_LIMIT_BYTES),
        name="pool",
    )(u, pw, ps)


def _outproj_kernel(x_ref, a_ref, p_ref, wa_ref, wp_ref, o_ref):
    o_ref[...] = (x_ref[...]
                  + jnp.dot(a_ref[...], wa_ref[...], preferred_element_type=F32)
                  + jnp.dot(p_ref[...], wp_ref[...], preferred_element_type=F32))


def _outproj(x1, attn, pooled, wa, wp, *, tm=512):
    n, d = x1.shape
    return pl.pallas_call(
        _outproj_kernel,
        grid=(n // tm,),
        in_specs=[
            pl.BlockSpec((tm, d), lambda i: (i, 0)),
            pl.BlockSpec((tm, attn.shape[1]), lambda i: (i, 0)),
            pl.BlockSpec((tm, pooled.shape[1]), lambda i: (i, 0)),
            pl.BlockSpec(wa.shape, lambda i: (0, 0)),
            pl.BlockSpec(wp.shape, lambda i: (0, 0)),
        ],
        out_specs=pl.BlockSpec((tm, d), lambda i: (i, 0)),
        out_shape=jax.ShapeDtypeStruct((n, d), F32),
        compiler_params=pltpu.CompilerParams(
            dimension_semantics=("parallel",), vmem_limit_bytes=VMEM_LIMIT_BYTES),
        name="outproj",
    )(x1, attn, pooled, wa, wp)


def _rope_tables(seq, rot_dim):
    half = rot_dim // 2
    inv = ROPE_THETA ** (-jnp.arange(half, dtype=F32) / half)
    ang = inv[:, None] * jnp.arange(seq, dtype=F32)[None, :]
    return jnp.cos(ang), jnp.sin(ang)


def kernel(x, ffn1_norm, ffn1_w_gate, ffn1_w_up, ffn1_w_down, mix_norm, w_in, q_norm, k_norm,
           pool_w, pool_scale, w_out, ffn2_norm, ffn2_w_gate, ffn2_w_up, ffn2_w_down):
    batch, seq, d = x.shape
    depth = w_in.shape[0]
    top_k = min(INDEX_TOPK, seq // 4)
    idx_scale = (IDX_HEADS ** -0.5) * (IDX_DIM ** -0.5)
    cos, sin = _rope_tables(seq, HEAD_DIM)
    cosi, sini = _rope_tables(seq, IDX_ROPE_DIM)
    n_feat = 3 * ATT_WIDTH + IDX_HEADS * IDX_DIM + IDX_DIM
    xf = x.reshape(batch * seq, d)

    for l in range(depth):
        xf = _ffn(xf, ffn1_norm[l][None], ffn1_w_gate[l].astype(BF16), ffn1_w_up[l].astype(BF16),
                  ffn1_w_down[l].astype(BF16))

        w = w_in[l]
        w_feat = jnp.pad(w[:, :n_feat + IDX_HEADS], ((0, 0), (0, WI_ROWS - IDX_HEADS)))
        wt = w_feat.T.astype(BF16)
        wu = w[:, n_feat + IDX_HEADS:].astype(BF16)
        qT, kp, vT, qiT, ki, wiT, u = _proj(
            xf, mix_norm[l][None], wt, wu, q_norm[l][:, None], k_norm[l][:, None],
            cos, sin, cosi, sini, batch=batch, seq=seq, idx_scale=idx_scale)

        attn = _attn(qT, kp, vT, qiT, ki, wiT, top_k=top_k)
        pooled = _pool(u.reshape(batch, seq, POOL_WIDTH), pool_w[l].astype(BF16), pool_scale[l][None])

        wo = w_out[l].astype(BF16)
        xf = _outproj(xf, attn.reshape(batch * seq, ATT_WIDTH), pooled.reshape(batch * seq, POOL_WIDTH),
                      wo[:ATT_WIDTH], wo[ATT_WIDTH:])

        xf = _ffn(xf, ffn2_norm[l][None], ffn2_w_gate[l].astype(BF16), ffn2_w_up[l].astype(BF16),
                  ffn2_w_down[l].astype(BF16))
    return xf.reshape(batch, seq, d)
```

```python
import functools
import math

import jax
import jax.numpy as jnp
import numpy as np
from jax import lax
from jax.experimental import pallas as pl
from jax.experimental.pallas import tpu as pltpu

HEAD_DIM = 64
N_HEADS = 8
ATT_WIDTH = N_HEADS * HEAD_DIM
IDX_HEADS = 4
IDX_DIM = 64
IDX_ROPE_DIM = 32
INDEX_TOPK = 256
POOL_WINDOWS = (2, 4, 8, 16)
POOL_CH = 128
POOL_WIDTH = POOL_CH * len(POOL_WINDOWS)
ROPE_THETA = 10000.0
NORM_EPS = 1e-6
WI_ROWS = 8
V_ROWS = HEAD_DIM + 16

VMEM_LIMIT_BYTES = 56 * 1024 * 1024

F32 = jnp.float32
BF16 = jnp.bfloat16
NT_DIMS = (((1,), (1,)), ((), ()))


def _rms(x, g):
    ms = jnp.mean(x * x, axis=-1, keepdims=True)
    return x * lax.rsqrt(ms + NORM_EPS) * g


def _ffn_kernel(x_ref, g_ref, wg_ref, wu_ref, wd_ref, o_ref, act_ref, *, tf):
    x = x_ref[...]
    h = _rms(x, g_ref[...]).astype(BF16)
    for c in range(wg_ref.shape[1] // tf):
        cols = slice(c * tf, (c + 1) * tf)
        gate = jnp.dot(h, wg_ref[:, cols], preferred_element_type=F32)
        up = jnp.dot(h, wu_ref[:, cols], preferred_element_type=F32)
        act_ref[:, cols] = (gate * jax.nn.sigmoid(gate) * up).astype(BF16)
    o_ref[...] = x + 0.5 * jnp.dot(act_ref[...], wd_ref[...], preferred_element_type=F32)


def _resident(shape):
    return pl.BlockSpec(shape, lambda *_: (0,) * len(shape), pipeline_mode=pl.Buffered(1))


def _ffn(x, g, wg, wu, wd, *, tm=512, tf=256):
    n, d = x.shape
    f = wg.shape[1]
    return pl.pallas_call(
        functools.partial(_ffn_kernel, tf=tf),
        grid=(n // tm,),
        in_specs=[
            pl.BlockSpec((tm, d), lambda i: (i, 0)),
            _resident((1, d)),
            _resident((d, f)),
            _resident((d, f)),
            _resident((f, d)),
        ],
        out_specs=pl.BlockSpec((tm, d), lambda i: (i, 0)),
        out_shape=jax.ShapeDtypeStruct((n, d), F32),
        scratch_shapes=[pltpu.VMEM((tm, f), BF16)],
        compiler_params=pltpu.CompilerParams(
            dimension_semantics=("parallel",),
            vmem_limit_bytes=VMEM_LIMIT_BYTES),
        name="ffn",
    )(x, g, wg, wu, wd)


def _rope_rows(x, cos, sin, half):
    x1, x2 = x[:half], x[half:2 * half]
    return x1 * cos - x2 * sin, x2 * cos + x1 * sin


def _proj_kernel(x_ref, g_ref, wt_ref, wu_ref, gq_ref, gk_ref, cos_ref, sin_ref,
                 cosi_ref, sini_ref,
                 qT_ref, kp_ref, vT_ref, qiT_ref, ki_ref, wiT_ref, u_ref, *, idx_scale, q_scale):
    h = _rms(x_ref[...], g_ref[...]).astype(BF16)
    pT = lax.dot_general(wt_ref[...], h, NT_DIMS, preferred_element_type=F32)
    u_ref[...] = jnp.dot(h, wu_ref[...], preferred_element_type=F32)

    cos, sin = cos_ref[...], sin_ref[...]
    gq, gk = gq_ref[...], gk_ref[...]
    half = HEAD_DIM // 2

    def head_norm_rope(rows, gain):
        ms = jnp.mean(rows * rows, axis=0, keepdims=True)
        return _rope_rows(rows * lax.rsqrt(ms + NORM_EPS) * gain, cos, sin, half)

    for hd in range(N_HEADS):
        r0 = hd * HEAD_DIM
        a, b = head_norm_rope(pT[r0:r0 + HEAD_DIM], gq)
        qT_ref[0, r0:r0 + half, :] = (a * q_scale).astype(BF16)
        qT_ref[0, r0 + half:r0 + HEAD_DIM, :] = (b * q_scale).astype(BF16)

    k0 = ATT_WIDTH
    for pair in range(N_HEADS // 2):
        parts = []
        for e in range(2):
            r0 = k0 + (2 * pair + e) * HEAD_DIM
            parts.extend(head_norm_rope(pT[r0:r0 + HEAD_DIM], gk))
        kp_ref[0, pair] = jnp.concatenate(parts, axis=0).T.astype(BF16)

    v0 = 2 * ATT_WIDTH
    tt = pT.shape[1]
    ones_rows = jnp.where(lax.broadcasted_iota(jnp.int32, (V_ROWS - HEAD_DIM, tt), 0) == 0,
                          1.0, 0.0).astype(BF16)
    for hd in range(N_HEADS):
        r0 = v0 + hd * HEAD_DIM
        vT_ref[0, hd, :HEAD_DIM, :] = pT[r0:r0 + HEAD_DIM].astype(BF16)
        vT_ref[0, hd, HEAD_DIM:, :] = ones_rows

    cosi, sini = cosi_ref[...], sini_ref[...]
    ihalf = IDX_ROPE_DIM // 2
    qi0 = 3 * ATT_WIDTH
    for hi in range(IDX_HEADS):
        r0 = qi0 + hi * IDX_DIM
        a, b = _rope_rows(pT[r0:r0 + IDX_DIM], cosi, sini, ihalf)
        qiT_ref[0, r0 - qi0:r0 - qi0 + ihalf, :] = a.astype(BF16)
        qiT_ref[0, r0 - qi0 + ihalf:r0 - qi0 + IDX_ROPE_DIM, :] = b.astype(BF16)
        qiT_ref[0, r0 - qi0 + IDX_ROPE_DIM:r0 - qi0 + IDX_DIM, :] = (
            pT[r0 + IDX_ROPE_DIM:r0 + IDX_DIM].astype(BF16))

    ki0 = qi0 + IDX_HEADS * IDX_DIM
    a, b = _rope_rows(pT[ki0:ki0 + IDX_DIM], cosi, sini, ihalf)
    ki = jnp.concatenate([a, b, pT[ki0 + IDX_ROPE_DIM:ki0 + IDX_DIM]], axis=0)
    zeros = jnp.zeros_like(ki)
    ki_ref[0, 0] = jnp.concatenate([ki, zeros], axis=0).T.astype(BF16)
    ki_ref[0, 1] = jnp.concatenate([zeros, ki], axis=0).T.astype(BF16)

    wi0 = ki0 + IDX_DIM
    wiT_ref[0] = pT[wi0:wi0 + WI_ROWS] * idx_scale


def _proj(x1, g, wt, wu, gq, gk, cos, sin, cosi, sini, *, batch, seq, idx_scale, tt=512):
    n, d = x1.shape
    rows = wt.shape[0]
    nt = seq // tt
    tok = lambda b, t: (b * nt + t, 0)
    const = lambda b, t: (0, 0)
    featmaj = lambda b, t: (b, 0, t)
    out_shape = (
        jax.ShapeDtypeStruct((batch, ATT_WIDTH, seq), BF16),
        jax.ShapeDtypeStruct((batch, N_HEADS // 2, seq, 2 * HEAD_DIM), BF16),
        jax.ShapeDtypeStruct((batch, N_HEADS, V_ROWS, seq), BF16),
        jax.ShapeDtypeStruct((batch, IDX_HEADS * IDX_DIM, seq), BF16),
        jax.ShapeDtypeStruct((batch, 2, seq, 2 * IDX_DIM), BF16),
        jax.ShapeDtypeStruct((batch, WI_ROWS, seq), F32),
        jax.ShapeDtypeStruct((n, POOL_WIDTH), F32),
    )
    out_specs = (
        pl.BlockSpec((1, ATT_WIDTH, tt), featmaj),
        pl.BlockSpec((1, N_HEADS // 2, tt, 2 * HEAD_DIM), lambda b, t: (b, 0, t, 0)),
        pl.BlockSpec((1, N_HEADS, V_ROWS, tt), lambda b, t: (b, 0, 0, t)),
        pl.BlockSpec((1, IDX_HEADS * IDX_DIM, tt), featmaj),
        pl.BlockSpec((1, 2, tt, 2 * IDX_DIM), lambda b, t: (b, 0, t, 0)),
        pl.BlockSpec((1, WI_ROWS, tt), featmaj),
        pl.BlockSpec((tt, POOL_WIDTH), tok),
    )
    in_specs = [
        pl.BlockSpec((tt, d), tok),
        pl.BlockSpec((1, d), const),
        pl.BlockSpec((rows, d), const),
        pl.BlockSpec((d, POOL_WIDTH), const),
        pl.BlockSpec((HEAD_DIM, 1), const),
        pl.BlockSpec((HEAD_DIM, 1), const),
        pl.BlockSpec((HEAD_DIM // 2, tt), lambda b, t: (0, t)),
        pl.BlockSpec((HEAD_DIM // 2, tt), lambda b, t: (0, t)),
        pl.BlockSpec((IDX_ROPE_DIM // 2, tt), lambda b, t: (0, t)),
        pl.BlockSpec((IDX_ROPE_DIM // 2, tt), lambda b, t: (0, t)),
    ]
    return pl.pallas_call(
        functools.partial(_proj_kernel, idx_scale=idx_scale,
                          q_scale=HEAD_DIM ** -0.5 * math.log2(math.e)),
        grid=(batch, nt),
        in_specs=in_specs,
        out_specs=out_specs,
        out_shape=out_shape,
        compiler_params=pltpu.CompilerParams(
            dimension_semantics=("parallel", "parallel"),
            vmem_limit_bytes=VMEM_LIMIT_BYTES),
        name="proj",
    )(x1, g, wt, wu, gq, gk, cos, sin, cosi, sini)


INT_MIN = np.int32(-2 ** 31)
I16 = jnp.int16
I16_MIN = -2 ** 15
COUNT_ROWS = 128


def _count(pred):
    ones = jnp.where(pred, jnp.bfloat16(1), jnp.bfloat16(0))
    rows, cols = ones.shape
    groups = ones.reshape(rows // COUNT_ROWS, COUNT_ROWS, cols)
    part = groups[0]
    for i in range(1, rows // COUNT_ROWS):
        part = part + groups[i]
    return part.astype(F32).sum(axis=0, keepdims=True).astype(jnp.int32)


def _to_i16(pattern):
    return (pattern + I16_MIN).astype(I16)


def _kth_largest_16(ref, k, n_bits=16):
    def step(i, found):
        cand = found | lax.shift_left(jnp.int32(1), n_bits - 1 - i)
        cnt = _count(ref[...] >= _to_i16(cand))
        return jnp.where(cnt >= k, cand, found)

    return _to_i16(lax.fori_loop(0, n_bits, step, jnp.zeros_like(k)))


def _topk_bias(score, causal, k_pos, hi_ref, lo_ref, top_k):
    seq, tq = score.shape
    bits = pltpu.bitcast(score, jnp.int32)
    key = jnp.where(bits < 0, INT_MIN - bits, bits)
    hi_ref[...] = lax.shift_right_arithmetic(key, 16).astype(I16)
    lo_ref[...] = ((key & 0xFFFF) + I16_MIN).astype(I16)

    h16 = _kth_largest_16(hi_ref, jnp.full((1, tq), top_k, jnp.int32))
    hi = hi_ref[...]
    k_lo = top_k - _count(hi > h16)
    lo_ref[...] = jnp.where(hi == h16, lo_ref[...], jnp.int16(I16_MIN))
    l16 = _kth_largest_16(lo_ref, k_lo)
    lo = lo_ref[...]
    need = k_lo - _count(lo > l16)

    tie_pos = jnp.where(hi == h16, jnp.where(lo == l16, k_pos.astype(I16), jnp.int16(seq)),
                        jnp.int16(seq))
    lo_ref[...] = tie_pos

    def tie_step(i, c):
        cand = c + lax.shift_left(jnp.int32(1), seq.bit_length() - 2 - i)
        cnt = _count(lo_ref[...] < cand.astype(I16))
        return jnp.where(cnt < need, cand, c)

    cut = lax.fori_loop(0, seq.bit_length() - 1, tie_step, jnp.zeros((1, tq), jnp.int32)).astype(I16)
    causal_bias = jnp.where(causal, 0.0, -jnp.inf).astype(BF16)
    masked = jnp.bfloat16(-jnp.inf)
    return jnp.where(hi > h16, causal_bias,
                     jnp.where(lo > l16, causal_bias,
                               jnp.where(tie_pos <= cut, causal_bias, masked)))


def _attn_kernel(qT_ref, kp_ref, vT_ref, qiT_ref, ki_ref, wiT_ref, o_ref,
                 hi_ref, lo_ref, bias_ref, *, top_k, first_block):
    seq = kp_ref.shape[2]
    tq = qT_ref.shape[2]
    q_pos = (first_block + pl.program_id(1)) * tq + lax.broadcasted_iota(jnp.int32, (seq, tq), 1)
    k_pos = lax.broadcasted_iota(jnp.int32, (seq, tq), 0)
    causal = k_pos <= q_pos

    wi = wiT_ref[0]
    score = jnp.zeros((seq, tq), F32)
    for hi in range(IDX_HEADS):
        pair, e = divmod(hi, 2)
        rel = jnp.dot(ki_ref[0, e], qiT_ref[0, 2 * IDX_DIM * pair:2 * IDX_DIM * (pair + 1), :],
                      preferred_element_type=F32)
        score = score + wi[hi:hi + 1, :] * jnp.maximum(rel, 0.0)
    score = jnp.where(causal, score, -jnp.inf)
    bias_ref[...] = _topk_bias(score, causal, k_pos, hi_ref, lo_ref, top_k)

    half_rows = lax.broadcasted_iota(jnp.int32, (2 * HEAD_DIM, tq), 0) < HEAD_DIM
    outs = []
    for hd in range(N_HEADS):
        pair, e = divmod(hd, 2)
        qpair = qT_ref[0, 2 * HEAD_DIM * pair:2 * HEAD_DIM * (pair + 1), :]
        qz = jnp.where(half_rows if e == 0 else jnp.logical_not(half_rows), qpair,
                       jnp.zeros_like(qpair))
        logits = jnp.dot(kp_ref[0, pair], qz, preferred_element_type=F32).astype(BF16) + bias_ref[...]
        p = jnp.exp2(logits - jnp.max(logits, axis=0, keepdims=True))
        o_aug = jnp.dot(vT_ref[0, hd], p, preferred_element_type=F32)
        outs.append(o_aug[:HEAD_DIM] / o_aug[HEAD_DIM:HEAD_DIM + 1])
    o_ref[0] = jnp.concatenate(outs, axis=0).T.astype(BF16)


def _attn_class(qT, kp, vT, qiT, ki, wiT, *, top_k, tq, first_block, n_blocks):
    batch = qT.shape[0]
    seq = (first_block + n_blocks) * tq
    qblk = lambda b, i: (b, 0, first_block + i)
    full3 = lambda b, i: (b, 0, 0)
    full4 = lambda b, i: (b, 0, 0, 0)
    return pl.pallas_call(
        functools.partial(_attn_kernel, top_k=top_k, first_block=first_block),
        grid=(batch, n_blocks),
        in_specs=[
            pl.BlockSpec((1, ATT_WIDTH, tq), qblk),
            pl.BlockSpec((1, N_HEADS // 2, seq, 2 * HEAD_DIM), full4),
            pl.BlockSpec((1, N_HEADS, V_ROWS, seq), full4),
            pl.BlockSpec((1, IDX_HEADS * IDX_DIM, tq), qblk),
            pl.BlockSpec((1, 2, seq, 2 * IDX_DIM), full4),
            pl.BlockSpec((1, WI_ROWS, tq), qblk),
        ],
        out_specs=pl.BlockSpec((1, tq, ATT_WIDTH), lambda b, i: (b, i, 0)),
        out_shape=jax.ShapeDtypeStruct((batch, n_blocks * tq, ATT_WIDTH), BF16),
        scratch_shapes=[pltpu.VMEM((seq, tq), I16), pltpu.VMEM((seq, tq), I16),
                        pltpu.VMEM((seq, tq), BF16)],
        compiler_params=pltpu.CompilerParams(
            dimension_semantics=("parallel", "arbitrary"),
            vmem_limit_bytes=VMEM_LIMIT_BYTES),
        name=f"attn_k{seq}",
    )(qT, kp, vT, qiT, ki, wiT)


def _attn(qT, kp, vT, qiT, ki, wiT, *, top_k, tq=256, blocks_per_class=2):
    seq = qT.shape[2]
    parts = [
        _attn_class(qT, kp, vT, qiT, ki, wiT, top_k=top_k, tq=tq, first_block=fb,
                    n_blocks=blocks_per_class)
        for fb in range(0, seq // tq, blocks_per_class)
    ]
    return jnp.concatenate(parts, axis=1)


def _pool_kernel(u_ref, pw_ref, ps_ref, o_ref):
    seq = u_ref.shape[1]
    t = lax.broadcasted_iota(jnp.int32, (seq, POOL_CH), 0)
    for g, w in enumerate(POOL_WINDOWS):
        ug = u_ref[0, :, g * POOL_CH:(g + 1) * POOL_CH]
        s, sh = ug, 1
        while sh < w:
            s = s + jnp.where(t >= sh, pltpu.roll(s, sh, axis=0), 0.0)
            sh *= 2
        pooled = s / jnp.minimum(t + 1, w).astype(F32) - ug
        mixed = jnp.dot(pooled.astype(BF16), pw_ref[g], preferred_element_type=F32)
        o_ref[0, :, g * POOL_CH:(g + 1) * POOL_CH] = (
            mixed * ps_ref[:, g * POOL_CH:(g + 1) * POOL_CH]).astype(BF16)


def _pool(u, pw, ps):
    batch, seq, width = u.shape
    return pl.pallas_call(
        _pool_kernel,
        grid=(batch,),
        in_specs=[
            pl.BlockSpec((1, seq, width), lambda b: (b, 0, 0)),
            pl.BlockSpec(pw.shape, lambda b: (0, 0, 0)),
            pl.BlockSpec((1, width), lambda b: (0, 0)),
        ],
        out_specs=pl.BlockSpec((1, seq, width), lambda b: (b, 0, 0)),
        out_shape=jax.ShapeDtypeStruct((batch, seq, width), BF16),
        compiler_params=pltpu.CompilerParams(
            dimension_semantics=("parallel",), vmem_limit_bytes=VMEM_LIMIT_BYTES),
        name="pool",
    )(u, pw, ps)


def _outproj_kernel(x_ref, a_ref, p_ref, wa_ref, wp_ref, o_ref):
    o_ref[...] = (x_ref[...]
                  + jnp.dot(a_ref[...], wa_ref[...], preferred_element_type=F32)
                  + jnp.dot(p_ref[...], wp_ref[...], preferred_element_type=F32))


def _outproj(x1, attn, pooled, wa, wp, *, tm=512):
    n, d = x1.shape
    return pl.pallas_call(
        _outproj_kernel,
        grid=(n // tm,),
        in_specs=[
            pl.BlockSpec((tm, d), lambda i: (i, 0)),
            pl.BlockSpec((tm, attn.shape[1]), lambda i: (i, 0)),
            pl.BlockSpec((tm, pooled.shape[1]), lambda i: (i, 0)),
            pl.BlockSpec(wa.shape, lambda i: (0, 0)),
            pl.BlockSpec(wp.shape, lambda i: (0, 0)),
        ],
        out_specs=pl.BlockSpec((tm, d), lambda i: (i, 0)),
        out_shape=jax.ShapeDtypeStruct((n, d), F32),
        compiler_params=pltpu.CompilerParams(
            dimension_semantics=("parallel",), vmem_limit_bytes=VMEM_LIMIT_BYTES),
        name="outproj",
    )(x1, attn, pooled, wa, wp)


def _rope_tables(seq, rot_dim):
    half = rot_dim // 2
    inv = ROPE_THETA ** (-jnp.arange(half, dtype=F32) / half)
    ang = inv[:, None] * jnp.arange(seq, dtype=F32)[None, :]
    return jnp.cos(ang), jnp.sin(ang)


def kernel(x, ffn1_norm, ffn1_w_gate, ffn1_w_up, ffn1_w_down, mix_norm, w_in, q_norm, k_norm,
           pool_w, pool_scale, w_out, ffn2_norm, ffn2_w_gate, ffn2_w_up, ffn2_w_down):
    batch, seq, d = x.shape
    depth = w_in.shape[0]
    top_k = min(INDEX_TOPK, seq // 4)
    idx_scale = (IDX_HEADS ** -0.5) * (IDX_DIM ** -0.5)
    cos, sin = _rope_tables(seq, HEAD_DIM)
    cosi, sini = _rope_tables(seq, IDX_ROPE_DIM)
    n_feat = 3 * ATT_WIDTH + IDX_HEADS * IDX_DIM + IDX_DIM
    xf = x.reshape(batch * seq, d)

    for l in range(depth):
        xf = _ffn(xf, ffn1_norm[l][None], ffn1_w_gate[l].astype(BF16), ffn1_w_up[l].astype(BF16),
                  ffn1_w_down[l].astype(BF16))

        w = w_in[l]
        w_feat = jnp.pad(w[:, :n_feat + IDX_HEADS], ((0, 0), (0, WI_ROWS - IDX_HEADS)))
        wt = w_feat.T.astype(BF16)
        wu = w[:, n_feat + IDX_HEADS:].astype(BF16)
        qT, kp, vT, qiT, ki, wiT, u = _proj(
            xf, mix_norm[l][None], wt, wu, q_norm[l][:, None], k_norm[l][:, None],
            cos, sin, cosi, sini, batch=batch, seq=seq, idx_scale=idx_scale)

        attn = _attn(qT, kp, vT, qiT, ki, wiT, top_k=top_k)
        pooled = _pool(u.reshape(batch, seq, POOL_WIDTH), pool_w[l].astype(BF16), pool_scale[l][None])

        wo = w_out[l].astype(BF16)
        xf = _outproj(xf, attn.reshape(batch * seq, ATT_WIDTH), pooled.reshape(batch * seq, POOL_WIDTH),
                      wo[:ATT_WIDTH], wo[ATT_WIDTH:])

        xf = _ffn(xf, ffn2_norm[l][None], ffn2_w_gate[l].astype(BF16), ffn2_w_up[l].astype(BF16),
                  ffn2_w_down[l].astype(BF16))
    return xf.reshape(batch, seq, d)
```

```python
import functools
import math

import jax
import jax.numpy as jnp
import numpy as np
from jax import lax
from jax.experimental import pallas as pl
from jax.experimental.pallas import tpu as pltpu

HEAD_DIM = 64
N_HEADS = 8
ATT_WIDTH = N_HEADS * HEAD_DIM
IDX_HEADS = 4
IDX_DIM = 64
IDX_ROPE_DIM = 32
INDEX_TOPK = 256
POOL_WINDOWS = (2, 4, 8, 16)
POOL_CH = 128
POOL_WIDTH = POOL_CH * len(POOL_WINDOWS)
ROPE_THETA = 10000.0
NORM_EPS = 1e-6
WI_ROWS = 8
V_ROWS = HEAD_DIM + 16

VMEM_LIMIT_BYTES = 56 * 1024 * 1024

F32 = jnp.float32
BF16 = jnp.bfloat16
NT_DIMS = (((1,), (1,)), ((), ()))


def _rms(x, g):
    ms = jnp.mean(x * x, axis=-1, keepdims=True)
    return x * lax.rsqrt(ms + NORM_EPS) * g


def _ffn_kernel(x_ref, g_ref, wg_ref, wu_ref, wd_ref, o_ref, act_ref, *, tf):
    x = x_ref[...]
    h = _rms(x, g_ref[...]).astype(BF16)
    for c in range(wg_ref.shape[1] // tf):
        cols = slice(c * tf, (c + 1) * tf)
        gate = jnp.dot(h, wg_ref[:, cols], preferred_element_type=F32)
        up = jnp.dot(h, wu_ref[:, cols], preferred_element_type=F32)
        act_ref[:, cols] = (gate * jax.nn.sigmoid(gate) * up).astype(BF16)
    o_ref[...] = x + 0.5 * jnp.dot(act_ref[...], wd_ref[...], preferred_element_type=F32)


def _resident(shape):
    return pl.BlockSpec(shape, lambda *_: (0,) * len(shape), pipeline_mode=pl.Buffered(1))


def _ffn(x, g, wg, wu, wd, *, tm=512, tf=256):
    n, d = x.shape
    f = wg.shape[1]
    return pl.pallas_call(
        functools.partial(_ffn_kernel, tf=tf),
        grid=(n // tm,),
        in_specs=[
            pl.BlockSpec((tm, d), lambda i: (i, 0)),
            _resident((1, d)),
            _resident((d, f)),
            _resident((d, f)),
            _resident((f, d)),
        ],
        out_specs=pl.BlockSpec((tm, d), lambda i: (i, 0)),
        out_shape=jax.ShapeDtypeStruct((n, d), F32),
        scratch_shapes=[pltpu.VMEM((tm, f), BF16)],
        compiler_params=pltpu.CompilerParams(
            dimension_semantics=("parallel",),
            vmem_limit_bytes=VMEM_LIMIT_BYTES),
        name="ffn",
    )(x, g, wg, wu, wd)


def _rope_rows(x, cos, sin, half):
    x1, x2 = x[:half], x[half:2 * half]
    return x1 * cos - x2 * sin, x2 * cos + x1 * sin


def _proj_kernel(x_ref, g_ref, wt_ref, wu_ref, gq_ref, gk_ref, cos_ref, sin_ref,
                 cosi_ref, sini_ref,
                 qT_ref, kp_ref, vT_ref, qiT_ref, ki_ref, wiT_ref, u_ref, *, idx_scale, q_scale):
    h = _rms(x_ref[...], g_ref[...]).astype(BF16)
    pT = lax.dot_general(wt_ref[...], h, NT_DIMS, preferred_element_type=F32)
    u_ref[...] = jnp.dot(h, wu_ref[...], preferred_element_type=F32)

    cos, sin = cos_ref[...], sin_ref[...]
    gq, gk = gq_ref[...], gk_ref[...]
    half = HEAD_DIM // 2

    def head_norm_rope(rows, gain):
        ms = jnp.mean(rows * rows, axis=0, keepdims=True)
        return _rope_rows(rows * lax.rsqrt(ms + NORM_EPS) * gain, cos, sin, half)

    for hd in range(N_HEADS):
        r0 = hd * HEAD_DIM
        a, b = head_norm_rope(pT[r0:r0 + HEAD_DIM], gq)
        qT_ref[0, r0:r0 + half, :] = (a * q_scale).astype(BF16)
        qT_ref[0, r0 + half:r0 + HEAD_DIM, :] = (b * q_scale).astype(BF16)

    k0 = ATT_WIDTH
    for pair in range(N_HEADS // 2):
        parts = []
        for e in range(2):
            r0 = k0 + (2 * pair + e) * HEAD_DIM
            parts.extend(head_norm_rope(pT[r0:r0 + HEAD_DIM], gk))
        kp_ref[0, pair] = jnp.concatenate(parts, axis=0).T.astype(BF16)

    v0 = 2 * ATT_WIDTH
    tt = pT.shape[1]
    ones_rows = jnp.where(lax.broadcasted_iota(jnp.int32, (V_ROWS - HEAD_DIM, tt), 0) == 0,
                          1.0, 0.0).astype(BF16)
    for hd in range(N_HEADS):
        r0 = v0 + hd * HEAD_DIM
        vT_ref[0, hd, :HEAD_DIM, :] = pT[r0:r0 + HEAD_DIM].astype(BF16)
        vT_ref[0, hd, HEAD_DIM:, :] = ones_rows

    cosi, sini = cosi_ref[...], sini_ref[...]
    ihalf = IDX_ROPE_DIM // 2
    qi0 = 3 * ATT_WIDTH
    for hi in range(IDX_HEADS):
        r0 = qi0 + hi * IDX_DIM
        a, b = _rope_rows(pT[r0:r0 + IDX_DIM], cosi, sini, ihalf)
        qiT_ref[0, r0 - qi0:r0 - qi0 + ihalf, :] = a.astype(BF16)
        qiT_ref[0, r0 - qi0 + ihalf:r0 - qi0 + IDX_ROPE_DIM, :] = b.astype(BF16)
        qiT_ref[0, r0 - qi0 + IDX_ROPE_DIM:r0 - qi0 + IDX_DIM, :] = (
            pT[r0 + IDX_ROPE_DIM:r0 + IDX_DIM].astype(BF16))

    ki0 = qi0 + IDX_HEADS * IDX_DIM
    a, b = _rope_rows(pT[ki0:ki0 + IDX_DIM], cosi, sini, ihalf)
    ki = jnp.concatenate([a, b, pT[ki0 + IDX_ROPE_DIM:ki0 + IDX_DIM]], axis=0)
    zeros = jnp.zeros_like(ki)
    ki_ref[0, 0] = jnp.concatenate([ki, zeros], axis=0).T.astype(BF16)
    ki_ref[0, 1] = jnp.concatenate([zeros, ki], axis=0).T.astype(BF16)

    wi0 = ki0 + IDX_DIM
    wiT_ref[0] = pT[wi0:wi0 + WI_ROWS] * idx_scale


def _proj(x1, g, wt, wu, gq, gk, cos, sin, cosi, sini, *, batch, seq, idx_scale, tt=512):
    n, d = x1.shape
    rows = wt.shape[0]
    nt = seq // tt
    tok = lambda b, t: (b * nt + t, 0)
    const = lambda b, t: (0, 0)
    featmaj = lambda b, t: (b, 0, t)
    out_shape = (
        jax.ShapeDtypeStruct((batch, ATT_WIDTH, seq), BF16),
        jax.ShapeDtypeStruct((batch, N_HEADS // 2, seq, 2 * HEAD_DIM), BF16),
        jax.ShapeDtypeStruct((batch, N_HEADS, V_ROWS, seq), BF16),
        jax.ShapeDtypeStruct((batch, IDX_HEADS * IDX_DIM, seq), BF16),
        jax.ShapeDtypeStruct((batch, 2, seq, 2 * IDX_DIM), BF16),
        jax.ShapeDtypeStruct((batch, WI_ROWS, seq), F32),
        jax.ShapeDtypeStruct((n, POOL_WIDTH), F32),
    )
    out_specs = (
        pl.BlockSpec((1, ATT_WIDTH, tt), featmaj),
        pl.BlockSpec((1, N_HEADS // 2, tt, 2 * HEAD_DIM), lambda b, t: (b, 0, t, 0)),
        pl.BlockSpec((1, N_HEADS, V_ROWS, tt), lambda b, t: (b, 0, 0, t)),
        pl.BlockSpec((1, IDX_HEADS * IDX_DIM, tt), featmaj),
        pl.BlockSpec((1, 2, tt, 2 * IDX_DIM), lambda b, t: (b, 0, t, 0)),
        pl.BlockSpec((1, WI_ROWS, tt), featmaj),
        pl.BlockSpec((tt, POOL_WIDTH), tok),
    )
    in_specs = [
        pl.BlockSpec((tt, d), tok),
        pl.BlockSpec((1, d), const),
        pl.BlockSpec((rows, d), const),
        pl.BlockSpec((d, POOL_WIDTH), const),
        pl.BlockSpec((HEAD_DIM, 1), const),
        pl.BlockSpec((HEAD_DIM, 1), const),
        pl.BlockSpec((HEAD_DIM // 2, tt), lambda b, t: (0, t)),
        pl.BlockSpec((HEAD_DIM // 2, tt), lambda b, t: (0, t)),
        pl.BlockSpec((IDX_ROPE_DIM // 2, tt), lambda b, t: (0, t)),
        pl.BlockSpec((IDX_ROPE_DIM // 2, tt), lambda b, t: (0, t)),
    ]
    return pl.pallas_call(
        functools.partial(_proj_kernel, idx_scale=idx_scale,
                          q_scale=HEAD_DIM ** -0.5 * math.log2(math.e)),
        grid=(batch, nt),
        in_specs=in_specs,
        out_specs=out_specs,
        out_shape=out_shape,
        compiler_params=pltpu.CompilerParams(
            dimension_semantics=("parallel", "parallel"),
            vmem_limit_bytes=VMEM_LIMIT_BYTES),
        name="proj",
    )(x1, g, wt, wu, gq, gk, cos, sin, cosi, sini)


INT_MIN = np.int32(-2 ** 31)
I16 = jnp.int16
I16_MIN = -2 ** 15
COUNT_ROWS = 128


def _count(pred):
    ones = jnp.where(pred, jnp.bfloat16(1), jnp.bfloat16(0))
    rows, cols = ones.shape
    groups = ones.reshape(rows // COUNT_ROWS, COUNT_ROWS, cols)
    part = groups[0]
    for i in range(1, rows // COUNT_ROWS):
        part = part + groups[i]
    return part.astype(F32).sum(axis=0, keepdims=True).astype(jnp.int32)


def _to_i16(pattern):
    return (pattern + I16_MIN).astype(I16)


class _Pacer:
    def __init__(self):
        self._zero = None

    def post(self, positive_row):
        self._zero = jnp.where(positive_row < 0.0, 1, 0)

    def wait(self, value):
        if self._zero is None:
            return value
        value, self._zero = value | self._zero, None
        return value


def _kth_largest_16(ref, k, pacer, n_bits=16):
    found = jnp.zeros_like(k)
    for bit in reversed(range(n_bits)):
        cand = pacer.wait(found) | (1 << bit)
        found = jnp.where(_count(ref[...] >= _to_i16(cand)) >= k, cand, found)
        yield None
    yield _to_i16(found)


def _drain(gen):
    for item in gen:
        if item is not None:
            return item
        yield None


def _topk_bias_steps(score, causal_bias, k_pos, hi_ref, lo_ref, bias_out_ref, top_k, pacer):
    seq, tq = score.shape
    bits = pltpu.bitcast(score, jnp.int32)
    key = jnp.where(bits < 0, INT_MIN - bits, bits)
    hi_ref[...] = lax.shift_right_arithmetic(key, 16).astype(I16)
    lo_ref[...] = ((key & 0xFFFF) + I16_MIN).astype(I16)
    yield

    h16 = yield from _drain(_kth_largest_16(hi_ref, jnp.full((1, tq), top_k, jnp.int32), pacer))
    hi = hi_ref[...]
    k_lo = top_k - _count(hi > h16)
    lo_ref[...] = jnp.where(hi == h16, lo_ref[...], jnp.int16(I16_MIN))
    yield
    l16 = yield from _drain(_kth_largest_16(lo_ref, k_lo, pacer))
    lo = lo_ref[...]
    need = k_lo - _count(lo > l16)

    bias_out_ref[...] = jnp.where(hi > h16, causal_bias(),
                                  jnp.where(lo > l16, causal_bias(), jnp.bfloat16(-jnp.inf)))
    lo_ref[...] = jnp.where(hi == h16, jnp.where(lo == l16, k_pos.astype(I16), jnp.int16(seq)),
                            jnp.int16(seq))
    yield
    cut = jnp.zeros((1, tq), jnp.int32)
    for bit in reversed(range(seq.bit_length() - 1)):
        cand = pacer.wait(cut) + (1 << bit)
        cut = jnp.where(_count(lo_ref[...] < cand.astype(I16)) < need, cand, cut)
        yield
    bias_out_ref[...] = jnp.where(lo_ref[...] <= cut.astype(I16), causal_bias(), bias_out_ref[...])


def _head_steps(qT_ref, kp_ref, vT_ref, bias_ref, o_ref, pacer):
    tq = qT_ref.shape[2]
    half_rows = lax.broadcasted_iota(jnp.int32, (2 * HEAD_DIM, tq), 0) < HEAD_DIM
    outs = []
    for hd in range(N_HEADS):
        pair, e = divmod(hd, 2)
        qpair = qT_ref[0, 2 * HEAD_DIM * pair:2 * HEAD_DIM * (pair + 1), :]
        qz = jnp.where(half_rows if e == 0 else jnp.logical_not(half_rows), qpair,
                       jnp.zeros_like(qpair))
        logits = jnp.dot(kp_ref[0, pair], qz, preferred_element_type=F32).astype(BF16) + bias_ref[...]
        p = jnp.exp2(logits - jnp.max(logits, axis=0, keepdims=True))
        o_aug = jnp.dot(vT_ref[0, hd], p, preferred_element_type=F32)
        outs.append(o_aug[:HEAD_DIM] / o_aug[HEAD_DIM:HEAD_DIM + 1])
        pacer.post(o_aug[HEAD_DIM:HEAD_DIM + 1])
        yield
    o_ref[0] = jnp.concatenate(outs, axis=0).T.astype(BF16)


def _attn_kernel(qT_ref, kp_ref, vT_ref, qiT_ref, ki_ref, wiT_ref, o_ref,
                 hi_ref, lo_ref, bias_ref, *, top_k, first_block, n_blocks, n_steps):
    j = pl.program_id(0)
    seq = kp_ref.shape[2]
    tq = qT_ref.shape[2]
    sel_slot = j % 2
    att_slot = 1 - sel_slot

    @pl.when(j == 0)
    def _():
        bias_ref[1] = jnp.zeros((seq, tq), BF16)

    block = first_block + jnp.minimum(j, n_steps - 1) % n_blocks
    k_pos = lax.broadcasted_iota(jnp.int32, (seq, tq), 0)

    def causal():
        return k_pos <= block * tq + lax.broadcasted_iota(jnp.int32, (seq, tq), 1)

    def causal_bias():
        return jnp.where(causal(), 0.0, -jnp.inf).astype(BF16)

    wi = wiT_ref[0]
    score = jnp.zeros((seq, tq), F32)
    for hi in range(IDX_HEADS):
        pair, e = divmod(hi, 2)
        rel = jnp.dot(ki_ref[0, e], qiT_ref[0, 2 * IDX_DIM * pair:2 * IDX_DIM * (pair + 1), :],
                      preferred_element_type=F32)
        score = score + wi[hi:hi + 1, :] * jnp.maximum(rel, 0.0)
    score = jnp.where(causal(), score, -jnp.inf)

    pacer = _Pacer()
    select = _topk_bias_steps(score, causal_bias, k_pos, hi_ref, lo_ref, bias_ref.at[sel_slot], top_k,
                              pacer)
    attend = _head_steps(qT_ref, kp_ref, vT_ref, bias_ref.at[att_slot], o_ref, pacer)
    n_select = 3 + 2 * 16 + seq.bit_length() - 1
    per_head = -(-n_select // N_HEADS)
    for _ in range(N_HEADS):
        next(attend)
        for _ in range(per_head):
            next(select, None)
    for _ in select:
        pass
    for _ in attend:
        pass


def _attn_class(qT, kp, vT, qiT, ki, wiT, *, top_k, tq, first_block, n_blocks):
    batch = qT.shape[0]
    seq = (first_block + n_blocks) * tq
    n_steps = batch * n_blocks

    def sel(j):
        return jnp.minimum(j, n_steps - 1)

    def att(j):
        return jnp.maximum(j - 1, 0)

    return pl.pallas_call(
        functools.partial(_attn_kernel, top_k=top_k, first_block=first_block, n_blocks=n_blocks,
                          n_steps=n_steps),
        grid=(n_steps + 1,),
        in_specs=[
            pl.BlockSpec((1, ATT_WIDTH, tq),
                         lambda j: (att(j) // n_blocks, 0, first_block + att(j) % n_blocks)),
            pl.BlockSpec((1, N_HEADS // 2, seq, 2 * HEAD_DIM), lambda j: (att(j) // n_blocks, 0, 0, 0)),
            pl.BlockSpec((1, N_HEADS, V_ROWS, seq), lambda j: (att(j) // n_blocks, 0, 0, 0)),
            pl.BlockSpec((1, IDX_HEADS * IDX_DIM, tq),
                         lambda j: (sel(j) // n_blocks, 0, first_block + sel(j) % n_blocks)),
            pl.BlockSpec((1, 2, seq, 2 * IDX_DIM), lambda j: (sel(j) // n_blocks, 0, 0, 0)),
            pl.BlockSpec((1, WI_ROWS, tq),
                         lambda j: (sel(j) // n_blocks, 0, first_block + sel(j) % n_blocks)),
        ],
        out_specs=pl.BlockSpec((1, tq, ATT_WIDTH),
                               lambda j: (att(j) // n_blocks, att(j) % n_blocks, 0)),
        out_shape=jax.ShapeDtypeStruct((batch, n_blocks * tq, ATT_WIDTH), BF16),
        scratch_shapes=[pltpu.VMEM((seq, tq), I16), pltpu.VMEM((seq, tq), I16),
                        pltpu.VMEM((2, seq, tq), BF16)],
        compiler_params=pltpu.CompilerParams(
            dimension_semantics=("arbitrary",),
            vmem_limit_bytes=VMEM_LIMIT_BYTES),
        name=f"attn_k{seq}",
    )(qT, kp, vT, qiT, ki, wiT)


def _attn(qT, kp, vT, qiT, ki, wiT, *, top_k, tq=256, blocks_per_class=2):
    seq = qT.shape[2]
    parts = [
        _attn_class(qT, kp, vT, qiT, ki, wiT, top_k=top_k, tq=tq, first_block=fb,
                    n_blocks=blocks_per_class)
        for fb in range(0, seq // tq, blocks_per_class)
    ]
    return jnp.concatenate(parts, axis=1)


def _pool_kernel(u_ref, pw_ref, ps_ref, o_ref):
    seq = u_ref.shape[1]
    t = lax.broadcasted_iota(jnp.int32, (seq, POOL_CH), 0)
    for g, w in enumerate(POOL_WINDOWS):
        ug = u_ref[0, :, g * POOL_CH:(g + 1) * POOL_CH]
        s, sh = ug, 1
        while sh < w:
            s = s + jnp.where(t >= sh, pltpu.roll(s, sh, axis=0), 0.0)
            sh *= 2
        pooled = s / jnp.minimum(t + 1, w).astype(F32) - ug
        mixed = jnp.dot(pooled.astype(BF16), pw_ref[g], preferred_element_type=F32)
        o_ref[0, :, g * POOL_CH:(g + 1) * POOL_CH] = (
            mixed * ps_ref[:, g * POOL_CH:(g + 1) * POOL_CH]).astype(BF16)


def _pool(u, pw, ps):
    batch, seq, width = u.shape
    return pl.pallas_call(
        _pool_kernel,
        grid=(batch,),
        in_specs=[
            pl.BlockSpec((1, seq, width), lambda b: (b, 0, 0)),
            pl.BlockSpec(pw.shape, lambda b: (0, 0, 0)),
            pl.BlockSpec((1, width), lambda b: (0, 0)),
        ],
        out_specs=pl.BlockSpec((1, seq, width), lambda b: (b, 0, 0)),
        out_shape=jax.ShapeDtypeStruct((batch, seq, width), BF16),
        compiler_params=pltpu.CompilerParams(
            dimension_semantics=("parallel",), vmem_limit_bytes=VMEM_LIMIT_BYTES),
        name="pool",
    )(u, pw, ps)


def _outproj_kernel(x_ref, a_ref, p_ref, wa_ref, wp_ref, o_ref):
    o_ref[...] = (x_ref[...]
                  + jnp.dot(a_ref[...], wa_ref[...], preferred_element_type=F32)
                  + jnp.dot(p_ref[...], wp_ref[...], preferred_element_type=F32))


def _outproj(x1, attn, pooled, wa, wp, *, tm=512):
    n, d = x1.shape
    return pl.pallas_call(
        _outproj_kernel,
        grid=(n // tm,),
        in_specs=[
            pl.BlockSpec((tm, d), lambda i: (i, 0)),
            pl.BlockSpec((tm, attn.shape[1]), lambda i: (i, 0)),
            pl.BlockSpec((tm, pooled.shape[1]), lambda i: (i, 0)),
            pl.BlockSpec(wa.shape, lambda i: (0, 0)),
            pl.BlockSpec(wp.shape, lambda i: (0, 0)),
        ],
        out_specs=pl.BlockSpec((tm, d), lambda i: (i, 0)),
        out_shape=jax.ShapeDtypeStruct((n, d), F32),
        compiler_params=pltpu.CompilerParams(
            dimension_semantics=("parallel",), vmem_limit_bytes=VMEM_LIMIT_BYTES),
        name="outproj",
    )(x1, attn, pooled, wa, wp)


def _rope_tables(seq, rot_dim):
    half = rot_dim // 2
    inv = ROPE_THETA ** (-jnp.arange(half, dtype=F32) / half)
    ang = inv[:, None] * jnp.arange(seq, dtype=F32)[None, :]
    return jnp.cos(ang), jnp.sin(ang)


def kernel(x, ffn1_norm, ffn1_w_gate, ffn1_w_up, ffn1_w_down, mix_norm, w_in, q_norm, k_norm,
           pool_w, pool_scale, w_out, ffn2_norm, ffn2_w_gate, ffn2_w_up, ffn2_w_down):
    batch, seq, d = x.shape
    depth = w_in.shape[0]
    top_k = min(INDEX_TOPK, seq // 4)
    idx_scale = (IDX_HEADS ** -0.5) * (IDX_DIM ** -0.5)
    cos, sin = _rope_tables(seq, HEAD_DIM)
    cosi, sini = _rope_tables(seq, IDX_ROPE_DIM)
    n_feat = 3 * ATT_WIDTH + IDX_HEADS * IDX_DIM + IDX_DIM
    xf = x.reshape(batch * seq, d)

    for l in range(depth):
        xf = _ffn(xf, ffn1_norm[l][None], ffn1_w_gate[l].astype(BF16), ffn1_w_up[l].astype(BF16),
                  ffn1_w_down[l].astype(BF16))

        w = w_in[l]
        w_feat = jnp.pad(w[:, :n_feat + IDX_HEADS], ((0, 0), (0, WI_ROWS - IDX_HEADS)))
        wt = w_feat.T.astype(BF16)
        wu = w[:, n_feat + IDX_HEADS:].astype(BF16)
        qT, kp, vT, qiT, ki, wiT, u = _proj(
            xf, mix_norm[l][None], wt, wu, q_norm[l][:, None], k_norm[l][:, None],
            cos, sin, cosi, sini, batch=batch, seq=seq, idx_scale=idx_scale)

        attn = _attn(qT, kp, vT, qiT, ki, wiT, top_k=top_k)
        pooled = _pool(u.reshape(batch, seq, POOL_WIDTH), pool_w[l].astype(BF16), pool_scale[l][None])

        wo = w_out[l].astype(BF16)
        xf = _outproj(xf, attn.reshape(batch * seq, ATT_WIDTH), pooled.reshape(batch * seq, POOL_WIDTH),
                      wo[:ATT_WIDTH], wo[ATT_WIDTH:])

        xf = _ffn(xf, ffn2_norm[l][None], ffn2_w_gate[l].astype(BF16), ffn2_w_up[l].astype(BF16),
                  ffn2_w_down[l].astype(BF16))
    return xf.reshape(batch, seq, d)
```

```python
import functools
import math

import jax
import jax.numpy as jnp
import numpy as np
from jax import lax
from jax.experimental import pallas as pl
from jax.experimental.pallas import tpu as pltpu

HEAD_DIM = 64
N_HEADS = 8
ATT_WIDTH = N_HEADS * HEAD_DIM
IDX_HEADS = 4
IDX_DIM = 64
IDX_ROPE_DIM = 32
INDEX_TOPK = 256
POOL_WINDOWS = (2, 4, 8, 16)
POOL_CH = 128
POOL_WIDTH = POOL_CH * len(POOL_WINDOWS)
ROPE_THETA = 10000.0
NORM_EPS = 1e-6
WI_ROWS = 8
V_ROWS = HEAD_DIM + 16

VMEM_LIMIT_BYTES = 56 * 1024 * 1024

F32 = jnp.float32
BF16 = jnp.bfloat16
NT_DIMS = (((1,), (1,)), ((), ()))


def _rms(x, g):
    ms = jnp.mean(x * x, axis=-1, keepdims=True)
    return x * lax.rsqrt(ms + NORM_EPS) * g


def _ffn_kernel(x_ref, g_ref, wg_ref, wu_ref, wd_ref, *rest, tf):
    *mixer, o_ref, act_ref = rest
    x = x_ref[...]
    if mixer:
        attn_ref, pooled_ref, wo_ref = mixer
        mixed = jnp.concatenate([attn_ref[...], pooled_ref[...]], axis=1)
        x = x + jnp.dot(mixed, wo_ref[...], preferred_element_type=F32)
    h = _rms(x, g_ref[...]).astype(BF16)
    for c in range(wg_ref.shape[1] // tf):
        cols = slice(c * tf, (c + 1) * tf)
        gate = jnp.dot(h, wg_ref[:, cols], preferred_element_type=F32)
        up = jnp.dot(h, wu_ref[:, cols], preferred_element_type=F32)
        act_ref[:, cols] = (gate * jax.nn.sigmoid(gate) * up).astype(BF16)
    o_ref[...] = x + 0.5 * jnp.dot(act_ref[...], wd_ref[...], preferred_element_type=F32)


def _resident(shape):
    return pl.BlockSpec(shape, lambda *_: (0,) * len(shape), pipeline_mode=pl.Buffered(1))


def _ffn(x, g, wg, wu, wd, mixer=(), *, tm=512, tf=256):
    n, d = x.shape
    f = wg.shape[1]
    rows = lambda i: (i, 0)
    mixer_specs = []
    if mixer:
        attn, pooled, wo = mixer
        mixer_specs = [pl.BlockSpec((tm, attn.shape[1]), rows), pl.BlockSpec((tm, pooled.shape[1]), rows),
                       _resident(wo.shape)]
    return pl.pallas_call(
        functools.partial(_ffn_kernel, tf=tf),
        grid=(n // tm,),
        in_specs=[
            pl.BlockSpec((tm, d), rows),
            _resident((1, d)),
            _resident((d, f)),
            _resident((d, f)),
            _resident((f, d)),
            *mixer_specs,
        ],
        out_specs=pl.BlockSpec((tm, d), rows),
        out_shape=jax.ShapeDtypeStruct((n, d), F32),
        scratch_shapes=[pltpu.VMEM((tm, f), BF16)],
        compiler_params=pltpu.CompilerParams(
            dimension_semantics=("parallel",),
            vmem_limit_bytes=VMEM_LIMIT_BYTES),
        name="ffn_mix" if mixer else "ffn",
    )(x, g, wg, wu, wd, *mixer)


def _rope_rows(x, cos, sin, half):
    x1, x2 = x[:half], x[half:2 * half]
    return x1 * cos - x2 * sin, x2 * cos + x1 * sin


def _proj_kernel(x_ref, g_ref, wt_ref, wu_ref, gq_ref, gk_ref, cos_ref, sin_ref,
                 cosi_ref, sini_ref,
                 qT_ref, kp_ref, vT_ref, qiT_ref, ki_ref, wiT_ref, u_ref, *, idx_scale, q_scale):
    h = _rms(x_ref[...], g_ref[...]).astype(BF16)
    pT = lax.dot_general(wt_ref[...], h, NT_DIMS, preferred_element_type=F32)
    u_ref[...] = jnp.dot(h, wu_ref[...], preferred_element_type=F32)

    cos, sin = cos_ref[...], sin_ref[...]
    gq, gk = gq_ref[...], gk_ref[...]
    half = HEAD_DIM // 2

    def head_norm_rope(rows, gain):
        ms = jnp.mean(rows * rows, axis=0, keepdims=True)
        return _rope_rows(rows * lax.rsqrt(ms + NORM_EPS) * gain, cos, sin, half)

    for hd in range(N_HEADS):
        r0 = hd * HEAD_DIM
        a, b = head_norm_rope(pT[r0:r0 + HEAD_DIM], gq)
        qT_ref[0, r0:r0 + half, :] = (a * q_scale).astype(BF16)
        qT_ref[0, r0 + half:r0 + HEAD_DIM, :] = (b * q_scale).astype(BF16)

    k0 = ATT_WIDTH
    for pair in range(N_HEADS // 2):
        parts = []
        for e in range(2):
            r0 = k0 + (2 * pair + e) * HEAD_DIM
            parts.extend(head_norm_rope(pT[r0:r0 + HEAD_DIM], gk))
        kp_ref[0, pair] = jnp.concatenate(parts, axis=0).T.astype(BF16)

    v0 = 2 * ATT_WIDTH
    tt = pT.shape[1]
    ones_rows = jnp.where(lax.broadcasted_iota(jnp.int32, (V_ROWS - HEAD_DIM, tt), 0) == 0,
                          1.0, 0.0).astype(BF16)
    for hd in range(N_HEADS):
        r0 = v0 + hd * HEAD_DIM
        vT_ref[0, hd, :HEAD_DIM, :] = pT[r0:r0 + HEAD_DIM].astype(BF16)
        vT_ref[0, hd, HEAD_DIM:, :] = ones_rows

    cosi, sini = cosi_ref[...], sini_ref[...]
    ihalf = IDX_ROPE_DIM // 2
    qi0 = 3 * ATT_WIDTH
    for hi in range(IDX_HEADS):
        r0 = qi0 + hi * IDX_DIM
        a, b = _rope_rows(pT[r0:r0 + IDX_DIM], cosi, sini, ihalf)
        qiT_ref[0, r0 - qi0:r0 - qi0 + ihalf, :] = a.astype(BF16)
        qiT_ref[0, r0 - qi0 + ihalf:r0 - qi0 + IDX_ROPE_DIM, :] = b.astype(BF16)
        qiT_ref[0, r0 - qi0 + IDX_ROPE_DIM:r0 - qi0 + IDX_DIM, :] = (
            pT[r0 + IDX_ROPE_DIM:r0 + IDX_DIM].astype(BF16))

    ki0 = qi0 + IDX_HEADS * IDX_DIM
    a, b = _rope_rows(pT[ki0:ki0 + IDX_DIM], cosi, sini, ihalf)
    ki = jnp.concatenate([a, b, pT[ki0 + IDX_ROPE_DIM:ki0 + IDX_DIM]], axis=0)
    zeros = jnp.zeros_like(ki)
    ki_ref[0, 0] = jnp.concatenate([ki, zeros], axis=0).T.astype(BF16)
    ki_ref[0, 1] = jnp.concatenate([zeros, ki], axis=0).T.astype(BF16)

    wi0 = ki0 + IDX_DIM
    wiT_ref[0] = pT[wi0:wi0 + WI_ROWS] * idx_scale


def _proj(x1, g, wt, wu, gq, gk, cos, sin, cosi, sini, *, batch, seq, idx_scale, tt=512):
    n, d = x1.shape
    rows = wt.shape[0]
    nt = seq // tt
    tok = lambda b, t: (b * nt + t, 0)
    const = lambda b, t: (0, 0)
    featmaj = lambda b, t: (b, 0, t)
    out_shape = (
        jax.ShapeDtypeStruct((batch, ATT_WIDTH, seq), BF16),
        jax.ShapeDtypeStruct((batch, N_HEADS // 2, seq, 2 * HEAD_DIM), BF16),
        jax.ShapeDtypeStruct((batch, N_HEADS, V_ROWS, seq), BF16),
        jax.ShapeDtypeStruct((batch, IDX_HEADS * IDX_DIM, seq), BF16),
        jax.ShapeDtypeStruct((batch, 2, seq, 2 * IDX_DIM), BF16),
        jax.ShapeDtypeStruct((batch, WI_ROWS, seq), F32),
        jax.ShapeDtypeStruct((n, POOL_WIDTH), F32),
    )
    out_specs = (
        pl.BlockSpec((1, ATT_WIDTH, tt), featmaj),
        pl.BlockSpec((1, N_HEADS // 2, tt, 2 * HEAD_DIM), lambda b, t: (b, 0, t, 0)),
        pl.BlockSpec((1, N_HEADS, V_ROWS, tt), lambda b, t: (b, 0, 0, t)),
        pl.BlockSpec((1, IDX_HEADS * IDX_DIM, tt), featmaj),
        pl.BlockSpec((1, 2, tt, 2 * IDX_DIM), lambda b, t: (b, 0, t, 0)),
        pl.BlockSpec((1, WI_ROWS, tt), featmaj),
        pl.BlockSpec((tt, POOL_WIDTH), tok),
    )
    in_specs = [
        pl.BlockSpec((tt, d), tok),
        pl.BlockSpec((1, d), const),
        pl.BlockSpec((rows, d), const),
        pl.BlockSpec((d, POOL_WIDTH), const),
        pl.BlockSpec((HEAD_DIM, 1), const),
        pl.BlockSpec((HEAD_DIM, 1), const),
        pl.BlockSpec((HEAD_DIM // 2, tt), lambda b, t: (0, t)),
        pl.BlockSpec((HEAD_DIM // 2, tt), lambda b, t: (0, t)),
        pl.BlockSpec((IDX_ROPE_DIM // 2, tt), lambda b, t: (0, t)),
        pl.BlockSpec((IDX_ROPE_DIM // 2, tt), lambda b, t: (0, t)),
    ]
    return pl.pallas_call(
        functools.partial(_proj_kernel, idx_scale=idx_scale,
                          q_scale=HEAD_DIM ** -0.5 * math.log2(math.e)),
        grid=(batch, nt),
        in_specs=in_specs,
        out_specs=out_specs,
        out_shape=out_shape,
        compiler_params=pltpu.CompilerParams(
            dimension_semantics=("parallel", "parallel"),
            vmem_limit_bytes=VMEM_LIMIT_BYTES),
        name="proj",
    )(x1, g, wt, wu, gq, gk, cos, sin, cosi, sini)


INT_MIN = np.int32(-2 ** 31)
I16 = jnp.int16
I16_MIN = -2 ** 15
COUNT_ROWS = 128


def _count(pred):
    ones = jnp.where(pred, jnp.bfloat16(1), jnp.bfloat16(0))
    rows, cols = ones.shape
    groups = ones.reshape(rows // COUNT_ROWS, COUNT_ROWS, cols)
    part = groups[0]
    for i in range(1, rows // COUNT_ROWS):
        part = part + groups[i]
    return part.astype(F32).sum(axis=0, keepdims=True).astype(jnp.int32)


def _to_i16(pattern):
    return (pattern + I16_MIN).astype(I16)


class _Pacer:
    def __init__(self):
        self._zero = None

    def post(self, positive_row):
        self._zero = jnp.where(positive_row < 0.0, 1, 0)

    def wait(self, value):
        if self._zero is None:
            return value
        value, self._zero = value | self._zero, None
        return value


def _kth_largest_16(ref, k, pacer, n_bits=16):
    found = jnp.zeros_like(k)
    for bit in reversed(range(n_bits)):
        cand = pacer.wait(found) | (1 << bit)
        found = jnp.where(_count(ref[...] >= _to_i16(cand)) >= k, cand, found)
        yield None
    yield _to_i16(found)


def _drain(gen):
    for item in gen:
        if item is not None:
            return item
        yield None


def _topk_bias_steps(score, causal_bias, k_pos, hi_ref, lo_ref, bias_out_ref, top_k, pacer):
    seq, tq = score.shape
    bits = pltpu.bitcast(score, jnp.int32)
    key = jnp.where(bits < 0, INT_MIN - bits, bits)
    hi_ref[...] = lax.shift_right_arithmetic(key, 16).astype(I16)
    lo_ref[...] = ((key & 0xFFFF) + I16_MIN).astype(I16)
    yield

    h16 = yield from _drain(_kth_largest_16(hi_ref, jnp.full((1, tq), top_k, jnp.int32), pacer))
    hi = hi_ref[...]
    k_lo = top_k - _count(hi > h16)
    lo_ref[...] = jnp.where(hi == h16, lo_ref[...], jnp.int16(I16_MIN))
    yield
    l16 = yield from _drain(_kth_largest_16(lo_ref, k_lo, pacer))
    lo = lo_ref[...]
    need = k_lo - _count(lo > l16)

    bias_out_ref[...] = jnp.where(hi > h16, causal_bias(),
                                  jnp.where(lo > l16, causal_bias(), jnp.bfloat16(-jnp.inf)))
    lo_ref[...] = jnp.where(hi == h16, jnp.where(lo == l16, k_pos.astype(I16), jnp.int16(seq)),
                            jnp.int16(seq))
    yield
    cut = jnp.zeros((1, tq), jnp.int32)
    for bit in reversed(range(seq.bit_length() - 1)):
        cand = pacer.wait(cut) + (1 << bit)
        cut = jnp.where(_count(lo_ref[...] < cand.astype(I16)) < need, cand, cut)
        yield
    bias_out_ref[...] = jnp.where(lo_ref[...] <= cut.astype(I16), causal_bias(), bias_out_ref[...])


def _head_steps(qT_ref, kp_ref, vT_ref, bias_ref, o_ref, pacer):
    tq = qT_ref.shape[2]
    half_rows = lax.broadcasted_iota(jnp.int32, (2 * HEAD_DIM, tq), 0) < HEAD_DIM
    outs = []
    for hd in range(N_HEADS):
        pair, e = divmod(hd, 2)
        qpair = qT_ref[0, 2 * HEAD_DIM * pair:2 * HEAD_DIM * (pair + 1), :]
        qz = jnp.where(half_rows if e == 0 else jnp.logical_not(half_rows), qpair,
                       jnp.zeros_like(qpair))
        logits = jnp.dot(kp_ref[0, pair], qz, preferred_element_type=F32).astype(BF16) + bias_ref[...]
        p = jnp.exp2(logits - jnp.max(logits, axis=0, keepdims=True))
        o_aug = jnp.dot(vT_ref[0, hd], p, preferred_element_type=F32)
        outs.append(o_aug[:HEAD_DIM] / o_aug[HEAD_DIM:HEAD_DIM + 1])
        pacer.post(o_aug[HEAD_DIM:HEAD_DIM + 1])
        yield
    o_ref[0] = jnp.concatenate(outs, axis=0).T.astype(BF16)


def _attn_kernel(qT_ref, kp_ref, vT_ref, qiT_ref, ki_ref, wiT_ref, _aliased_out_ref, o_ref,
                 hi_ref, lo_ref, bias_ref, *, top_k, first_block, n_blocks, n_steps):
    j = pl.program_id(0)
    seq = kp_ref.shape[2]
    tq = qT_ref.shape[2]
    sel_slot = j % 2
    att_slot = 1 - sel_slot

    @pl.when(j == 0)
    def _():
        bias_ref[1] = jnp.zeros((seq, tq), BF16)

    block = first_block + jnp.minimum(j, n_steps - 1) % n_blocks
    k_pos = lax.broadcasted_iota(jnp.int32, (seq, tq), 0)

    def causal():
        return k_pos <= block * tq + lax.broadcasted_iota(jnp.int32, (seq, tq), 1)

    def causal_bias():
        return jnp.where(causal(), 0.0, -jnp.inf).astype(BF16)

    wi = wiT_ref[0]
    score = jnp.zeros((seq, tq), F32)
    for hi in range(IDX_HEADS):
        pair, e = divmod(hi, 2)
        rel = jnp.dot(ki_ref[0, e], qiT_ref[0, 2 * IDX_DIM * pair:2 * IDX_DIM * (pair + 1), :],
                      preferred_element_type=F32)
        score = score + wi[hi:hi + 1, :] * jnp.maximum(rel, 0.0)
    score = jnp.where(causal(), score, -jnp.inf)

    pacer = _Pacer()
    select = _topk_bias_steps(score, causal_bias, k_pos, hi_ref, lo_ref, bias_ref.at[sel_slot], top_k,
                              pacer)
    attend = _head_steps(qT_ref, kp_ref, vT_ref, bias_ref.at[att_slot], o_ref, pacer)
    n_select = 3 + 2 * 16 + seq.bit_length() - 1
    per_head = -(-n_select // N_HEADS)
    for _ in range(N_HEADS):
        next(attend)
        for _ in range(per_head):
            next(select, None)
    for _ in select:
        pass
    for _ in attend:
        pass


def _attn_class(qT, kp, vT, qiT, ki, wiT, out, *, top_k, tq, first_block, n_blocks):
    batch = qT.shape[0]
    seq = (first_block + n_blocks) * tq
    n_steps = batch * n_blocks

    def sel(j):
        return jnp.minimum(j, n_steps - 1)

    def att(j):
        return jnp.maximum(j - 1, 0)

    return pl.pallas_call(
        functools.partial(_attn_kernel, top_k=top_k, first_block=first_block, n_blocks=n_blocks,
                          n_steps=n_steps),
        grid=(n_steps + 1,),
        in_specs=[
            pl.BlockSpec((1, ATT_WIDTH, tq),
                         lambda j: (att(j) // n_blocks, 0, first_block + att(j) % n_blocks)),
            pl.BlockSpec((1, N_HEADS // 2, seq, 2 * HEAD_DIM), lambda j: (att(j) // n_blocks, 0, 0, 0)),
            pl.BlockSpec((1, N_HEADS, V_ROWS, seq), lambda j: (att(j) // n_blocks, 0, 0, 0)),
            pl.BlockSpec((1, IDX_HEADS * IDX_DIM, tq),
                         lambda j: (sel(j) // n_blocks, 0, first_block + sel(j) % n_blocks)),
            pl.BlockSpec((1, 2, seq, 2 * IDX_DIM), lambda j: (sel(j) // n_blocks, 0, 0, 0)),
            pl.BlockSpec((1, WI_ROWS, tq),
                         lambda j: (sel(j) // n_blocks, 0, first_block + sel(j) % n_blocks)),
            pl.BlockSpec(memory_space=pl.ANY),
        ],
        out_specs=pl.BlockSpec((1, tq, ATT_WIDTH),
                               lambda j: (att(j) // n_blocks, first_block + att(j) % n_blocks, 0)),
        out_shape=jax.ShapeDtypeStruct(out.shape, out.dtype),
        input_output_aliases={6: 0},
        scratch_shapes=[pltpu.VMEM((seq, tq), I16), pltpu.VMEM((seq, tq), I16),
                        pltpu.VMEM((2, seq, tq), BF16)],
        compiler_params=pltpu.CompilerParams(
            dimension_semantics=("arbitrary",),
            vmem_limit_bytes=VMEM_LIMIT_BYTES),
        name=f"attn_k{seq}",
    )(qT, kp, vT, qiT, ki, wiT, out)


def _attn(qT, kp, vT, qiT, ki, wiT, *, top_k, tq=256, blocks_per_class=2):
    batch, width, seq = qT.shape
    out = jnp.zeros((batch, seq, width), BF16)
    for fb in range(0, seq // tq, blocks_per_class):
        out = _attn_class(qT, kp, vT, qiT, ki, wiT, out, top_k=top_k, tq=tq, first_block=fb,
                          n_blocks=blocks_per_class)
    return out


def _pool_kernel(u_ref, pw_ref, ps_ref, o_ref):
    seq = u_ref.shape[1]
    t = lax.broadcasted_iota(jnp.int32, (seq, POOL_CH), 0)
    for g, w in enumerate(POOL_WINDOWS):
        ug = u_ref[0, :, g * POOL_CH:(g + 1) * POOL_CH]
        s, sh = ug, 1
        while sh < w:
            s = s + jnp.where(t >= sh, pltpu.roll(s, sh, axis=0), 0.0)
            sh *= 2
        pooled = s / jnp.minimum(t + 1, w).astype(F32) - ug
        mixed = jnp.dot(pooled.astype(BF16), pw_ref[g], preferred_element_type=F32)
        o_ref[0, :, g * POOL_CH:(g + 1) * POOL_CH] = (
            mixed * ps_ref[:, g * POOL_CH:(g + 1) * POOL_CH]).astype(BF16)


def _pool(u, pw, ps):
    batch, seq, width = u.shape
    return pl.pallas_call(
        _pool_kernel,
        grid=(batch,),
        in_specs=[
            pl.BlockSpec((1, seq, width), lambda b: (b, 0, 0)),
            pl.BlockSpec(pw.shape, lambda b: (0, 0, 0)),
            pl.BlockSpec((1, width), lambda b: (0, 0)),
        ],
        out_specs=pl.BlockSpec((1, seq, width), lambda b: (b, 0, 0)),
        out_shape=jax.ShapeDtypeStruct((batch, seq, width), BF16),
        compiler_params=pltpu.CompilerParams(
            dimension_semantics=("parallel",), vmem_limit_bytes=VMEM_LIMIT_BYTES),
        name="pool",
    )(u, pw, ps)


def _rope_tables(seq, rot_dim):
    half = rot_dim // 2
    inv = ROPE_THETA ** (-jnp.arange(half, dtype=F32) / half)
    ang = inv[:, None] * jnp.arange(seq, dtype=F32)[None, :]
    return jnp.cos(ang), jnp.sin(ang)


def kernel(x, ffn1_norm, ffn1_w_gate, ffn1_w_up, ffn1_w_down, mix_norm, w_in, q_norm, k_norm,
           pool_w, pool_scale, w_out, ffn2_norm, ffn2_w_gate, ffn2_w_up, ffn2_w_down):
    batch, seq, d = x.shape
    depth = w_in.shape[0]
    top_k = min(INDEX_TOPK, seq // 4)
    idx_scale = (IDX_HEADS ** -0.5) * (IDX_DIM ** -0.5)
    cos, sin = _rope_tables(seq, HEAD_DIM)
    cosi, sini = _rope_tables(seq, IDX_ROPE_DIM)
    n_feat = 3 * ATT_WIDTH + IDX_HEADS * IDX_DIM + IDX_DIM
    xf = x.reshape(batch * seq, d)

    for l in range(depth):
        xf = _ffn(xf, ffn1_norm[l][None], ffn1_w_gate[l].astype(BF16), ffn1_w_up[l].astype(BF16),
                  ffn1_w_down[l].astype(BF16))

        w = w_in[l]
        w_feat = jnp.pad(w[:, :n_feat + IDX_HEADS], ((0, 0), (0, WI_ROWS - IDX_HEADS)))
        wt = w_feat.T.astype(BF16)
        wu = w[:, n_feat + IDX_HEADS:].astype(BF16)
        qT, kp, vT, qiT, ki, wiT, u = _proj(
            xf, mix_norm[l][None], wt, wu, q_norm[l][:, None], k_norm[l][:, None],
            cos, sin, cosi, sini, batch=batch, seq=seq, idx_scale=idx_scale)

        attn = _attn(qT, kp, vT, qiT, ki, wiT, top_k=top_k)
        pooled = _pool(u.reshape(batch, seq, POOL_WIDTH), pool_w[l].astype(BF16), pool_scale[l][None])

        mixer = (attn.reshape(batch * seq, ATT_WIDTH), pooled.reshape(batch * seq, POOL_WIDTH),
                 w_out[l].astype(BF16))
        xf = _ffn(xf, ffn2_norm[l][None], ffn2_w_gate[l].astype(BF16), ffn2_w_up[l].astype(BF16),
                  ffn2_w_down[l].astype(BF16), mixer)
    return xf.reshape(batch, seq, d)
```

```python
import functools
import math

import jax
import jax.numpy as jnp
import numpy as np
from jax import lax
from jax.experimental import pallas as pl
from jax.experimental.pallas import tpu as pltpu

HEAD_DIM = 64
N_HEADS = 8
ATT_WIDTH = N_HEADS * HEAD_DIM
IDX_HEADS = 4
IDX_DIM = 64
IDX_ROPE_DIM = 32
INDEX_TOPK = 256
POOL_WINDOWS = (2, 4, 8, 16)
POOL_CH = 128
POOL_WIDTH = POOL_CH * len(POOL_WINDOWS)
ROPE_THETA = 10000.0
NORM_EPS = 1e-6
WI_ROWS = 8
V_ROWS = HEAD_DIM + 16
PROJ_GROUP = 512

VMEM_LIMIT_BYTES = 56 * 1024 * 1024

F32 = jnp.float32
BF16 = jnp.bfloat16
NT_DIMS = (((1,), (1,)), ((), ()))


def _rms(x, g):
    ms = jnp.mean(x * x, axis=-1, keepdims=True)
    return x * lax.rsqrt(ms + NORM_EPS) * g


def _ffn_kernel(x_ref, g_ref, wg_ref, wu_ref, wd_ref, *rest, tf):
    *mixer, o_ref, act_ref = rest
    x = x_ref[...]
    if mixer:
        attn_ref, pooled_ref, wo_ref = mixer
        mixed = jnp.concatenate([attn_ref[...], pooled_ref[...]], axis=1)
        x = x + jnp.dot(mixed, wo_ref[...], preferred_element_type=F32)
    h = _rms(x, g_ref[...]).astype(BF16)
    for c in range(wg_ref.shape[1] // tf):
        cols = slice(c * tf, (c + 1) * tf)
        gate = jnp.dot(h, wg_ref[:, cols], preferred_element_type=F32)
        up = jnp.dot(h, wu_ref[:, cols], preferred_element_type=F32)
        act_ref[:, cols] = (gate * jax.nn.sigmoid(gate) * up).astype(BF16)
    o_ref[...] = x + 0.5 * jnp.dot(act_ref[...], wd_ref[...], preferred_element_type=F32)


def _resident(shape):
    return pl.BlockSpec(shape, lambda *_: (0,) * len(shape), pipeline_mode=pl.Buffered(1))


def _ffn(x, g, wg, wu, wd, mixer=(), *, tm=512, tf=256):
    n, d = x.shape
    f = wg.shape[1]
    rows = lambda i: (i, 0)
    mixer_specs = []
    if mixer:
        attn, pooled, wo = mixer
        mixer_specs = [pl.BlockSpec((tm, attn.shape[1]), rows), pl.BlockSpec((tm, pooled.shape[1]), rows),
                       _resident(wo.shape)]
    return pl.pallas_call(
        functools.partial(_ffn_kernel, tf=tf),
        grid=(n // tm,),
        in_specs=[
            pl.BlockSpec((tm, d), rows),
            _resident((1, d)),
            _resident((d, f)),
            _resident((d, f)),
            _resident((f, d)),
            *mixer_specs,
        ],
        out_specs=pl.BlockSpec((tm, d), rows),
        out_shape=jax.ShapeDtypeStruct((n, d), F32),
        scratch_shapes=[pltpu.VMEM((tm, f), BF16)],
        compiler_params=pltpu.CompilerParams(
            dimension_semantics=("parallel",),
            vmem_limit_bytes=VMEM_LIMIT_BYTES),
        name="ffn_mix" if mixer else "ffn",
    )(x, g, wg, wu, wd, *mixer)


def _rope_rows(x, cos, sin, half):
    x1, x2 = x[:half], x[half:2 * half]
    return x1 * cos - x2 * sin, x2 * cos + x1 * sin


def _proj_kernel(x_ref, g_ref, wt_ref, wu_ref, gq_ref, gk_ref, cos_ref, sin_ref,
                 cosi_ref, sini_ref,
                 qT_ref, kp_ref, vT_ref, qiT_ref, ki_ref, wiT_ref, u_ref, *, idx_scale, q_scale):
    h = _rms(x_ref[...], g_ref[...]).astype(BF16)
    u_ref[...] = jnp.dot(h, wu_ref[...], preferred_element_type=F32)

    class _FeatureMajorProjection:
        shape = (wt_ref.shape[0], h.shape[0])

        def __init__(self):
            self._groups = {}

        def __getitem__(self, rows):
            g = rows.start // PROJ_GROUP
            assert (rows.stop - 1) // PROJ_GROUP == g
            if g not in self._groups:
                g0 = g * PROJ_GROUP
                g1 = min(g0 + PROJ_GROUP, self.shape[0])
                self._groups[g] = lax.dot_general(wt_ref[g0:g1, :], h, NT_DIMS,
                                                  preferred_element_type=F32)
            return self._groups[g][rows.start - g * PROJ_GROUP:rows.stop - g * PROJ_GROUP]

    pT = _FeatureMajorProjection()

    cos, sin = cos_ref[...], sin_ref[...]
    gq, gk = gq_ref[...], gk_ref[...]
    half = HEAD_DIM // 2

    def head_norm_rope(rows, gain):
        ms = jnp.mean(rows * rows, axis=0, keepdims=True)
        return _rope_rows(rows * lax.rsqrt(ms + NORM_EPS) * gain, cos, sin, half)

    for hd in range(N_HEADS):
        r0 = hd * HEAD_DIM
        a, b = head_norm_rope(pT[r0:r0 + HEAD_DIM], gq)
        qT_ref[0, r0:r0 + half, :] = (a * q_scale).astype(BF16)
        qT_ref[0, r0 + half:r0 + HEAD_DIM, :] = (b * q_scale).astype(BF16)

    k0 = ATT_WIDTH
    for pair in range(N_HEADS // 2):
        parts = []
        for e in range(2):
            r0 = k0 + (2 * pair + e) * HEAD_DIM
            parts.extend(head_norm_rope(pT[r0:r0 + HEAD_DIM], gk))
        kp_ref[0, pair] = jnp.concatenate(parts, axis=0).T.astype(BF16)

    v0 = 2 * ATT_WIDTH
    tt = pT.shape[1]
    ones_rows = jnp.where(lax.broadcasted_iota(jnp.int32, (V_ROWS - HEAD_DIM, tt), 0) == 0,
                          1.0, 0.0).astype(BF16)
    for hd in range(N_HEADS):
        r0 = v0 + hd * HEAD_DIM
        vT_ref[0, hd, :HEAD_DIM, :] = pT[r0:r0 + HEAD_DIM].astype(BF16)
        vT_ref[0, hd, HEAD_DIM:, :] = ones_rows

    cosi, sini = cosi_ref[...], sini_ref[...]
    ihalf = IDX_ROPE_DIM // 2
    qi0 = 3 * ATT_WIDTH
    for hi in range(IDX_HEADS):
        r0 = qi0 + hi * IDX_DIM
        a, b = _rope_rows(pT[r0:r0 + IDX_DIM], cosi, sini, ihalf)
        qiT_ref[0, r0 - qi0:r0 - qi0 + ihalf, :] = a.astype(BF16)
        qiT_ref[0, r0 - qi0 + ihalf:r0 - qi0 + IDX_ROPE_DIM, :] = b.astype(BF16)
        qiT_ref[0, r0 - qi0 + IDX_ROPE_DIM:r0 - qi0 + IDX_DIM, :] = (
            pT[r0 + IDX_ROPE_DIM:r0 + IDX_DIM].astype(BF16))

    ki0 = qi0 + IDX_HEADS * IDX_DIM
    a, b = _rope_rows(pT[ki0:ki0 + IDX_DIM], cosi, sini, ihalf)
    ki = jnp.concatenate([a, b, pT[ki0 + IDX_ROPE_DIM:ki0 + IDX_DIM]], axis=0)
    zeros = jnp.zeros_like(ki)
    ki_ref[0, 0] = jnp.concatenate([ki, zeros], axis=0).T.astype(BF16)
    ki_ref[0, 1] = jnp.concatenate([zeros, ki], axis=0).T.astype(BF16)

    wi0 = ki0 + IDX_DIM
    wiT_ref[0] = pT[wi0:wi0 + WI_ROWS] * idx_scale


def _proj(x1, g, wt, wu, gq, gk, cos, sin, cosi, sini, *, batch, seq, idx_scale, tt=512):
    n, d = x1.shape
    rows = wt.shape[0]
    nt = seq // tt
    tok = lambda b, t: (b * nt + t, 0)
    const = lambda b, t: (0, 0)
    featmaj = lambda b, t: (b, 0, t)
    out_shape = (
        jax.ShapeDtypeStruct((batch, ATT_WIDTH, seq), BF16),
        jax.ShapeDtypeStruct((batch, N_HEADS // 2, seq, 2 * HEAD_DIM), BF16),
        jax.ShapeDtypeStruct((batch, N_HEADS, V_ROWS, seq), BF16),
        jax.ShapeDtypeStruct((batch, IDX_HEADS * IDX_DIM, seq), BF16),
        jax.ShapeDtypeStruct((batch, 2, seq, 2 * IDX_DIM), BF16),
        jax.ShapeDtypeStruct((batch, WI_ROWS, seq), F32),
        jax.ShapeDtypeStruct((n, POOL_WIDTH), F32),
    )
    out_specs = (
        pl.BlockSpec((1, ATT_WIDTH, tt), featmaj),
        pl.BlockSpec((1, N_HEADS // 2, tt, 2 * HEAD_DIM), lambda b, t: (b, 0, t, 0)),
        pl.BlockSpec((1, N_HEADS, V_ROWS, tt), lambda b, t: (b, 0, 0, t)),
        pl.BlockSpec((1, IDX_HEADS * IDX_DIM, tt), featmaj),
        pl.BlockSpec((1, 2, tt, 2 * IDX_DIM), lambda b, t: (b, 0, t, 0)),
        pl.BlockSpec((1, WI_ROWS, tt), featmaj),
        pl.BlockSpec((tt, POOL_WIDTH), tok),
    )
    in_specs = [
        pl.BlockSpec((tt, d), tok),
        pl.BlockSpec((1, d), const),
        pl.BlockSpec((rows, d), const),
        pl.BlockSpec((d, POOL_WIDTH), const),
        pl.BlockSpec((HEAD_DIM, 1), const),
        pl.BlockSpec((HEAD_DIM, 1), const),
        pl.BlockSpec((HEAD_DIM // 2, tt), lambda b, t: (0, t)),
        pl.BlockSpec((HEAD_DIM // 2, tt), lambda b, t: (0, t)),
        pl.BlockSpec((IDX_ROPE_DIM // 2, tt), lambda b, t: (0, t)),
        pl.BlockSpec((IDX_ROPE_DIM // 2, tt), lambda b, t: (0, t)),
    ]
    return pl.pallas_call(
        functools.partial(_proj_kernel, idx_scale=idx_scale,
                          q_scale=HEAD_DIM ** -0.5 * math.log2(math.e)),
        grid=(batch, nt),
        in_specs=in_specs,
        out_specs=out_specs,
        out_shape=out_shape,
        compiler_params=pltpu.CompilerParams(
            dimension_semantics=("parallel", "parallel"),
            vmem_limit_bytes=VMEM_LIMIT_BYTES),
        name="proj",
    )(x1, g, wt, wu, gq, gk, cos, sin, cosi, sini)


INT_MIN = np.int32(-2 ** 31)
COUNT_ROWS = 128
COUNT_ROWS_F32 = 64
TIE_CHUNK = 256
COARSE_BITS = 16
FINE_BITS = 17


def _count(pred):
    ones = jnp.where(pred, jnp.bfloat16(1), jnp.bfloat16(0))
    rows, cols = ones.shape
    groups = ones.reshape(rows // COUNT_ROWS, COUNT_ROWS, cols)
    part = groups[0]
    for i in range(1, rows // COUNT_ROWS):
        part = part + groups[i]
    return part.astype(F32).sum(axis=0, keepdims=True).astype(jnp.int32)


def _count_f32(pred):
    ones = jnp.where(pred, 1.0, 0.0)
    rows, cols = ones.shape
    part = ones.reshape(rows // COUNT_ROWS_F32, COUNT_ROWS_F32, cols).sum(axis=0)
    return part.sum(axis=0, keepdims=True).astype(jnp.int32)


def _value_of_key(key):
    return pltpu.bitcast(jnp.where(key < 0, INT_MIN - key, key), F32)


class _Pacer:
    def __init__(self):
        self.head_token = None
        self.pass_token = None
        self._select_gate = None
        self._attend_gate = None

    def head_done(self, positive_row):
        self.head_token = positive_row < 0.0

    def pass_done(self, nonnegative_row):
        self.pass_token = nonnegative_row < 0

    def open_gates(self, head_token, pass_token):
        self._select_gate, self._attend_gate = head_token, pass_token

    def wait(self, value):
        if self._select_gate is None:
            return value
        value, self._select_gate = value | jnp.where(self._select_gate, 1, 0), None
        return value

    def wait_attend(self, value):
        if self._attend_gate is None:
            return value
        value, self._attend_gate = value + jnp.where(self._attend_gate, 1.0, 0.0).astype(BF16), None
        return value


def _topk_bias_steps(score, causal_bias, score_ref, coarse_ref, bias_out_ref, top_k, pacer):
    seq, tq = score.shape
    score_ref[...] = score
    coarse_ref[...] = score.astype(BF16)
    yield

    shift, mid = 32 - COARSE_BITS, 1 << (COARSE_BITS - 1)
    found = jnp.zeros((1, tq), jnp.int32)
    for bit in reversed(range(COARSE_BITS)):
        cand = pacer.wait(found) | (1 << bit)
        cand_value = _value_of_key(lax.shift_left(cand - mid, shift)).astype(BF16)
        found = jnp.where(_count(coarse_ref[...] >= cand_value) >= top_k, cand, found)
        pacer.pass_done(found)
        yield
    base = lax.shift_left(found - mid, shift) - (1 << (shift - 1))
    offset = jnp.zeros((1, tq), jnp.int32)
    for bit in reversed(range(FINE_BITS)):
        cand = pacer.wait(offset) | (1 << bit)
        offset = jnp.where(_count_f32(score_ref[...] >= _value_of_key(base + cand)) >= top_k, cand, offset)
        pacer.pass_done(offset)
        yield
    thr = _value_of_key(base + offset)
    need = top_k - _count_f32(score_ref[...] > thr)

    c = TIE_CHUNK
    lower_tri = jnp.where(lax.broadcasted_iota(jnp.int32, (c, c), 0)
                          >= lax.broadcasted_iota(jnp.int32, (c, c), 1), 1.0, 0.0).astype(BF16)
    remaining = need.astype(F32)
    for r0 in range(0, seq, c):
        x = score_ref[r0:r0 + c]
        passes = causal_bias(r0, c)
        ties = x == thr
        rank = jnp.dot(lower_tri, jnp.where(ties, 1.0, 0.0).astype(BF16), preferred_element_type=F32)
        bias_out_ref[r0:r0 + c] = jnp.where(
            x > thr, passes,
            jnp.where(ties, jnp.where(rank <= remaining, passes, -jnp.inf), -jnp.inf)).astype(BF16)
        remaining = remaining - rank[c - 1:c, :]
        yield


def _head_steps(qT_ref, kp_ref, vT_ref, bias_ref, o_ref, pacer):
    tq = qT_ref.shape[2]
    half_rows = lax.broadcasted_iota(jnp.int32, (2 * HEAD_DIM, tq), 0) < HEAD_DIM
    outs = []
    for hd in range(N_HEADS):
        pair, e = divmod(hd, 2)
        qpair = qT_ref[0, 2 * HEAD_DIM * pair:2 * HEAD_DIM * (pair + 1), :]
        qz = pacer.wait_attend(jnp.where(half_rows if e == 0 else jnp.logical_not(half_rows), qpair,
                                         jnp.zeros_like(qpair)))
        logits = jnp.dot(kp_ref[0, pair], qz, preferred_element_type=F32).astype(BF16) + bias_ref[...]
        p = jnp.exp2(logits - jnp.max(logits, axis=0, keepdims=True))
        o_aug = jnp.dot(vT_ref[0, hd], p, preferred_element_type=F32)
        outs.append(o_aug[:HEAD_DIM] / o_aug[HEAD_DIM:HEAD_DIM + 1])
        pacer.head_done(o_aug[HEAD_DIM:HEAD_DIM + 1])
        yield
    o_ref[0] = jnp.concatenate(outs, axis=0).T.astype(BF16)


def _attn_kernel(qT_ref, kp_ref, vT_ref, qiT_ref, ki_ref, wiT_ref, _aliased_out_ref, o_ref,
                 score_ref, coarse_ref, bias_ref, *, top_k, first_block, n_blocks, n_steps):
    j = pl.program_id(0)

    @pl.when(j == 0)
    def _():
        _attn_step(qT_ref, kp_ref, vT_ref, qiT_ref, ki_ref, wiT_ref, o_ref, score_ref, coarse_ref,
                   bias_ref, top_k=top_k, first_block=first_block, n_blocks=n_blocks,
                   do_select=True, do_attend=False)

    @pl.when(jnp.logical_and(j > 0, j < n_steps))
    def _():
        _attn_step(qT_ref, kp_ref, vT_ref, qiT_ref, ki_ref, wiT_ref, o_ref, score_ref, coarse_ref,
                   bias_ref, top_k=top_k, first_block=first_block, n_blocks=n_blocks,
                   do_select=True, do_attend=True)

    @pl.when(j == n_steps)
    def _():
        _attn_step(qT_ref, kp_ref, vT_ref, qiT_ref, ki_ref, wiT_ref, o_ref, score_ref, coarse_ref,
                   bias_ref, top_k=top_k, first_block=first_block, n_blocks=n_blocks,
                   do_select=False, do_attend=True)


def _attn_step(qT_ref, kp_ref, vT_ref, qiT_ref, ki_ref, wiT_ref, o_ref, score_ref, coarse_ref,
               bias_ref, *, top_k, first_block, n_blocks, do_select, do_attend):
    j = pl.program_id(0)
    seq = kp_ref.shape[2]
    tq = qT_ref.shape[2]
    sel_slot = j % 2
    att_slot = 1 - sel_slot
    pacer = _Pacer()
    attend = _head_steps(qT_ref, kp_ref, vT_ref, bias_ref.at[att_slot], o_ref, pacer)
    if not do_select:
        for _ in attend:
            pass
        return

    block = first_block + j % n_blocks
    all_causal_rows = first_block * tq

    def causal(r0, rows):
        k_pos = r0 + lax.broadcasted_iota(jnp.int32, (rows, tq), 0)
        return k_pos <= block * tq + lax.broadcasted_iota(jnp.int32, (rows, tq), 1)

    def causal_bias(r0, rows):
        if r0 + rows <= all_causal_rows:
            return 0.0
        return jnp.where(causal(r0, rows), 0.0, -jnp.inf)

    wi = wiT_ref[0]
    score = jnp.zeros((seq, tq), F32)
    for hi in range(IDX_HEADS):
        pair, e = divmod(hi, 2)
        rel = jnp.dot(ki_ref[0, e], qiT_ref[0, 2 * IDX_DIM * pair:2 * IDX_DIM * (pair + 1), :],
                      preferred_element_type=F32)
        score = score + wi[hi:hi + 1, :] * jnp.maximum(rel, 0.0)
    tail = jnp.where(causal(all_causal_rows, seq - all_causal_rows), score[all_causal_rows:], -jnp.inf)
    score = jnp.concatenate([score[:all_causal_rows], tail], axis=0) if all_causal_rows else tail

    select = _topk_bias_steps(score, causal_bias, score_ref, coarse_ref, bias_ref.at[sel_slot], top_k,
                              pacer)
    if not do_attend:
        for _ in select:
            pass
        return
    n_select = 1 + COARSE_BITS + FINE_BITS + seq // TIE_CHUNK
    per_head = -(-n_select // N_HEADS)
    for _ in range(N_HEADS):
        pacer.open_gates(pacer.head_token, pacer.pass_token)
        next(attend)
        for _ in range(per_head):
            next(select, None)
    for _ in select:
        pass
    for _ in attend:
        pass


def _attn_class(qT, kp, vT, qiT, ki, wiT, out, *, top_k, tq, first_block, n_blocks):
    batch = qT.shape[0]
    seq = (first_block + n_blocks) * tq
    n_steps = batch * n_blocks

    def sel(j):
        return jnp.minimum(j, n_steps - 1)

    def att(j):
        return jnp.maximum(j - 1, 0)

    return pl.pallas_call(
        functools.partial(_attn_kernel, top_k=top_k, first_block=first_block, n_blocks=n_blocks,
                          n_steps=n_steps),
        grid=(n_steps + 1,),
        in_specs=[
            pl.BlockSpec((1, ATT_WIDTH, tq),
                         lambda j: (att(j) // n_blocks, 0, first_block + att(j) % n_blocks)),
            pl.BlockSpec((1, N_HEADS // 2, seq, 2 * HEAD_DIM), lambda j: (att(j) // n_blocks, 0, 0, 0)),
            pl.BlockSpec((1, N_HEADS, V_ROWS, seq), lambda j: (att(j) // n_blocks, 0, 0, 0)),
            pl.BlockSpec((1, IDX_HEADS * IDX_DIM, tq),
                         lambda j: (sel(j) // n_blocks, 0, first_block + sel(j) % n_blocks)),
            pl.BlockSpec((1, 2, seq, 2 * IDX_DIM), lambda j: (sel(j) // n_blocks, 0, 0, 0)),
            pl.BlockSpec((1, WI_ROWS, tq),
                         lambda j: (sel(j) // n_blocks, 0, first_block + sel(j) % n_blocks)),
            pl.BlockSpec(memory_space=pl.ANY),
        ],
        out_specs=pl.BlockSpec((1, tq, ATT_WIDTH),
                               lambda j: (att(j) // n_blocks, first_block + att(j) % n_blocks, 0)),
        out_shape=jax.ShapeDtypeStruct(out.shape, out.dtype),
        input_output_aliases={6: 0},
        scratch_shapes=[pltpu.VMEM((seq, tq), F32), pltpu.VMEM((seq, tq), BF16),
                        pltpu.VMEM((2, seq, tq), BF16)],
        compiler_params=pltpu.CompilerParams(
            dimension_semantics=("arbitrary",),
            vmem_limit_bytes=VMEM_LIMIT_BYTES),
        name=f"attn_k{seq}",
    )(qT, kp, vT, qiT, ki, wiT, out)


def _attn(qT, kp, vT, qiT, ki, wiT, *, top_k, tq=512, blocks_per_class=1):
    batch, width, seq = qT.shape
    out = jnp.zeros((batch, seq, width), BF16)
    for fb in range(0, seq // tq, blocks_per_class):
        out = _attn_class(qT, kp, vT, qiT, ki, wiT, out, top_k=top_k, tq=tq, first_block=fb,
                          n_blocks=blocks_per_class)
    return out


def _pool_kernel(u_ref, pw_ref, ps_ref, o_ref):
    seq = u_ref.shape[1]
    t = lax.broadcasted_iota(jnp.int32, (seq, POOL_CH), 0)
    for g, w in enumerate(POOL_WINDOWS):
        ug = u_ref[0, :, g * POOL_CH:(g + 1) * POOL_CH]
        s, sh = ug, 1
        while sh < w:
            s = s + jnp.where(t >= sh, pltpu.roll(s, sh, axis=0), 0.0)
            sh *= 2
        pooled = s / jnp.minimum(t + 1, w).astype(F32) - ug
        mixed = jnp.dot(pooled.astype(BF16), pw_ref[g], preferred_element_type=F32)
        o_ref[0, :, g * POOL_CH:(g + 1) * POOL_CH] = (
            mixed * ps_ref[:, g * POOL_CH:(g + 1) * POOL_CH]).astype(BF16)


def _pool(u, pw, ps):
    batch, seq, width = u.shape
    return pl.pallas_call(
        _pool_kernel,
        grid=(batch,),
        in_specs=[
            pl.BlockSpec((1, seq, width), lambda b: (b, 0, 0)),
            pl.BlockSpec(pw.shape, lambda b: (0, 0, 0)),
            pl.BlockSpec((1, width), lambda b: (0, 0)),
        ],
        out_specs=pl.BlockSpec((1, seq, width), lambda b: (b, 0, 0)),
        out_shape=jax.ShapeDtypeStruct((batch, seq, width), BF16),
        compiler_params=pltpu.CompilerParams(
            dimension_semantics=("parallel",), vmem_limit_bytes=VMEM_LIMIT_BYTES),
        name="pool",
    )(u, pw, ps)


def _rope_tables(seq, rot_dim):
    half = rot_dim // 2
    inv = ROPE_THETA ** (-jnp.arange(half, dtype=F32) / half)
    ang = inv[:, None] * jnp.arange(seq, dtype=F32)[None, :]
    return jnp.cos(ang), jnp.sin(ang)


def kernel(x, ffn1_norm, ffn1_w_gate, ffn1_w_up, ffn1_w_down, mix_norm, w_in, q_norm, k_norm,
           pool_w, pool_scale, w_out, ffn2_norm, ffn2_w_gate, ffn2_w_up, ffn2_w_down):
    batch, seq, d = x.shape
    depth = w_in.shape[0]
    top_k = min(INDEX_TOPK, seq // 4)
    idx_scale = (IDX_HEADS ** -0.5) * (IDX_DIM ** -0.5)
    cos, sin = _rope_tables(seq, HEAD_DIM)
    cosi, sini = _rope_tables(seq, IDX_ROPE_DIM)
    n_feat = 3 * ATT_WIDTH + IDX_HEADS * IDX_DIM + IDX_DIM
    xf = x.reshape(batch * seq, d)

    for l in range(depth):
        xf = _ffn(xf, ffn1_norm[l][None], ffn1_w_gate[l].astype(BF16), ffn1_w_up[l].astype(BF16),
                  ffn1_w_down[l].astype(BF16))

        w = w_in[l]
        w_feat = jnp.pad(w[:, :n_feat + IDX_HEADS], ((0, 0), (0, WI_ROWS - IDX_HEADS)))
        wt = w_feat.T.astype(BF16)
        wu = w[:, n_feat + IDX_HEADS:].astype(BF16)
        qT, kp, vT, qiT, ki, wiT, u = _proj(
            xf, mix_norm[l][None], wt, wu, q_norm[l][:, None], k_norm[l][:, None],
            cos, sin, cosi, sini, batch=batch, seq=seq, idx_scale=idx_scale)

        attn = _attn(qT, kp, vT, qiT, ki, wiT, top_k=top_k)
        pooled = _pool(u.reshape(batch, seq, POOL_WIDTH), pool_w[l].astype(BF16), pool_scale[l][None])

        mixer = (attn.reshape(batch * seq, ATT_WIDTH), pooled.reshape(batch * seq, POOL_WIDTH),
                 w_out[l].astype(BF16))
        xf = _ffn(xf, ffn2_norm[l][None], ffn2_w_gate[l].astype(BF16), ffn2_w_up[l].astype(BF16),
                  ffn2_w_down[l].astype(BF16), mixer)
    return xf.reshape(batch, seq, d)
```

```python
import functools
import math

import jax
import jax.numpy as jnp
import numpy as np
from jax import lax
from jax.experimental import pallas as pl
from jax.experimental.pallas import tpu as pltpu

HEAD_DIM = 64
N_HEADS = 8
ATT_WIDTH = N_HEADS * HEAD_DIM
IDX_HEADS = 4
IDX_DIM = 64
IDX_ROPE_DIM = 32
INDEX_TOPK = 256
POOL_WINDOWS = (2, 4, 8, 16)
POOL_CH = 128
POOL_WIDTH = POOL_CH * len(POOL_WINDOWS)
ROPE_THETA = 10000.0
NORM_EPS = 1e-6
WI_ROWS = 8
V_ROWS = HEAD_DIM + 16
PROJ_GROUP = 512

VMEM_LIMIT_BYTES = 56 * 1024 * 1024

F32 = jnp.float32
BF16 = jnp.bfloat16
NT_DIMS = (((1,), (1,)), ((), ()))


def _rms(x, g):
    ms = jnp.mean(x * x, axis=-1, keepdims=True)
    return x * lax.rsqrt(ms + NORM_EPS) * g


def _ffn_kernel(x_ref, g_ref, wg_ref, wu_ref, wd_ref, *rest, tf):
    *mixer, o_ref, act_ref = rest
    x = x_ref[...]
    if mixer:
        attn_ref, pooled_ref, wo_ref = mixer
        mixed = jnp.concatenate([attn_ref[...], pooled_ref[...]], axis=1)
        x = x + jnp.dot(mixed, wo_ref[...], preferred_element_type=F32)
    h = _rms(x, g_ref[...]).astype(BF16)
    for c in range(wg_ref.shape[1] // tf):
        cols = slice(c * tf, (c + 1) * tf)
        gate = jnp.dot(h, wg_ref[:, cols], preferred_element_type=F32)
        up = jnp.dot(h, wu_ref[:, cols], preferred_element_type=F32)
        act_ref[:, cols] = (gate * jax.nn.sigmoid(gate) * up).astype(BF16)
    o_ref[...] = x + 0.5 * jnp.dot(act_ref[...], wd_ref[...], preferred_element_type=F32)


def _resident(shape):
    return pl.BlockSpec(shape, lambda *_: (0,) * len(shape), pipeline_mode=pl.Buffered(1))


def _ffn(x, g, wg, wu, wd, mixer=(), *, tm=512, tf=256):
    n, d = x.shape
    f = wg.shape[1]
    rows = lambda i: (i, 0)
    mixer_specs = []
    if mixer:
        attn, pooled, wo = mixer
        mixer_specs = [pl.BlockSpec((tm, attn.shape[1]), rows), pl.BlockSpec((tm, pooled.shape[1]), rows),
                       _resident(wo.shape)]
    return pl.pallas_call(
        functools.partial(_ffn_kernel, tf=tf),
        grid=(n // tm,),
        in_specs=[
            pl.BlockSpec((tm, d), rows),
            _resident((1, d)),
            _resident((d, f)),
            _resident((d, f)),
            _resident((f, d)),
            *mixer_specs,
        ],
        out_specs=pl.BlockSpec((tm, d), rows),
        out_shape=jax.ShapeDtypeStruct((n, d), F32),
        scratch_shapes=[pltpu.VMEM((tm, f), BF16)],
        compiler_params=pltpu.CompilerParams(
            dimension_semantics=("parallel",),
            vmem_limit_bytes=VMEM_LIMIT_BYTES),
        name="ffn_mix" if mixer else "ffn",
    )(x, g, wg, wu, wd, *mixer)


def _rope_rows(x, cos, sin, half):
    x1, x2 = x[:half], x[half:2 * half]
    return x1 * cos - x2 * sin, x2 * cos + x1 * sin


def _proj_kernel(x_ref, g_ref, wt_ref, wu_ref, gq_ref, gk_ref, cos_ref, sin_ref,
                 cosi_ref, sini_ref,
                 qT_ref, kp_ref, vT_ref, qiT_ref, ki_ref, wiT_ref, u_ref, *, idx_scale, q_scale):
    h = _rms(x_ref[...], g_ref[...]).astype(BF16)
    u_ref[...] = jnp.dot(h, wu_ref[...], preferred_element_type=F32)

    class _FeatureMajorProjection:
        shape = (wt_ref.shape[0], h.shape[0])

        def __init__(self):
            self._groups = {}

        def __getitem__(self, rows):
            g = rows.start // PROJ_GROUP
            assert (rows.stop - 1) // PROJ_GROUP == g
            if g not in self._groups:
                g0 = g * PROJ_GROUP
                g1 = min(g0 + PROJ_GROUP, self.shape[0])
                self._groups[g] = lax.dot_general(wt_ref[g0:g1, :], h, NT_DIMS,
                                                  preferred_element_type=F32)
            return self._groups[g][rows.start - g * PROJ_GROUP:rows.stop - g * PROJ_GROUP]

    pT = _FeatureMajorProjection()

    cos, sin = cos_ref[...], sin_ref[...]
    gq, gk = gq_ref[...], gk_ref[...]
    half = HEAD_DIM // 2

    def head_norm_rope(rows, gain):
        ms = jnp.mean(rows * rows, axis=0, keepdims=True)
        return _rope_rows(rows * lax.rsqrt(ms + NORM_EPS) * gain, cos, sin, half)

    for hd in range(N_HEADS):
        r0 = hd * HEAD_DIM
        a, b = head_norm_rope(pT[r0:r0 + HEAD_DIM], gq)
        qT_ref[0, r0:r0 + half, :] = (a * q_scale).astype(BF16)
        qT_ref[0, r0 + half:r0 + HEAD_DIM, :] = (b * q_scale).astype(BF16)

    k0 = ATT_WIDTH
    for pair in range(N_HEADS // 2):
        parts = []
        for e in range(2):
            r0 = k0 + (2 * pair + e) * HEAD_DIM
            parts.extend(head_norm_rope(pT[r0:r0 + HEAD_DIM], gk))
        kp_ref[0, pair] = jnp.concatenate(parts, axis=0).T.astype(BF16)

    v0 = 2 * ATT_WIDTH
    tt = pT.shape[1]
    ones_rows = jnp.where(lax.broadcasted_iota(jnp.int32, (V_ROWS - HEAD_DIM, tt), 0) == 0,
                          1.0, 0.0).astype(BF16)
    for hd in range(N_HEADS):
        r0 = v0 + hd * HEAD_DIM
        vT_ref[0, hd, :HEAD_DIM, :] = pT[r0:r0 + HEAD_DIM].astype(BF16)
        vT_ref[0, hd, HEAD_DIM:, :] = ones_rows

    cosi, sini = cosi_ref[...], sini_ref[...]
    ihalf = IDX_ROPE_DIM // 2
    qi0 = 3 * ATT_WIDTH
    for hi in range(IDX_HEADS):
        r0 = qi0 + hi * IDX_DIM
        a, b = _rope_rows(pT[r0:r0 + IDX_DIM], cosi, sini, ihalf)
        qiT_ref[0, r0 - qi0:r0 - qi0 + ihalf, :] = a.astype(BF16)
        qiT_ref[0, r0 - qi0 + ihalf:r0 - qi0 + IDX_ROPE_DIM, :] = b.astype(BF16)
        qiT_ref[0, r0 - qi0 + IDX_ROPE_DIM:r0 - qi0 + IDX_DIM, :] = (
            pT[r0 + IDX_ROPE_DIM:r0 + IDX_DIM].astype(BF16))

    ki0 = qi0 + IDX_HEADS * IDX_DIM
    a, b = _rope_rows(pT[ki0:ki0 + IDX_DIM], cosi, sini, ihalf)
    ki = jnp.concatenate([a, b, pT[ki0 + IDX_ROPE_DIM:ki0 + IDX_DIM]], axis=0)
    zeros = jnp.zeros_like(ki)
    ki_ref[0, 0] = jnp.concatenate([ki, zeros], axis=0).T.astype(BF16)
    ki_ref[0, 1] = jnp.concatenate([zeros, ki], axis=0).T.astype(BF16)

    wi0 = ki0 + IDX_DIM
    wiT_ref[0] = pT[wi0:wi0 + WI_ROWS] * idx_scale


def _proj(x1, g, wt, wu, gq, gk, cos, sin, cosi, sini, *, batch, seq, idx_scale, tt=512):
    n, d = x1.shape
    rows = wt.shape[0]
    nt = seq // tt
    tok = lambda b, t: (b * nt + t, 0)
    const = lambda b, t: (0, 0)
    featmaj = lambda b, t: (b, 0, t)
    out_shape = (
        jax.ShapeDtypeStruct((batch, ATT_WIDTH, seq), BF16),
        jax.ShapeDtypeStruct((batch, N_HEADS // 2, seq, 2 * HEAD_DIM), BF16),
        jax.ShapeDtypeStruct((batch, N_HEADS, V_ROWS, seq), BF16),
        jax.ShapeDtypeStruct((batch, IDX_HEADS * IDX_DIM, seq), BF16),
        jax.ShapeDtypeStruct((batch, 2, seq, 2 * IDX_DIM), BF16),
        jax.ShapeDtypeStruct((batch, WI_ROWS, seq), F32),
        jax.ShapeDtypeStruct((n, POOL_WIDTH), F32),
    )
    out_specs = (
        pl.BlockSpec((1, ATT_WIDTH, tt), featmaj),
        pl.BlockSpec((1, N_HEADS // 2, tt, 2 * HEAD_DIM), lambda b, t: (b, 0, t, 0)),
        pl.BlockSpec((1, N_HEADS, V_ROWS, tt), lambda b, t: (b, 0, 0, t)),
        pl.BlockSpec((1, IDX_HEADS * IDX_DIM, tt), featmaj),
        pl.BlockSpec((1, 2, tt, 2 * IDX_DIM), lambda b, t: (b, 0, t, 0)),
        pl.BlockSpec((1, WI_ROWS, tt), featmaj),
        pl.BlockSpec((tt, POOL_WIDTH), tok),
    )
    in_specs = [
        pl.BlockSpec((tt, d), tok),
        pl.BlockSpec((1, d), const),
        pl.BlockSpec((rows, d), const),
        pl.BlockSpec((d, POOL_WIDTH), const),
        pl.BlockSpec((HEAD_DIM, 1), const),
        pl.BlockSpec((HEAD_DIM, 1), const),
        pl.BlockSpec((HEAD_DIM // 2, tt), lambda b, t: (0, t)),
        pl.BlockSpec((HEAD_DIM // 2, tt), lambda b, t: (0, t)),
        pl.BlockSpec((IDX_ROPE_DIM // 2, tt), lambda b, t: (0, t)),
        pl.BlockSpec((IDX_ROPE_DIM // 2, tt), lambda b, t: (0, t)),
    ]
    return pl.pallas_call(
        functools.partial(_proj_kernel, idx_scale=idx_scale,
                          q_scale=HEAD_DIM ** -0.5 * math.log2(math.e)),
        grid=(batch, nt),
        in_specs=in_specs,
        out_specs=out_specs,
        out_shape=out_shape,
        compiler_params=pltpu.CompilerParams(
            dimension_semantics=("parallel", "parallel"),
            vmem_limit_bytes=VMEM_LIMIT_BYTES),
        name="proj",
    )(x1, g, wt, wu, gq, gk, cos, sin, cosi, sini)


INT_MIN = np.int32(-2 ** 31)
COUNT_ROWS = 128
COUNT_ROWS_F32 = 64
TIE_CHUNK = 256
COARSE_BITS = 16
FINE_BITS = 17


def _count(pred):
    ones = jnp.where(pred, jnp.bfloat16(1), jnp.bfloat16(0))
    rows, cols = ones.shape
    groups = ones.reshape(rows // COUNT_ROWS, COUNT_ROWS, cols)
    part = groups[0]
    for i in range(1, rows // COUNT_ROWS):
        part = part + groups[i]
    return part.astype(F32).sum(axis=0, keepdims=True).astype(jnp.int32)


def _count_f32(pred):
    ones = jnp.where(pred, 1.0, 0.0)
    rows, cols = ones.shape
    part = ones.reshape(rows // COUNT_ROWS_F32, COUNT_ROWS_F32, cols).sum(axis=0)
    return part.sum(axis=0, keepdims=True).astype(jnp.int32)


def _value_of_key(key):
    return pltpu.bitcast(jnp.where(key < 0, INT_MIN - key, key), F32)


def _topk_bias(score, causal_bias, score_ref, coarse_ref, bias_out_ref, top_k):
    seq, tq = score.shape
    score_ref[...] = score
    coarse_ref[...] = score.astype(BF16)

    shift, mid = 32 - COARSE_BITS, 1 << (COARSE_BITS - 1)
    found = jnp.zeros((1, tq), jnp.int32)
    for bit in reversed(range(COARSE_BITS)):
        cand = found | (1 << bit)
        cand_value = _value_of_key(lax.shift_left(cand - mid, shift)).astype(BF16)
        found = jnp.where(_count(coarse_ref[...] >= cand_value) >= top_k, cand, found)
    base = lax.shift_left(found - mid, shift) - (1 << (shift - 1))
    offset = jnp.zeros((1, tq), jnp.int32)
    for bit in reversed(range(FINE_BITS)):
        cand = offset | (1 << bit)
        offset = jnp.where(_count_f32(score_ref[...] >= _value_of_key(base + cand)) >= top_k, cand, offset)
    thr = _value_of_key(base + offset)
    need = top_k - _count_f32(score_ref[...] > thr)

    c = TIE_CHUNK
    lower_tri = jnp.where(lax.broadcasted_iota(jnp.int32, (c, c), 0)
                          >= lax.broadcasted_iota(jnp.int32, (c, c), 1), 1.0, 0.0).astype(BF16)
    remaining = need.astype(F32)
    for r0 in range(0, seq, c):
        x = score_ref[r0:r0 + c]
        passes = causal_bias(r0, c)
        ties = x == thr
        rank = jnp.dot(lower_tri, jnp.where(ties, 1.0, 0.0).astype(BF16), preferred_element_type=F32)
        bias_out_ref[r0:r0 + c] = jnp.where(
            x > thr, passes,
            jnp.where(ties, jnp.where(rank <= remaining, passes, -jnp.inf), -jnp.inf)).astype(BF16)
        remaining = remaining - rank[c - 1:c, :]


def _attend_heads(qT_ref, kp_ref, vT_ref, bias_ref, o_ref):
    tq = qT_ref.shape[2]
    half_rows = lax.broadcasted_iota(jnp.int32, (2 * HEAD_DIM, tq), 0) < HEAD_DIM
    outs = []
    for hd in range(N_HEADS):
        pair, e = divmod(hd, 2)
        qpair = qT_ref[0, 2 * HEAD_DIM * pair:2 * HEAD_DIM * (pair + 1), :]
        qz = jnp.where(half_rows if e == 0 else jnp.logical_not(half_rows), qpair,
                       jnp.zeros_like(qpair))
        logits = jnp.dot(kp_ref[0, pair], qz, preferred_element_type=F32).astype(BF16) + bias_ref[...]
        p = jnp.exp2(logits - jnp.max(logits, axis=0, keepdims=True))
        o_aug = jnp.dot(vT_ref[0, hd], p, preferred_element_type=F32)
        outs.append(o_aug[:HEAD_DIM] / o_aug[HEAD_DIM:HEAD_DIM + 1])
    o_ref[0] = jnp.concatenate(outs, axis=0).T.astype(BF16)


def _attn_kernel(qT_ref, kp_ref, vT_ref, qiT_ref, ki_ref, wiT_ref, _aliased_out_ref, o_ref,
                 score_ref, coarse_ref, bias_ref, *, top_k, first_block, n_blocks, n_steps):
    j = pl.program_id(0)
    seq = kp_ref.shape[2]
    tq = qT_ref.shape[2]
    sel_slot = j % 2
    att_slot = 1 - sel_slot

    @pl.when(j == 0)
    def _():
        bias_ref[1] = jnp.zeros((seq, tq), BF16)

    block = first_block + jnp.minimum(j, n_steps - 1) % n_blocks
    all_causal_rows = first_block * tq

    def causal(r0, rows):
        k_pos = r0 + lax.broadcasted_iota(jnp.int32, (rows, tq), 0)
        return k_pos <= block * tq + lax.broadcasted_iota(jnp.int32, (rows, tq), 1)

    def causal_bias(r0, rows):
        if r0 + rows <= all_causal_rows:
            return 0.0
        return jnp.where(causal(r0, rows), 0.0, -jnp.inf)

    wi = wiT_ref[0]
    score = jnp.zeros((seq, tq), F32)
    for hi in range(IDX_HEADS):
        pair, e = divmod(hi, 2)
        rel = jnp.dot(ki_ref[0, e], qiT_ref[0, 2 * IDX_DIM * pair:2 * IDX_DIM * (pair + 1), :],
                      preferred_element_type=F32)
        score = score + wi[hi:hi + 1, :] * jnp.maximum(rel, 0.0)
    tail = jnp.where(causal(all_causal_rows, seq - all_causal_rows), score[all_causal_rows:], -jnp.inf)
    score = jnp.concatenate([score[:all_causal_rows], tail], axis=0) if all_causal_rows else tail

    _attend_heads(qT_ref, kp_ref, vT_ref, bias_ref.at[att_slot], o_ref)
    _topk_bias(score, causal_bias, score_ref, coarse_ref, bias_ref.at[sel_slot], top_k)


def _attn_class(qT, kp, vT, qiT, ki, wiT, out, *, top_k, tq, first_block, n_blocks):
    batch = qT.shape[0]
    seq = (first_block + n_blocks) * tq
    n_steps = batch * n_blocks

    def sel(j):
        return jnp.minimum(j, n_steps - 1)

    def att(j):
        return jnp.maximum(j - 1, 0)

    return pl.pallas_call(
        functools.partial(_attn_kernel, top_k=top_k, first_block=first_block, n_blocks=n_blocks,
                          n_steps=n_steps),
        grid=(n_steps + 1,),
        in_specs=[
            pl.BlockSpec((1, ATT_WIDTH, tq),
                         lambda j: (att(j) // n_blocks, 0, first_block + att(j) % n_blocks)),
            pl.BlockSpec((1, N_HEADS // 2, seq, 2 * HEAD_DIM), lambda j: (att(j) // n_blocks, 0, 0, 0)),
            pl.BlockSpec((1, N_HEADS, V_ROWS, seq), lambda j: (att(j) // n_blocks, 0, 0, 0)),
            pl.BlockSpec((1, IDX_HEADS * IDX_DIM, tq),
                         lambda j: (sel(j) // n_blocks, 0, first_block + sel(j) % n_blocks)),
            pl.BlockSpec((1, 2, seq, 2 * IDX_DIM), lambda j: (sel(j) // n_blocks, 0, 0, 0)),
            pl.BlockSpec((1, WI_ROWS, tq),
                         lambda j: (sel(j) // n_blocks, 0, first_block + sel(j) % n_blocks)),
            pl.BlockSpec(memory_space=pl.ANY),
        ],
        out_specs=pl.BlockSpec((1, tq, ATT_WIDTH),
                               lambda j: (att(j) // n_blocks, first_block + att(j) % n_blocks, 0)),
        out_shape=jax.ShapeDtypeStruct(out.shape, out.dtype),
        input_output_aliases={6: 0},
        scratch_shapes=[pltpu.VMEM((seq, tq), F32), pltpu.VMEM((seq, tq), BF16),
                        pltpu.VMEM((2, seq, tq), BF16)],
        compiler_params=pltpu.CompilerParams(
            dimension_semantics=("arbitrary",),
            vmem_limit_bytes=VMEM_LIMIT_BYTES),
        name=f"attn_k{seq}",
    )(qT, kp, vT, qiT, ki, wiT, out)


def _attn(qT, kp, vT, qiT, ki, wiT, *, top_k, tq=512, blocks_per_class=1):
    batch, width, seq = qT.shape
    out = jnp.zeros((batch, seq, width), BF16)
    for fb in range(0, seq // tq, blocks_per_class):
        out = _attn_class(qT, kp, vT, qiT, ki, wiT, out, top_k=top_k, tq=tq, first_block=fb,
                          n_blocks=blocks_per_class)
    return out


def _pool_kernel(u_ref, pw_ref, ps_ref, o_ref):
    seq = u_ref.shape[1]
    t = lax.broadcasted_iota(jnp.int32, (seq, POOL_CH), 0)
    for g, w in enumerate(POOL_WINDOWS):
        ug = u_ref[0, :, g * POOL_CH:(g + 1) * POOL_CH]
        s, sh = ug, 1
        while sh < w:
            s = s + jnp.where(t >= sh, pltpu.roll(s, sh, axis=0), 0.0)
            sh *= 2
        pooled = s / jnp.minimum(t + 1, w).astype(F32) - ug
        mixed = jnp.dot(pooled.astype(BF16), pw_ref[g], preferred_element_type=F32)
        o_ref[0, :, g * POOL_CH:(g + 1) * POOL_CH] = (
            mixed * ps_ref[:, g * POOL_CH:(g + 1) * POOL_CH]).astype(BF16)


def _pool(u, pw, ps):
    batch, seq, width = u.shape
    return pl.pallas_call(
        _pool_kernel,
        grid=(batch,),
        in_specs=[
            pl.BlockSpec((1, seq, width), lambda b: (b, 0, 0)),
            pl.BlockSpec(pw.shape, lambda b: (0, 0, 0)),
            pl.BlockSpec((1, width), lambda b: (0, 0)),
        ],
        out_specs=pl.BlockSpec((1, seq, width), lambda b: (b, 0, 0)),
        out_shape=jax.ShapeDtypeStruct((batch, seq, width), BF16),
        compiler_params=pltpu.CompilerParams(
            dimension_semantics=("parallel",), vmem_limit_bytes=VMEM_LIMIT_BYTES),
        name="pool",
    )(u, pw, ps)


def _rope_tables(seq, rot_dim):
    half = rot_dim // 2
    inv = ROPE_THETA ** (-jnp.arange(half, dtype=F32) / half)
    ang = inv[:, None] * jnp.arange(seq, dtype=F32)[None, :]
    return jnp.cos(ang), jnp.sin(ang)


def kernel(x, ffn1_norm, ffn1_w_gate, ffn1_w_up, ffn1_w_down, mix_norm, w_in, q_norm, k_norm,
           pool_w, pool_scale, w_out, ffn2_norm, ffn2_w_gate, ffn2_w_up, ffn2_w_down):
    batch, seq, d = x.shape
    depth = w_in.shape[0]
    top_k = min(INDEX_TOPK, seq // 4)
    idx_scale = (IDX_HEADS ** -0.5) * (IDX_DIM ** -0.5)
    cos, sin = _rope_tables(seq, HEAD_DIM)
    cosi, sini = _rope_tables(seq, IDX_ROPE_DIM)
    n_feat = 3 * ATT_WIDTH + IDX_HEADS * IDX_DIM + IDX_DIM
    xf = x.reshape(batch * seq, d)

    for l in range(depth):
        xf = _ffn(xf, ffn1_norm[l][None], ffn1_w_gate[l].astype(BF16), ffn1_w_up[l].astype(BF16),
                  ffn1_w_down[l].astype(BF16))

        w = w_in[l]
        w_feat = jnp.pad(w[:, :n_feat + IDX_HEADS], ((0, 0), (0, WI_ROWS - IDX_HEADS)))
        wt = w_feat.T.astype(BF16)
        wu = w[:, n_feat + IDX_HEADS:].astype(BF16)
        qT, kp, vT, qiT, ki, wiT, u = _proj(
            xf, mix_norm[l][None], wt, wu, q_norm[l][:, None], k_norm[l][:, None],
            cos, sin, cosi, sini, batch=batch, seq=seq, idx_scale=idx_scale)

        attn = _attn(qT, kp, vT, qiT, ki, wiT, top_k=top_k)
        pooled = _pool(u.reshape(batch, seq, POOL_WIDTH), pool_w[l].astype(BF16), pool_scale[l][None])

        mixer = (attn.reshape(batch * seq, ATT_WIDTH), pooled.reshape(batch * seq, POOL_WIDTH),
                 w_out[l].astype(BF16))
        xf = _ffn(xf, ffn2_norm[l][None], ffn2_w_gate[l].astype(BF16), ffn2_w_up[l].astype(BF16),
                  ffn2_w_down[l].astype(BF16), mixer)
    return xf.reshape(batch, seq, d)
```

```python
import functools
import math

import jax
import jax.numpy as jnp
import numpy as np
from jax import lax
from jax.experimental import pallas as pl
from jax.experimental.pallas import tpu as pltpu

HEAD_DIM = 64
N_HEADS = 8
ATT_WIDTH = N_HEADS * HEAD_DIM
IDX_HEADS = 4
IDX_DIM = 64
IDX_ROPE_DIM = 32
INDEX_TOPK = 256
POOL_WINDOWS = (2, 4, 8, 16)
POOL_CH = 128
POOL_WIDTH = POOL_CH * len(POOL_WINDOWS)
ROPE_THETA = 10000.0
NORM_EPS = 1e-6
WI_ROWS = 8
V_ROWS = HEAD_DIM + 16
PROJ_GROUP = 512
FEATURE_ROWS = 3 * ATT_WIDTH + IDX_HEADS * IDX_DIM + IDX_DIM + WI_ROWS

VMEM_LIMIT_BYTES = 56 * 1024 * 1024

F32 = jnp.float32
BF16 = jnp.bfloat16
NT_DIMS = (((1,), (1,)), ((), ()))


def _rms(x, g):
    ms = jnp.mean(x * x, axis=-1, keepdims=True)
    return x * lax.rsqrt(ms + NORM_EPS) * g


def _ffn_kernel(x_ref, g_ref, wg_ref, wu_ref, wd_ref, *rest, tf):
    *mixer, o_ref, act_ref = rest
    x = x_ref[...]
    if mixer:
        attn_ref, pooled_ref, wo_ref = mixer
        mixed = jnp.concatenate([attn_ref[...], pooled_ref[...]], axis=1)
        x = x + jnp.dot(mixed, wo_ref[...], preferred_element_type=F32)
    h = _rms(x, g_ref[...]).astype(BF16)
    for c in range(wg_ref.shape[1] // tf):
        cols = slice(c * tf, (c + 1) * tf)
        gate = jnp.dot(h, wg_ref[:, cols], preferred_element_type=F32)
        up = jnp.dot(h, wu_ref[:, cols], preferred_element_type=F32)
        act_ref[:, cols] = (gate * jax.nn.sigmoid(gate) * up).astype(BF16)
    o_ref[...] = x + 0.5 * jnp.dot(act_ref[...], wd_ref[...], preferred_element_type=F32)


def _resident(shape):
    return pl.BlockSpec(shape, lambda *_: (0,) * len(shape), pipeline_mode=pl.Buffered(1))


def _ffn(x, g, wg, wu, wd, mixer=(), *, tm=512, tf=256):
    n, d = x.shape
    f = wg.shape[1]
    rows = lambda i: (i, 0)
    mixer_specs = []
    if mixer:
        attn, pooled, wo = mixer
        mixer_specs = [pl.BlockSpec((tm, attn.shape[1]), rows), pl.BlockSpec((tm, pooled.shape[1]), rows),
                       _resident(wo.shape)]
    return pl.pallas_call(
        functools.partial(_ffn_kernel, tf=tf),
        grid=(n // tm,),
        in_specs=[
            pl.BlockSpec((tm, d), rows),
            _resident((1, d)),
            _resident((d, f)),
            _resident((d, f)),
            _resident((f, d)),
            *mixer_specs,
        ],
        out_specs=pl.BlockSpec((tm, d), rows),
        out_shape=jax.ShapeDtypeStruct((n, d), F32),
        scratch_shapes=[pltpu.VMEM((tm, f), BF16)],
        compiler_params=pltpu.CompilerParams(
            dimension_semantics=("parallel",),
            vmem_limit_bytes=VMEM_LIMIT_BYTES),
        name="ffn_mix" if mixer else "ffn",
    )(x, g, wg, wu, wd, *mixer)


LANES = 128


def _prep_w_in_kernel(w_ref, wt_ref, wu_ref, *, n_feat):
    for r0 in range(0, wt_ref.shape[0], LANES):
        wt_ref[r0:r0 + LANES, :] = w_ref[:, r0:r0 + LANES].T.astype(BF16)
    wu_ref[...] = w_ref[:, n_feat:].astype(BF16)


def _prep_w_in(w, n_feat):
    d, cols = w.shape
    rows = -(-n_feat // LANES) * LANES
    assert rows <= cols
    return pl.pallas_call(
        functools.partial(_prep_w_in_kernel, n_feat=n_feat),
        out_shape=(jax.ShapeDtypeStruct((rows, d), BF16), jax.ShapeDtypeStruct((d, cols - n_feat), BF16)),
        compiler_params=pltpu.CompilerParams(vmem_limit_bytes=VMEM_LIMIT_BYTES),
        name="prep_w_in",
    )(w)


def _rope_rows(x, cos, sin, half):
    x1, x2 = x[:half], x[half:2 * half]
    return x1 * cos - x2 * sin, x2 * cos + x1 * sin


def _proj_kernel(x_ref, g_ref, wt_ref, wu_ref, gq_ref, gk_ref, cos_ref, sin_ref,
                 cosi_ref, sini_ref,
                 qT_ref, kp_ref, vT_ref, qiT_ref, ki_ref, wiT_ref, u_ref, *, idx_scale, q_scale):
    h = _rms(x_ref[...], g_ref[...]).astype(BF16)
    u_ref[...] = jnp.dot(h, wu_ref[...], preferred_element_type=F32)

    class _FeatureMajorProjection:
        shape = (FEATURE_ROWS, h.shape[0])

        def __init__(self):
            self._groups = {}

        def __getitem__(self, rows):
            g = rows.start // PROJ_GROUP
            assert (rows.stop - 1) // PROJ_GROUP == g
            if g not in self._groups:
                g0 = g * PROJ_GROUP
                g1 = min(g0 + PROJ_GROUP, self.shape[0])
                self._groups[g] = lax.dot_general(wt_ref[g0:g1, :], h, NT_DIMS,
                                                  preferred_element_type=F32)
            return self._groups[g][rows.start - g * PROJ_GROUP:rows.stop - g * PROJ_GROUP]

    pT = _FeatureMajorProjection()

    cos, sin = cos_ref[...], sin_ref[...]
    gq, gk = gq_ref[...], gk_ref[...]
    half = HEAD_DIM // 2

    def head_norm_rope(rows, gain):
        ms = jnp.mean(rows * rows, axis=0, keepdims=True)
        return _rope_rows(rows * lax.rsqrt(ms + NORM_EPS) * gain, cos, sin, half)

    for hd in range(N_HEADS):
        r0 = hd * HEAD_DIM
        a, b = head_norm_rope(pT[r0:r0 + HEAD_DIM], gq)
        qT_ref[0, r0:r0 + half, :] = (a * q_scale).astype(BF16)
        qT_ref[0, r0 + half:r0 + HEAD_DIM, :] = (b * q_scale).astype(BF16)

    k0 = ATT_WIDTH
    for pair in range(N_HEADS // 2):
        parts = []
        for e in range(2):
            r0 = k0 + (2 * pair + e) * HEAD_DIM
            parts.extend(head_norm_rope(pT[r0:r0 + HEAD_DIM], gk))
        kp_ref[0, pair] = jnp.concatenate(parts, axis=0).T.astype(BF16)

    v0 = 2 * ATT_WIDTH
    tt = pT.shape[1]
    ones_rows = jnp.where(lax.broadcasted_iota(jnp.int32, (V_ROWS - HEAD_DIM, tt), 0) == 0,
                          1.0, 0.0).astype(BF16)
    for hd in range(N_HEADS):
        r0 = v0 + hd * HEAD_DIM
        vT_ref[0, hd, :HEAD_DIM, :] = pT[r0:r0 + HEAD_DIM].astype(BF16)
        vT_ref[0, hd, HEAD_DIM:, :] = ones_rows

    cosi, sini = cosi_ref[...], sini_ref[...]
    ihalf = IDX_ROPE_DIM // 2
    qi0 = 3 * ATT_WIDTH
    for hi in range(IDX_HEADS):
        r0 = qi0 + hi * IDX_DIM
        a, b = _rope_rows(pT[r0:r0 + IDX_DIM], cosi, sini, ihalf)
        qiT_ref[0, r0 - qi0:r0 - qi0 + ihalf, :] = a.astype(BF16)
        qiT_ref[0, r0 - qi0 + ihalf:r0 - qi0 + IDX_ROPE_DIM, :] = b.astype(BF16)
        qiT_ref[0, r0 - qi0 + IDX_ROPE_DIM:r0 - qi0 + IDX_DIM, :] = (
            pT[r0 + IDX_ROPE_DIM:r0 + IDX_DIM].astype(BF16))

    ki0 = qi0 + IDX_HEADS * IDX_DIM
    a, b = _rope_rows(pT[ki0:ki0 + IDX_DIM], cosi, sini, ihalf)
    ki = jnp.concatenate([a, b, pT[ki0 + IDX_ROPE_DIM:ki0 + IDX_DIM]], axis=0)
    zeros = jnp.zeros_like(ki)
    ki_ref[0, 0] = jnp.concatenate([ki, zeros], axis=0).T.astype(BF16)
    ki_ref[0, 1] = jnp.concatenate([zeros, ki], axis=0).T.astype(BF16)

    wi0 = ki0 + IDX_DIM
    wiT_ref[0] = pT[wi0:wi0 + WI_ROWS] * idx_scale


def _proj(x1, g, wt, wu, gq, gk, cos, sin, cosi, sini, *, batch, seq, idx_scale, tt=512):
    n, d = x1.shape
    rows = wt.shape[0]
    nt = seq // tt
    tok = lambda b, t: (b * nt + t, 0)
    const = lambda b, t: (0, 0)
    featmaj = lambda b, t: (b, 0, t)
    out_shape = (
        jax.ShapeDtypeStruct((batch, ATT_WIDTH, seq), BF16),
        jax.ShapeDtypeStruct((batch, N_HEADS // 2, seq, 2 * HEAD_DIM), BF16),
        jax.ShapeDtypeStruct((batch, N_HEADS, V_ROWS, seq), BF16),
        jax.ShapeDtypeStruct((batch, IDX_HEADS * IDX_DIM, seq), BF16),
        jax.ShapeDtypeStruct((batch, 2, seq, 2 * IDX_DIM), BF16),
        jax.ShapeDtypeStruct((batch, WI_ROWS, seq), F32),
        jax.ShapeDtypeStruct((n, POOL_WIDTH), F32),
    )
    out_specs = (
        pl.BlockSpec((1, ATT_WIDTH, tt), featmaj),
        pl.BlockSpec((1, N_HEADS // 2, tt, 2 * HEAD_DIM), lambda b, t: (b, 0, t, 0)),
        pl.BlockSpec((1, N_HEADS, V_ROWS, tt), lambda b, t: (b, 0, 0, t)),
        pl.BlockSpec((1, IDX_HEADS * IDX_DIM, tt), featmaj),
        pl.BlockSpec((1, 2, tt, 2 * IDX_DIM), lambda b, t: (b, 0, t, 0)),
        pl.BlockSpec((1, WI_ROWS, tt), featmaj),
        pl.BlockSpec((tt, POOL_WIDTH), tok),
    )
    in_specs = [
        pl.BlockSpec((tt, d), tok),
        pl.BlockSpec((1, d), const),
        pl.BlockSpec((rows, d), const),
        pl.BlockSpec((d, POOL_WIDTH), const),
        pl.BlockSpec((HEAD_DIM, 1), const),
        pl.BlockSpec((HEAD_DIM, 1), const),
        pl.BlockSpec((HEAD_DIM // 2, tt), lambda b, t: (0, t)),
        pl.BlockSpec((HEAD_DIM // 2, tt), lambda b, t: (0, t)),
        pl.BlockSpec((IDX_ROPE_DIM // 2, tt), lambda b, t: (0, t)),
        pl.BlockSpec((IDX_ROPE_DIM // 2, tt), lambda b, t: (0, t)),
    ]
    return pl.pallas_call(
        functools.partial(_proj_kernel, idx_scale=idx_scale,
                          q_scale=HEAD_DIM ** -0.5 * math.log2(math.e)),
        grid=(batch, nt),
        in_specs=in_specs,
        out_specs=out_specs,
        out_shape=out_shape,
        compiler_params=pltpu.CompilerParams(
            dimension_semantics=("parallel", "parallel"),
            vmem_limit_bytes=VMEM_LIMIT_BYTES),
        name="proj",
    )(x1, g, wt, wu, gq, gk, cos, sin, cosi, sini)


INT_MIN = np.int32(-2 ** 31)
COUNT_ROWS = 64
COUNT_ROWS_F32 = 32
TIE_CHUNK = 256
COARSE_BITS = 16
FINE_BITS = 17


def _count(pred):
    ones = jnp.where(pred, jnp.bfloat16(1), jnp.bfloat16(0))
    rows, cols = ones.shape
    groups = ones.reshape(rows // COUNT_ROWS, COUNT_ROWS, cols)
    part = groups[0]
    for i in range(1, rows // COUNT_ROWS):
        part = part + groups[i]
    return part.astype(F32).sum(axis=0, keepdims=True).astype(jnp.int32)


def _count_f32(pred):
    ones = jnp.where(pred, 1.0, 0.0)
    rows, cols = ones.shape
    part = ones.reshape(rows // COUNT_ROWS_F32, COUNT_ROWS_F32, cols).sum(axis=0)
    return part.sum(axis=0, keepdims=True).astype(jnp.int32)


def _value_of_key(key):
    return pltpu.bitcast(jnp.where(key < 0, INT_MIN - key, key), F32)


def _topk_bias(score, causal_bias, score_ref, coarse_ref, bias_out_ref, top_k):
    seq, tq = score.shape
    score_ref[...] = score
    coarse_ref[...] = score.astype(BF16)

    shift, mid = 32 - COARSE_BITS, 1 << (COARSE_BITS - 1)
    found = jnp.zeros((1, tq), jnp.int32)
    for bit in reversed(range(COARSE_BITS)):
        cand = found | (1 << bit)
        cand_value = _value_of_key(lax.shift_left(cand - mid, shift)).astype(BF16)
        found = jnp.where(_count(coarse_ref[...] >= cand_value) >= top_k, cand, found)
    base = lax.shift_left(found - mid, shift) - (1 << (shift - 1))
    offset = jnp.zeros((1, tq), jnp.int32)
    for bit in reversed(range(FINE_BITS)):
        cand = offset | (1 << bit)
        offset = jnp.where(_count_f32(score_ref[...] >= _value_of_key(base + cand)) >= top_k, cand, offset)
    thr = _value_of_key(base + offset)
    need = top_k - _count_f32(score_ref[...] > thr)

    c = TIE_CHUNK
    lower_tri = jnp.where(lax.broadcasted_iota(jnp.int32, (c, c), 0)
                          >= lax.broadcasted_iota(jnp.int32, (c, c), 1), 1.0, 0.0).astype(BF16)
    remaining = need.astype(F32)
    for r0 in range(0, seq, c):
        x = score_ref[r0:r0 + c]
        passes = causal_bias(r0, c)
        ties = x == thr
        rank = jnp.dot(lower_tri, jnp.where(ties, 1.0, 0.0).astype(BF16), preferred_element_type=F32)
        bias_out_ref[r0:r0 + c] = jnp.where(
            x > thr, passes,
            jnp.where(ties, jnp.where(rank <= remaining, passes, -jnp.inf), -jnp.inf)).astype(BF16)
        remaining = remaining - rank[c - 1:c, :]


def _attend_heads(qT_ref, kp_ref, vT_ref, bias_ref, o_ref):
    tq = qT_ref.shape[2]
    half_rows = lax.broadcasted_iota(jnp.int32, (2 * HEAD_DIM, tq), 0) < HEAD_DIM
    outs = []
    for hd in range(N_HEADS):
        pair, e = divmod(hd, 2)
        qpair = qT_ref[0, 2 * HEAD_DIM * pair:2 * HEAD_DIM * (pair + 1), :]
        qz = jnp.where(half_rows if e == 0 else jnp.logical_not(half_rows), qpair,
                       jnp.zeros_like(qpair))
        logits = jnp.dot(kp_ref[0, pair], qz, preferred_element_type=F32).astype(BF16) + bias_ref[...]
        p = jnp.exp2(logits - jnp.max(logits, axis=0, keepdims=True))
        o_aug = jnp.dot(vT_ref[0, hd], p, preferred_element_type=F32)
        outs.append(o_aug[:HEAD_DIM] / o_aug[HEAD_DIM:HEAD_DIM + 1])
    o_ref[0] = jnp.concatenate(outs, axis=0).T.astype(BF16)


def _attn_kernel(qT_ref, kp_ref, vT_ref, qiT_ref, ki_ref, wiT_ref, _aliased_out_ref, o_ref,
                 score_ref, coarse_ref, bias_ref, *, top_k, first_block, n_blocks, n_steps):
    j = pl.program_id(0)
    seq = kp_ref.shape[2]
    tq = qT_ref.shape[2]
    sel_slot = j % 2
    att_slot = 1 - sel_slot

    @pl.when(j == 0)
    def _():
        bias_ref[1] = jnp.zeros((seq, tq), BF16)

    block = first_block + jnp.minimum(j, n_steps - 1) % n_blocks
    all_causal_rows = first_block * tq

    def causal(r0, rows):
        k_pos = r0 + lax.broadcasted_iota(jnp.int32, (rows, tq), 0)
        return k_pos <= block * tq + lax.broadcasted_iota(jnp.int32, (rows, tq), 1)

    def causal_bias(r0, rows):
        if r0 + rows <= all_causal_rows:
            return 0.0
        return jnp.where(causal(r0, rows), 0.0, -jnp.inf)

    wi = wiT_ref[0]
    score = jnp.zeros((seq, tq), F32)
    for hi in range(IDX_HEADS):
        pair, e = divmod(hi, 2)
        rel = jnp.dot(ki_ref[0, e], qiT_ref[0, 2 * IDX_DIM * pair:2 * IDX_DIM * (pair + 1), :],
                      preferred_element_type=F32)
        score = score + wi[hi:hi + 1, :] * jnp.maximum(rel, 0.0)
    tail = jnp.where(causal(all_causal_rows, seq - all_causal_rows), score[all_causal_rows:], -jnp.inf)
    score = jnp.concatenate([score[:all_causal_rows], tail], axis=0) if all_causal_rows else tail

    _attend_heads(qT_ref, kp_ref, vT_ref, bias_ref.at[att_slot], o_ref)
    _topk_bias(score, causal_bias, score_ref, coarse_ref, bias_ref.at[sel_slot], top_k)


def _attn_class(qT, kp, vT, qiT, ki, wiT, out, *, top_k, tq, first_block, n_blocks):
    batch = qT.shape[0]
    seq = (first_block + n_blocks) * tq
    n_steps = batch * n_blocks

    def sel(j):
        return jnp.minimum(j, n_steps - 1)

    def att(j):
        return jnp.maximum(j - 1, 0)

    return pl.pallas_call(
        functools.partial(_attn_kernel, top_k=top_k, first_block=first_block, n_blocks=n_blocks,
                          n_steps=n_steps),
        grid=(n_steps + 1,),
        in_specs=[
            pl.BlockSpec((1, ATT_WIDTH, tq),
                         lambda j: (att(j) // n_blocks, 0, first_block + att(j) % n_blocks)),
            pl.BlockSpec((1, N_HEADS // 2, seq, 2 * HEAD_DIM), lambda j: (att(j) // n_blocks, 0, 0, 0)),
            pl.BlockSpec((1, N_HEADS, V_ROWS, seq), lambda j: (att(j) // n_blocks, 0, 0, 0)),
            pl.BlockSpec((1, IDX_HEADS * IDX_DIM, tq),
                         lambda j: (sel(j) // n_blocks, 0, first_block + sel(j) % n_blocks)),
            pl.BlockSpec((1, 2, seq, 2 * IDX_DIM), lambda j: (sel(j) // n_blocks, 0, 0, 0)),
            pl.BlockSpec((1, WI_ROWS, tq),
                         lambda j: (sel(j) // n_blocks, 0, first_block + sel(j) % n_blocks)),
            pl.BlockSpec(memory_space=pl.ANY),
        ],
        out_specs=pl.BlockSpec((1, tq, ATT_WIDTH),
                               lambda j: (att(j) // n_blocks, first_block + att(j) % n_blocks, 0)),
        out_shape=jax.ShapeDtypeStruct(out.shape, out.dtype),
        input_output_aliases={6: 0},
        scratch_shapes=[pltpu.VMEM((seq, tq), F32), pltpu.VMEM((seq, tq), BF16),
                        pltpu.VMEM((2, seq, tq), BF16)],
        compiler_params=pltpu.CompilerParams(
            dimension_semantics=("arbitrary",),
            vmem_limit_bytes=VMEM_LIMIT_BYTES),
        name=f"attn_k{seq}",
    )(qT, kp, vT, qiT, ki, wiT, out)


def _attn(qT, kp, vT, qiT, ki, wiT, *, top_k, tq=512, blocks_per_class=1):
    batch, width, seq = qT.shape
    out = jnp.zeros((batch, seq, width), BF16)
    for fb in range(0, seq // tq, blocks_per_class):
        out = _attn_class(qT, kp, vT, qiT, ki, wiT, out, top_k=top_k, tq=tq, first_block=fb,
                          n_blocks=blocks_per_class)
    return out


def _pool_kernel(u_ref, pw_ref, ps_ref, o_ref):
    seq = u_ref.shape[1]
    t = lax.broadcasted_iota(jnp.int32, (seq, POOL_CH), 0)
    for g, w in enumerate(POOL_WINDOWS):
        ug = u_ref[0, :, g * POOL_CH:(g + 1) * POOL_CH]
        s, sh = ug, 1
        while sh < w:
            s = s + jnp.where(t >= sh, pltpu.roll(s, sh, axis=0), 0.0)
            sh *= 2
        pooled = s / jnp.minimum(t + 1, w).astype(F32) - ug
        mixed = jnp.dot(pooled.astype(BF16), pw_ref[g], preferred_element_type=F32)
        o_ref[0, :, g * POOL_CH:(g + 1) * POOL_CH] = (
            mixed * ps_ref[:, g * POOL_CH:(g + 1) * POOL_CH]).astype(BF16)


def _pool(u, pw, ps):
    batch, seq, width = u.shape
    return pl.pallas_call(
        _pool_kernel,
        grid=(batch,),
        in_specs=[
            pl.BlockSpec((1, seq, width), lambda b: (b, 0, 0)),
            pl.BlockSpec(pw.shape, lambda b: (0, 0, 0)),
            pl.BlockSpec((1, width), lambda b: (0, 0)),
        ],
        out_specs=pl.BlockSpec((1, seq, width), lambda b: (b, 0, 0)),
        out_shape=jax.ShapeDtypeStruct((batch, seq, width), BF16),
        compiler_params=pltpu.CompilerParams(
            dimension_semantics=("parallel",), vmem_limit_bytes=VMEM_LIMIT_BYTES),
        name="pool",
    )(u, pw, ps)


def _rope_tables(seq, rot_dim):
    half = rot_dim // 2
    inv = ROPE_THETA ** (-jnp.arange(half, dtype=F32) / half)
    ang = inv[:, None] * jnp.arange(seq, dtype=F32)[None, :]
    return jnp.cos(ang), jnp.sin(ang)


def kernel(x, ffn1_norm, ffn1_w_gate, ffn1_w_up, ffn1_w_down, mix_norm, w_in, q_norm, k_norm,
           pool_w, pool_scale, w_out, ffn2_norm, ffn2_w_gate, ffn2_w_up, ffn2_w_down):
    batch, seq, d = x.shape
    depth = w_in.shape[0]
    top_k = min(INDEX_TOPK, seq // 4)
    idx_scale = (IDX_HEADS ** -0.5) * (IDX_DIM ** -0.5)
    cos, sin = _rope_tables(seq, HEAD_DIM)
    cosi, sini = _rope_tables(seq, IDX_ROPE_DIM)
    n_feat = 3 * ATT_WIDTH + IDX_HEADS * IDX_DIM + IDX_DIM
    xf = x.reshape(batch * seq, d)

    for l in range(depth):
        xf = _ffn(xf, ffn1_norm[l][None], ffn1_w_gate[l].astype(BF16), ffn1_w_up[l].astype(BF16),
                  ffn1_w_down[l].astype(BF16))

        w = w_in[l]
        wt, wu = _prep_w_in(w, n_feat + IDX_HEADS)
        qT, kp, vT, qiT, ki, wiT, u = _proj(
            xf, mix_norm[l][None], wt, wu, q_norm[l][:, None], k_norm[l][:, None],
            cos, sin, cosi, sini, batch=batch, seq=seq, idx_scale=idx_scale)

        attn = _attn(qT, kp, vT, qiT, ki, wiT, top_k=top_k)
        pooled = _pool(u.reshape(batch, seq, POOL_WIDTH), pool_w[l].astype(BF16), pool_scale[l][None])

        mixer = (attn.reshape(batch * seq, ATT_WIDTH), pooled.reshape(batch * seq, POOL_WIDTH),
                 w_out[l].astype(BF16))
        xf = _ffn(xf, ffn2_norm[l][None], ffn2_w_gate[l].astype(BF16), ffn2_w_up[l].astype(BF16),
                  ffn2_w_down[l].astype(BF16), mixer)
    return xf.reshape(batch, seq, d)
```

```python
import functools
import math

import jax
import jax.numpy as jnp
import numpy as np
from jax import lax
from jax.experimental import pallas as pl
from jax.experimental.pallas import tpu as pltpu

HEAD_DIM = 64
N_HEADS = 8
ATT_WIDTH = N_HEADS * HEAD_DIM
IDX_HEADS = 4
IDX_DIM = 64
IDX_ROPE_DIM = 32
INDEX_TOPK = 256
POOL_WINDOWS = (2, 4, 8, 16)
POOL_CH = 128
POOL_WIDTH = POOL_CH * len(POOL_WINDOWS)
ROPE_THETA = 10000.0
NORM_EPS = 1e-6
WI_ROWS = 8
V_ROWS = HEAD_DIM + 16
PROJ_GROUP = 512
FEATURE_ROWS = 3 * ATT_WIDTH + IDX_HEADS * IDX_DIM + IDX_DIM + WI_ROWS

VMEM_LIMIT_BYTES = 56 * 1024 * 1024

F32 = jnp.float32
BF16 = jnp.bfloat16
NT_DIMS = (((1,), (1,)), ((), ()))


def _rms(x, g):
    ms = jnp.mean(x * x, axis=-1, keepdims=True)
    return x * lax.rsqrt(ms + NORM_EPS) * g


def _ffn_kernel(x_ref, g_ref, wg_ref, wu_ref, wd_ref, *rest, tf):
    *mixer, o_ref, act_ref = rest
    x = x_ref[...]
    if mixer:
        attn_ref, pooled_ref, wo_ref = mixer
        mixed = jnp.concatenate([attn_ref[...], pooled_ref[...]], axis=1)
        x = x + jnp.dot(mixed, wo_ref[...], preferred_element_type=F32)
    h = _rms(x, g_ref[...]).astype(BF16)
    for c in range(wg_ref.shape[1] // tf):
        cols = slice(c * tf, (c + 1) * tf)
        gate = jnp.dot(h, wg_ref[:, cols], preferred_element_type=F32)
        up = jnp.dot(h, wu_ref[:, cols], preferred_element_type=F32)
        act_ref[:, cols] = (gate * jax.nn.sigmoid(gate) * up).astype(BF16)
    o_ref[...] = x + 0.5 * jnp.dot(act_ref[...], wd_ref[...], preferred_element_type=F32)


def _resident(shape):
    return pl.BlockSpec(shape, lambda *_: (0,) * len(shape), pipeline_mode=pl.Buffered(1))


def _ffn(x, g, wg, wu, wd, mixer=(), *, tm=512, tf=256):
    n, d = x.shape
    f = wg.shape[1]
    rows = lambda i: (i, 0)
    mixer_specs = []
    if mixer:
        attn, pooled, wo = mixer
        mixer_specs = [pl.BlockSpec((tm, attn.shape[1]), rows), pl.BlockSpec((tm, pooled.shape[1]), rows),
                       _resident(wo.shape)]
    return pl.pallas_call(
        functools.partial(_ffn_kernel, tf=tf),
        grid=(n // tm,),
        in_specs=[
            pl.BlockSpec((tm, d), rows),
            _resident((1, d)),
            _resident((d, f)),
            _resident((d, f)),
            _resident((f, d)),
            *mixer_specs,
        ],
        out_specs=pl.BlockSpec((tm, d), rows),
        out_shape=jax.ShapeDtypeStruct((n, d), F32),
        scratch_shapes=[pltpu.VMEM((tm, f), BF16)],
        compiler_params=pltpu.CompilerParams(
            dimension_semantics=("parallel",),
            vmem_limit_bytes=VMEM_LIMIT_BYTES),
        name="ffn_mix" if mixer else "ffn",
    )(x, g, wg, wu, wd, *mixer)


LANES = 128


def _prep_w_in_kernel(wT_ref, wt_ref, wu_ref, *, n_feat):
    wt_ref[...] = wT_ref[:wt_ref.shape[0], :].astype(BF16)
    wu_ref[...] = wT_ref[n_feat:, :].T.astype(BF16)


def _prep_w_in(wT, n_feat):
    cols, d = wT.shape
    rows = -(-n_feat // LANES) * LANES
    assert rows <= cols
    return pl.pallas_call(
        functools.partial(_prep_w_in_kernel, n_feat=n_feat),
        out_shape=(jax.ShapeDtypeStruct((rows, d), BF16), jax.ShapeDtypeStruct((d, cols - n_feat), BF16)),
        compiler_params=pltpu.CompilerParams(vmem_limit_bytes=VMEM_LIMIT_BYTES),
        name="prep_w_in",
    )(wT)


def _rope_rows(x, cos, sin, half):
    x1, x2 = x[:half], x[half:2 * half]
    return x1 * cos - x2 * sin, x2 * cos + x1 * sin


def _proj_kernel(x_ref, g_ref, wt_ref, wu_ref, gq_ref, gk_ref, cos_ref, sin_ref,
                 cosi_ref, sini_ref,
                 qT_ref, kp_ref, vT_ref, qiT_ref, ki_ref, wiT_ref, u_ref, *, idx_scale, q_scale):
    h = _rms(x_ref[...], g_ref[...]).astype(BF16)
    u_ref[...] = jnp.dot(h, wu_ref[...], preferred_element_type=F32)

    class _FeatureMajorProjection:
        shape = (FEATURE_ROWS, h.shape[0])

        def __init__(self):
            self._groups = {}

        def __getitem__(self, rows):
            g = rows.start // PROJ_GROUP
            assert (rows.stop - 1) // PROJ_GROUP == g
            if g not in self._groups:
                g0 = g * PROJ_GROUP
                g1 = min(g0 + PROJ_GROUP, self.shape[0])
                self._groups[g] = lax.dot_general(wt_ref[g0:g1, :], h, NT_DIMS,
                                                  preferred_element_type=F32)
            return self._groups[g][rows.start - g * PROJ_GROUP:rows.stop - g * PROJ_GROUP]

    pT = _FeatureMajorProjection()

    cos, sin = cos_ref[...], sin_ref[...]
    gq, gk = gq_ref[...], gk_ref[...]
    half = HEAD_DIM // 2

    def head_norm_rope(rows, gain):
        ms = jnp.mean(rows * rows, axis=0, keepdims=True)
        return _rope_rows(rows * lax.rsqrt(ms + NORM_EPS) * gain, cos, sin, half)

    for hd in range(N_HEADS):
        r0 = hd * HEAD_DIM
        a, b = head_norm_rope(pT[r0:r0 + HEAD_DIM], gq)
        qT_ref[0, r0:r0 + half, :] = (a * q_scale).astype(BF16)
        qT_ref[0, r0 + half:r0 + HEAD_DIM, :] = (b * q_scale).astype(BF16)

    k0 = ATT_WIDTH
    for pair in range(N_HEADS // 2):
        parts = []
        for e in range(2):
            r0 = k0 + (2 * pair + e) * HEAD_DIM
            parts.extend(head_norm_rope(pT[r0:r0 + HEAD_DIM], gk))
        kp_ref[0, pair] = jnp.concatenate(parts, axis=0).T.astype(BF16)

    v0 = 2 * ATT_WIDTH
    tt = pT.shape[1]
    ones_rows = jnp.where(lax.broadcasted_iota(jnp.int32, (V_ROWS - HEAD_DIM, tt), 0) == 0,
                          1.0, 0.0).astype(BF16)
    for hd in range(N_HEADS):
        r0 = v0 + hd * HEAD_DIM
        vT_ref[0, hd, :HEAD_DIM, :] = pT[r0:r0 + HEAD_DIM].astype(BF16)
        vT_ref[0, hd, HEAD_DIM:, :] = ones_rows

    cosi, sini = cosi_ref[...], sini_ref[...]
    ihalf = IDX_ROPE_DIM // 2
    qi0 = 3 * ATT_WIDTH
    for hi in range(IDX_HEADS):
        r0 = qi0 + hi * IDX_DIM
        a, b = _rope_rows(pT[r0:r0 + IDX_DIM], cosi, sini, ihalf)
        qiT_ref[0, r0 - qi0:r0 - qi0 + ihalf, :] = a.astype(BF16)
        qiT_ref[0, r0 - qi0 + ihalf:r0 - qi0 + IDX_ROPE_DIM, :] = b.astype(BF16)
        qiT_ref[0, r0 - qi0 + IDX_ROPE_DIM:r0 - qi0 + IDX_DIM, :] = (
            pT[r0 + IDX_ROPE_DIM:r0 + IDX_DIM].astype(BF16))

    ki0 = qi0 + IDX_HEADS * IDX_DIM
    a, b = _rope_rows(pT[ki0:ki0 + IDX_DIM], cosi, sini, ihalf)
    ki = jnp.concatenate([a, b, pT[ki0 + IDX_ROPE_DIM:ki0 + IDX_DIM]], axis=0)
    zeros = jnp.zeros_like(ki)
    ki_ref[0, 0] = jnp.concatenate([ki, zeros], axis=0).T.astype(BF16)
    ki_ref[0, 1] = jnp.concatenate([zeros, ki], axis=0).T.astype(BF16)

    wi0 = ki0 + IDX_DIM
    wiT_ref[0] = pT[wi0:wi0 + WI_ROWS] * idx_scale


def _proj(x1, g, wt, wu, gq, gk, cos, sin, cosi, sini, *, batch, seq, idx_scale, tt=512):
    n, d = x1.shape
    rows = wt.shape[0]
    nt = seq // tt
    tok = lambda b, t: (b * nt + t, 0)
    const = lambda b, t: (0, 0)
    featmaj = lambda b, t: (b, 0, t)
    out_shape = (
        jax.ShapeDtypeStruct((batch, ATT_WIDTH, seq), BF16),
        jax.ShapeDtypeStruct((batch, N_HEADS // 2, seq, 2 * HEAD_DIM), BF16),
        jax.ShapeDtypeStruct((batch, N_HEADS, V_ROWS, seq), BF16),
        jax.ShapeDtypeStruct((batch, IDX_HEADS * IDX_DIM, seq), BF16),
        jax.ShapeDtypeStruct((batch, 2, seq, 2 * IDX_DIM), BF16),
        jax.ShapeDtypeStruct((batch, WI_ROWS, seq), F32),
        jax.ShapeDtypeStruct((n, POOL_WIDTH), F32),
    )
    out_specs = (
        pl.BlockSpec((1, ATT_WIDTH, tt), featmaj),
        pl.BlockSpec((1, N_HEADS // 2, tt, 2 * HEAD_DIM), lambda b, t: (b, 0, t, 0)),
        pl.BlockSpec((1, N_HEADS, V_ROWS, tt), lambda b, t: (b, 0, 0, t)),
        pl.BlockSpec((1, IDX_HEADS * IDX_DIM, tt), featmaj),
        pl.BlockSpec((1, 2, tt, 2 * IDX_DIM), lambda b, t: (b, 0, t, 0)),
        pl.BlockSpec((1, WI_ROWS, tt), featmaj),
        pl.BlockSpec((tt, POOL_WIDTH), tok),
    )
    in_specs = [
        pl.BlockSpec((tt, d), tok),
        pl.BlockSpec((1, d), const),
        pl.BlockSpec((rows, d), const),
        pl.BlockSpec((d, POOL_WIDTH), const),
        pl.BlockSpec((HEAD_DIM, 1), const),
        pl.BlockSpec((HEAD_DIM, 1), const),
        pl.BlockSpec((HEAD_DIM // 2, tt), lambda b, t: (0, t)),
        pl.BlockSpec((HEAD_DIM // 2, tt), lambda b, t: (0, t)),
        pl.BlockSpec((IDX_ROPE_DIM // 2, tt), lambda b, t: (0, t)),
        pl.BlockSpec((IDX_ROPE_DIM // 2, tt), lambda b, t: (0, t)),
    ]
    return pl.pallas_call(
        functools.partial(_proj_kernel, idx_scale=idx_scale,
                          q_scale=HEAD_DIM ** -0.5 * math.log2(math.e)),
        grid=(batch, nt),
        in_specs=in_specs,
        out_specs=out_specs,
        out_shape=out_shape,
        compiler_params=pltpu.CompilerParams(
            dimension_semantics=("parallel", "parallel"),
            vmem_limit_bytes=VMEM_LIMIT_BYTES),
        name="proj",
    )(x1, g, wt, wu, gq, gk, cos, sin, cosi, sini)


INT_MIN = np.int32(-2 ** 31)
COUNT_ROWS = 64
COUNT_ROWS_F32 = 32
TIE_CHUNK = 256
COARSE_BITS = 16
FINE_BITS = 17


def _count(pred):
    ones = jnp.where(pred, jnp.bfloat16(1), jnp.bfloat16(0))
    rows, cols = ones.shape
    groups = ones.reshape(rows // COUNT_ROWS, COUNT_ROWS, cols)
    part = groups[0]
    for i in range(1, rows // COUNT_ROWS):
        part = part + groups[i]
    return part.astype(F32).sum(axis=0, keepdims=True).astype(jnp.int32)


def _count_f32(pred):
    ones = jnp.where(pred, 1.0, 0.0)
    rows, cols = ones.shape
    part = ones.reshape(rows // COUNT_ROWS_F32, COUNT_ROWS_F32, cols).sum(axis=0)
    return part.sum(axis=0, keepdims=True).astype(jnp.int32)


def _value_of_key(key):
    return pltpu.bitcast(jnp.where(key < 0, INT_MIN - key, key), F32)


def _topk_bias(score, causal_bias, score_ref, coarse_ref, bias_out_ref, top_k):
    seq, tq = score.shape
    score_ref[...] = score
    coarse_ref[...] = score.astype(BF16)

    shift, mid = 32 - COARSE_BITS, 1 << (COARSE_BITS - 1)
    found = jnp.zeros((1, tq), jnp.int32)
    for bit in reversed(range(COARSE_BITS)):
        cand = found | (1 << bit)
        cand_value = _value_of_key(lax.shift_left(cand - mid, shift)).astype(BF16)
        found = jnp.where(_count(coarse_ref[...] >= cand_value) >= top_k, cand, found)
    base = lax.shift_left(found - mid, shift) - (1 << (shift - 1))
    offset = jnp.zeros((1, tq), jnp.int32)
    for bit in reversed(range(FINE_BITS)):
        cand = offset | (1 << bit)
        offset = jnp.where(_count_f32(score_ref[...] >= _value_of_key(base + cand)) >= top_k, cand, offset)
    thr = _value_of_key(base + offset)
    need = top_k - _count_f32(score_ref[...] > thr)

    c = TIE_CHUNK
    lower_tri = jnp.where(lax.broadcasted_iota(jnp.int32, (c, c), 0)
                          >= lax.broadcasted_iota(jnp.int32, (c, c), 1), 1.0, 0.0).astype(BF16)
    remaining = need.astype(F32)
    for r0 in range(0, seq, c):
        x = score_ref[r0:r0 + c]
        passes = causal_bias(r0, c)
        ties = x == thr
        rank = jnp.dot(lower_tri, jnp.where(ties, 1.0, 0.0).astype(BF16), preferred_element_type=F32)
        bias_out_ref[r0:r0 + c] = jnp.where(
            x > thr, passes,
            jnp.where(ties, jnp.where(rank <= remaining, passes, -jnp.inf), -jnp.inf)).astype(BF16)
        remaining = remaining - rank[c - 1:c, :]


def _attend_heads(qT_ref, kp_ref, vT_ref, bias_ref, o_ref):
    tq = qT_ref.shape[2]
    half_rows = lax.broadcasted_iota(jnp.int32, (2 * HEAD_DIM, tq), 0) < HEAD_DIM
    outs = []
    for hd in range(N_HEADS):
        pair, e = divmod(hd, 2)
        qpair = qT_ref[0, 2 * HEAD_DIM * pair:2 * HEAD_DIM * (pair + 1), :]
        qz = jnp.where(half_rows if e == 0 else jnp.logical_not(half_rows), qpair,
                       jnp.zeros_like(qpair))
        logits = jnp.dot(kp_ref[0, pair], qz, preferred_element_type=F32).astype(BF16) + bias_ref[...]
        p = jnp.exp2(logits - jnp.max(logits, axis=0, keepdims=True))
        o_aug = jnp.dot(vT_ref[0, hd], p, preferred_element_type=F32)
        outs.append(o_aug[:HEAD_DIM] / o_aug[HEAD_DIM:HEAD_DIM + 1])
    o_ref[0] = jnp.concatenate(outs, axis=0).T.astype(BF16)


def _attn_kernel(qT_ref, kp_ref, vT_ref, qiT_ref, ki_ref, wiT_ref, _aliased_out_ref, o_ref,
                 score_ref, coarse_ref, bias_ref, *, top_k, first_block, n_blocks, n_steps):
    j = pl.program_id(0)
    seq = kp_ref.shape[2]
    tq = qT_ref.shape[2]
    sel_slot = j % 2
    att_slot = 1 - sel_slot

    @pl.when(j == 0)
    def _():
        bias_ref[1] = jnp.zeros((seq, tq), BF16)

    block = first_block + jnp.minimum(j, n_steps - 1) % n_blocks
    all_causal_rows = first_block * tq

    def causal(r0, rows):
        k_pos = r0 + lax.broadcasted_iota(jnp.int32, (rows, tq), 0)
        return k_pos <= block * tq + lax.broadcasted_iota(jnp.int32, (rows, tq), 1)

    def causal_bias(r0, rows):
        if r0 + rows <= all_causal_rows:
            return 0.0
        return jnp.where(causal(r0, rows), 0.0, -jnp.inf)

    wi = wiT_ref[0]
    score = jnp.zeros((seq, tq), F32)
    for hi in range(IDX_HEADS):
        pair, e = divmod(hi, 2)
        rel = jnp.dot(ki_ref[0, e], qiT_ref[0, 2 * IDX_DIM * pair:2 * IDX_DIM * (pair + 1), :],
                      preferred_element_type=F32)
        score = score + wi[hi:hi + 1, :] * jnp.maximum(rel, 0.0)
    tail = jnp.where(causal(all_causal_rows, seq - all_causal_rows), score[all_causal_rows:], -jnp.inf)
    score = jnp.concatenate([score[:all_causal_rows], tail], axis=0) if all_causal_rows else tail

    _attend_heads(qT_ref, kp_ref, vT_ref, bias_ref.at[att_slot], o_ref)
    _topk_bias(score, causal_bias, score_ref, coarse_ref, bias_ref.at[sel_slot], top_k)


def _attn_class(qT, kp, vT, qiT, ki, wiT, out, *, top_k, tq, first_block, n_blocks):
    batch = qT.shape[0]
    seq = (first_block + n_blocks) * tq
    n_steps = batch * n_blocks

    def sel(j):
        return jnp.minimum(j, n_steps - 1)

    def att(j):
        return jnp.maximum(j - 1, 0)

    return pl.pallas_call(
        functools.partial(_attn_kernel, top_k=top_k, first_block=first_block, n_blocks=n_blocks,
                          n_steps=n_steps),
        grid=(n_steps + 1,),
        in_specs=[
            pl.BlockSpec((1, ATT_WIDTH, tq),
                         lambda j: (att(j) // n_blocks, 0, first_block + att(j) % n_blocks)),
            pl.BlockSpec((1, N_HEADS // 2, seq, 2 * HEAD_DIM), lambda j: (att(j) // n_blocks, 0, 0, 0)),
            pl.BlockSpec((1, N_HEADS, V_ROWS, seq), lambda j: (att(j) // n_blocks, 0, 0, 0)),
            pl.BlockSpec((1, IDX_HEADS * IDX_DIM, tq),
                         lambda j: (sel(j) // n_blocks, 0, first_block + sel(j) % n_blocks)),
            pl.BlockSpec((1, 2, seq, 2 * IDX_DIM), lambda j: (sel(j) // n_blocks, 0, 0, 0)),
            pl.BlockSpec((1, WI_ROWS, tq),
                         lambda j: (sel(j) // n_blocks, 0, first_block + sel(j) % n_blocks)),
            pl.BlockSpec(memory_space=pl.ANY),
        ],
        out_specs=pl.BlockSpec((1, tq, ATT_WIDTH),
                               lambda j: (att(j) // n_blocks, first_block + att(j) % n_blocks, 0)),
        out_shape=jax.ShapeDtypeStruct(out.shape, out.dtype),
        input_output_aliases={6: 0},
        scratch_shapes=[pltpu.VMEM((seq, tq), F32), pltpu.VMEM((seq, tq), BF16),
                        pltpu.VMEM((2, seq, tq), BF16)],
        compiler_params=pltpu.CompilerParams(
            dimension_semantics=("arbitrary",),
            vmem_limit_bytes=VMEM_LIMIT_BYTES),
        name=f"attn_k{seq}",
    )(qT, kp, vT, qiT, ki, wiT, out)


def _attn(qT, kp, vT, qiT, ki, wiT, *, top_k, tq=512, blocks_per_class=1):
    batch, width, seq = qT.shape
    out = jnp.zeros((batch, seq, width), BF16)
    for fb in range(0, seq // tq, blocks_per_class):
        out = _attn_class(qT, kp, vT, qiT, ki, wiT, out, top_k=top_k, tq=tq, first_block=fb,
                          n_blocks=blocks_per_class)
    return out


def _pool_kernel(u_ref, pw_ref, ps_ref, o_ref):
    seq = u_ref.shape[1]
    t = lax.broadcasted_iota(jnp.int32, (seq, POOL_CH), 0)
    for g, w in enumerate(POOL_WINDOWS):
        ug = u_ref[0, :, g * POOL_CH:(g + 1) * POOL_CH]
        s, sh = ug, 1
        while sh < w:
            s = s + jnp.where(t >= sh, pltpu.roll(s, sh, axis=0), 0.0)
            sh *= 2
        pooled = s / jnp.minimum(t + 1, w).astype(F32) - ug
        mixed = jnp.dot(pooled.astype(BF16), pw_ref[g], preferred_element_type=F32)
        o_ref[0, :, g * POOL_CH:(g + 1) * POOL_CH] = (
            mixed * ps_ref[:, g * POOL_CH:(g + 1) * POOL_CH]).astype(BF16)


def _pool(u, pw, ps):
    batch, seq, width = u.shape
    return pl.pallas_call(
        _pool_kernel,
        grid=(batch,),
        in_specs=[
            pl.BlockSpec((1, seq, width), lambda b: (b, 0, 0)),
            pl.BlockSpec(pw.shape, lambda b: (0, 0, 0)),
            pl.BlockSpec((1, width), lambda b: (0, 0)),
        ],
        out_specs=pl.BlockSpec((1, seq, width), lambda b: (b, 0, 0)),
        out_shape=jax.ShapeDtypeStruct((batch, seq, width), BF16),
        compiler_params=pltpu.CompilerParams(
            dimension_semantics=("parallel",), vmem_limit_bytes=VMEM_LIMIT_BYTES),
        name="pool",
    )(u, pw, ps)


def _rope_tables(seq, rot_dim):
    half = rot_dim // 2
    inv = ROPE_THETA ** (-jnp.arange(half, dtype=F32) / half)
    ang = inv[:, None] * jnp.arange(seq, dtype=F32)[None, :]
    return jnp.cos(ang), jnp.sin(ang)


def kernel(x, ffn1_norm, ffn1_w_gate, ffn1_w_up, ffn1_w_down, mix_norm, w_in, q_norm, k_norm,
           pool_w, pool_scale, w_out, ffn2_norm, ffn2_w_gate, ffn2_w_up, ffn2_w_down):
    batch, seq, d = x.shape
    depth = w_in.shape[0]
    top_k = min(INDEX_TOPK, seq // 4)
    idx_scale = (IDX_HEADS ** -0.5) * (IDX_DIM ** -0.5)
    cos, sin = _rope_tables(seq, HEAD_DIM)
    cosi, sini = _rope_tables(seq, IDX_ROPE_DIM)
    n_feat = 3 * ATT_WIDTH + IDX_HEADS * IDX_DIM + IDX_DIM
    xf = x.reshape(batch * seq, d)

    for l in range(depth):
        xf = _ffn(xf, ffn1_norm[l][None], ffn1_w_gate[l].astype(BF16), ffn1_w_up[l].astype(BF16),
                  ffn1_w_down[l].astype(BF16))

        wt, wu = _prep_w_in(w_in[l].T, n_feat + IDX_HEADS)
        qT, kp, vT, qiT, ki, wiT, u = _proj(
            xf, mix_norm[l][None], wt, wu, q_norm[l][:, None], k_norm[l][:, None],
            cos, sin, cosi, sini, batch=batch, seq=seq, idx_scale=idx_scale)

        attn = _attn(qT, kp, vT, qiT, ki, wiT, top_k=top_k)
        pooled = _pool(u.reshape(batch, seq, POOL_WIDTH), pool_w[l].astype(BF16), pool_scale[l][None])

        mixer = (attn.reshape(batch * seq, ATT_WIDTH), pooled.reshape(batch * seq, POOL_WIDTH),
                 w_out[l].astype(BF16))
        xf = _ffn(xf, ffn2_norm[l][None], ffn2_w_gate[l].astype(BF16), ffn2_w_up[l].astype(BF16),
                  ffn2_w_down[l].astype(BF16), mixer)
    return xf.reshape(batch, seq, d)
```

```python
import functools
import math

import jax
import jax.numpy as jnp
import numpy as np
from jax import lax
from jax.experimental import pallas as pl
from jax.experimental.pallas import tpu as pltpu

HEAD_DIM = 64
N_HEADS = 8
ATT_WIDTH = N_HEADS * HEAD_DIM
IDX_HEADS = 4
IDX_DIM = 64
IDX_ROPE_DIM = 32
INDEX_TOPK = 256
POOL_WINDOWS = (2, 4, 8, 16)
POOL_CH = 128
POOL_WIDTH = POOL_CH * len(POOL_WINDOWS)
ROPE_THETA = 10000.0
NORM_EPS = 1e-6
WI_ROWS = 8
V_ROWS = HEAD_DIM + 16
PROJ_GROUP = 512
FEATURE_ROWS = 3 * ATT_WIDTH + IDX_HEADS * IDX_DIM + IDX_DIM + WI_ROWS

VMEM_LIMIT_BYTES = 56 * 1024 * 1024

F32 = jnp.float32
BF16 = jnp.bfloat16
NT_DIMS = (((1,), (1,)), ((), ()))


def _rms(x, g):
    ms = jnp.mean(x * x, axis=-1, keepdims=True)
    return x * lax.rsqrt(ms + NORM_EPS) * g


WEIGHT_CHUNKS = 8


def _load_weights_bf16(transfers):
    steps = []
    for src, dst, stage, sems in transfers:
        rows = stage.shape[1]
        for c in range(src.shape[0] // rows):
            slot = c % 2
            copy = pltpu.make_async_copy(src.at[pl.ds(c * rows, rows), :], stage.at[slot], sems.at[slot])
            steps.append((copy, dst, c * rows, rows, stage, slot))
    steps[0][0].start()
    for k, (copy, dst, r0, rows, stage, slot) in enumerate(steps):
        copy.wait()
        if k + 1 < len(steps):
            steps[k + 1][0].start()
        dst[r0:r0 + rows, :] = stage[slot].astype(BF16)


def _ffn_kernel(x_ref, g_ref, wg_hbm, wu_hbm, wd_hbm, *rest, tf):
    *mixer, o_ref, act_ref, wg_ref, wu_ref, wd_ref, stage_in, stage_out, sem_in, sem_out = rest

    @pl.when(pl.program_id(0) == 0)
    def _():
        _load_weights_bf16([(wg_hbm, wg_ref, stage_in, sem_in), (wu_hbm, wu_ref, stage_in, sem_in),
                            (wd_hbm, wd_ref, stage_out, sem_out)])

    x = x_ref[...]
    if mixer:
        attn_ref, pooled_ref, wo_ref = mixer
        mixed = jnp.concatenate([attn_ref[...], pooled_ref[...]], axis=1)
        x = x + jnp.dot(mixed, wo_ref[...], preferred_element_type=F32)
    h = _rms(x, g_ref[...]).astype(BF16)
    for c in range(wg_ref.shape[1] // tf):
        cols = slice(c * tf, (c + 1) * tf)
        gate = jnp.dot(h, wg_ref[:, cols], preferred_element_type=F32)
        up = jnp.dot(h, wu_ref[:, cols], preferred_element_type=F32)
        act_ref[:, cols] = (gate * jax.nn.sigmoid(gate) * up).astype(BF16)
    o_ref[...] = x + 0.5 * jnp.dot(act_ref[...], wd_ref[...], preferred_element_type=F32)


def _resident(shape):
    return pl.BlockSpec(shape, lambda *_: (0,) * len(shape), pipeline_mode=pl.Buffered(1))


def _ffn(x, g, wg, wu, wd, mixer=(), *, tm=512, tf=256):
    n, d = x.shape
    f = wg.shape[1]
    rows = lambda i: (i, 0)
    mixer_specs = []
    if mixer:
        attn, pooled, wo = mixer
        mixer_specs = [pl.BlockSpec((tm, attn.shape[1]), rows), pl.BlockSpec((tm, pooled.shape[1]), rows),
                       _resident(wo.shape)]
    return pl.pallas_call(
        functools.partial(_ffn_kernel, tf=tf),
        grid=(n // tm,),
        in_specs=[
            pl.BlockSpec((tm, d), rows),
            _resident((1, d)),
            pl.BlockSpec(memory_space=pl.ANY),
            pl.BlockSpec(memory_space=pl.ANY),
            pl.BlockSpec(memory_space=pl.ANY),
            *mixer_specs,
        ],
        out_specs=pl.BlockSpec((tm, d), rows),
        out_shape=jax.ShapeDtypeStruct((n, d), F32),
        scratch_shapes=[
            pltpu.VMEM((tm, f), BF16),
            pltpu.VMEM((d, f), BF16), pltpu.VMEM((d, f), BF16), pltpu.VMEM((f, d), BF16),
            pltpu.VMEM((2, d // WEIGHT_CHUNKS, f), F32), pltpu.VMEM((2, f // WEIGHT_CHUNKS, d), F32),
            pltpu.SemaphoreType.DMA((2,)), pltpu.SemaphoreType.DMA((2,)),
        ],
        compiler_params=pltpu.CompilerParams(
            dimension_semantics=("arbitrary",),
            vmem_limit_bytes=VMEM_LIMIT_BYTES),
        name="ffn_mix" if mixer else "ffn",
    )(x, g, wg, wu, wd, *mixer)


LANES = 128


def _prep_w_in_kernel(wT_ref, wt_ref, wu_ref, *, n_feat):
    wt_ref[...] = wT_ref[:wt_ref.shape[0], :].astype(BF16)
    wu_ref[...] = wT_ref[n_feat:, :].T.astype(BF16)


def _prep_w_in(wT, n_feat):
    cols, d = wT.shape
    rows = -(-n_feat // LANES) * LANES
    assert rows <= cols
    return pl.pallas_call(
        functools.partial(_prep_w_in_kernel, n_feat=n_feat),
        out_shape=(jax.ShapeDtypeStruct((rows, d), BF16), jax.ShapeDtypeStruct((d, cols - n_feat), BF16)),
        compiler_params=pltpu.CompilerParams(vmem_limit_bytes=VMEM_LIMIT_BYTES),
        name="prep_w_in",
    )(wT)


def _rope_rows(x, cos, sin, half):
    x1, x2 = x[:half], x[half:2 * half]
    return x1 * cos - x2 * sin, x2 * cos + x1 * sin


def _proj_kernel(x_ref, g_ref, wt_ref, wu_ref, gq_ref, gk_ref, cos_ref, sin_ref,
                 cosi_ref, sini_ref,
                 qT_ref, kp_ref, vT_ref, qiT_ref, ki_ref, wiT_ref, u_ref, *, idx_scale, q_scale):
    h = _rms(x_ref[...], g_ref[...]).astype(BF16)
    u_ref[...] = jnp.dot(h, wu_ref[...], preferred_element_type=F32)

    class _FeatureMajorProjection:
        shape = (FEATURE_ROWS, h.shape[0])

        def __init__(self):
            self._groups = {}

        def __getitem__(self, rows):
            g = rows.start // PROJ_GROUP
            assert (rows.stop - 1) // PROJ_GROUP == g
            if g not in self._groups:
                g0 = g * PROJ_GROUP
                g1 = min(g0 + PROJ_GROUP, self.shape[0])
                self._groups[g] = lax.dot_general(wt_ref[g0:g1, :], h, NT_DIMS,
                                                  preferred_element_type=F32)
            return self._groups[g][rows.start - g * PROJ_GROUP:rows.stop - g * PROJ_GROUP]

    pT = _FeatureMajorProjection()

    cos, sin = cos_ref[...], sin_ref[...]
    gq, gk = gq_ref[...], gk_ref[...]
    half = HEAD_DIM // 2

    def head_norm_rope(rows, gain):
        ms = jnp.mean(rows * rows, axis=0, keepdims=True)
        return _rope_rows(rows * lax.rsqrt(ms + NORM_EPS) * gain, cos, sin, half)

    for hd in range(N_HEADS):
        r0 = hd * HEAD_DIM
        a, b = head_norm_rope(pT[r0:r0 + HEAD_DIM], gq)
        qT_ref[0, r0:r0 + half, :] = (a * q_scale).astype(BF16)
        qT_ref[0, r0 + half:r0 + HEAD_DIM, :] = (b * q_scale).astype(BF16)

    k0 = ATT_WIDTH
    for pair in range(N_HEADS // 2):
        parts = []
        for e in range(2):
            r0 = k0 + (2 * pair + e) * HEAD_DIM
            parts.extend(head_norm_rope(pT[r0:r0 + HEAD_DIM], gk))
        kp_ref[0, pair] = jnp.concatenate(parts, axis=0).T.astype(BF16)

    v0 = 2 * ATT_WIDTH
    tt = pT.shape[1]
    ones_rows = jnp.where(lax.broadcasted_iota(jnp.int32, (V_ROWS - HEAD_DIM, tt), 0) == 0,
                          1.0, 0.0).astype(BF16)
    for hd in range(N_HEADS):
        r0 = v0 + hd * HEAD_DIM
        vT_ref[0, hd, :HEAD_DIM, :] = pT[r0:r0 + HEAD_DIM].astype(BF16)
        vT_ref[0, hd, HEAD_DIM:, :] = ones_rows

    cosi, sini = cosi_ref[...], sini_ref[...]
    ihalf = IDX_ROPE_DIM // 2
    qi0 = 3 * ATT_WIDTH
    for hi in range(IDX_HEADS):
        r0 = qi0 + hi * IDX_DIM
        a, b = _rope_rows(pT[r0:r0 + IDX_DIM], cosi, sini, ihalf)
        qiT_ref[0, r0 - qi0:r0 - qi0 + ihalf, :] = a.astype(BF16)
        qiT_ref[0, r0 - qi0 + ihalf:r0 - qi0 + IDX_ROPE_DIM, :] = b.astype(BF16)
        qiT_ref[0, r0 - qi0 + IDX_ROPE_DIM:r0 - qi0 + IDX_DIM, :] = (
            pT[r0 + IDX_ROPE_DIM:r0 + IDX_DIM].astype(BF16))

    ki0 = qi0 + IDX_HEADS * IDX_DIM
    a, b = _rope_rows(pT[ki0:ki0 + IDX_DIM], cosi, sini, ihalf)
    ki = jnp.concatenate([a, b, pT[ki0 + IDX_ROPE_DIM:ki0 + IDX_DIM]], axis=0)
    zeros = jnp.zeros_like(ki)
    ki_ref[0, 0] = jnp.concatenate([ki, zeros], axis=0).T.astype(BF16)
    ki_ref[0, 1] = jnp.concatenate([zeros, ki], axis=0).T.astype(BF16)

    wi0 = ki0 + IDX_DIM
    wiT_ref[0] = pT[wi0:wi0 + WI_ROWS] * idx_scale


def _proj(x1, g, wt, wu, gq, gk, cos, sin, cosi, sini, *, batch, seq, idx_scale, tt=512):
    n, d = x1.shape
    rows = wt.shape[0]
    nt = seq // tt
    tok = lambda b, t: (b * nt + t, 0)
    const = lambda b, t: (0, 0)
    featmaj = lambda b, t: (b, 0, t)
    out_shape = (
        jax.ShapeDtypeStruct((batch, ATT_WIDTH, seq), BF16),
        jax.ShapeDtypeStruct((batch, N_HEADS // 2, seq, 2 * HEAD_DIM), BF16),
        jax.ShapeDtypeStruct((batch, N_HEADS, V_ROWS, seq), BF16),
        jax.ShapeDtypeStruct((batch, IDX_HEADS * IDX_DIM, seq), BF16),
        jax.ShapeDtypeStruct((batch, 2, seq, 2 * IDX_DIM), BF16),
        jax.ShapeDtypeStruct((batch, WI_ROWS, seq), F32),
        jax.ShapeDtypeStruct((n, POOL_WIDTH), F32),
    )
    out_specs = (
        pl.BlockSpec((1, ATT_WIDTH, tt), featmaj),
        pl.BlockSpec((1, N_HEADS // 2, tt, 2 * HEAD_DIM), lambda b, t: (b, 0, t, 0)),
        pl.BlockSpec((1, N_HEADS, V_ROWS, tt), lambda b, t: (b, 0, 0, t)),
        pl.BlockSpec((1, IDX_HEADS * IDX_DIM, tt), featmaj),
        pl.BlockSpec((1, 2, tt, 2 * IDX_DIM), lambda b, t: (b, 0, t, 0)),
        pl.BlockSpec((1, WI_ROWS, tt), featmaj),
        pl.BlockSpec((tt, POOL_WIDTH), tok),
    )
    in_specs = [
        pl.BlockSpec((tt, d), tok),
        pl.BlockSpec((1, d), const),
        pl.BlockSpec((rows, d), const),
        pl.BlockSpec((d, POOL_WIDTH), const),
        pl.BlockSpec((HEAD_DIM, 1), const),
        pl.BlockSpec((HEAD_DIM, 1), const),
        pl.BlockSpec((HEAD_DIM // 2, tt), lambda b, t: (0, t)),
        pl.BlockSpec((HEAD_DIM // 2, tt), lambda b, t: (0, t)),
        pl.BlockSpec((IDX_ROPE_DIM // 2, tt), lambda b, t: (0, t)),
        pl.BlockSpec((IDX_ROPE_DIM // 2, tt), lambda b, t: (0, t)),
    ]
    return pl.pallas_call(
        functools.partial(_proj_kernel, idx_scale=idx_scale,
                          q_scale=HEAD_DIM ** -0.5 * math.log2(math.e)),
        grid=(batch, nt),
        in_specs=in_specs,
        out_specs=out_specs,
        out_shape=out_shape,
        compiler_params=pltpu.CompilerParams(
            dimension_semantics=("parallel", "parallel"),
            vmem_limit_bytes=VMEM_LIMIT_BYTES),
        name="proj",
    )(x1, g, wt, wu, gq, gk, cos, sin, cosi, sini)


INT_MIN = np.int32(-2 ** 31)
COUNT_ROWS = 64
COUNT_ROWS_F32 = 32
TIE_CHUNK = 256
COARSE_BITS = 16
FINE_BITS = 17


def _count(pred):
    ones = jnp.where(pred, jnp.bfloat16(1), jnp.bfloat16(0))
    rows, cols = ones.shape
    groups = ones.reshape(rows // COUNT_ROWS, COUNT_ROWS, cols)
    part = groups[0]
    for i in range(1, rows // COUNT_ROWS):
        part = part + groups[i]
    return part.astype(F32).sum(axis=0, keepdims=True).astype(jnp.int32)


def _count_f32(pred):
    ones = jnp.where(pred, 1.0, 0.0)
    rows, cols = ones.shape
    part = ones.reshape(rows // COUNT_ROWS_F32, COUNT_ROWS_F32, cols).sum(axis=0)
    return part.sum(axis=0, keepdims=True).astype(jnp.int32)


def _value_of_key(key):
    return pltpu.bitcast(jnp.where(key < 0, INT_MIN - key, key), F32)


def _topk_bias(score, causal_bias, score_ref, coarse_ref, bias_out_ref, top_k):
    seq, tq = score.shape
    score_ref[...] = score
    coarse_ref[...] = score.astype(BF16)

    shift, mid = 32 - COARSE_BITS, 1 << (COARSE_BITS - 1)
    found = jnp.zeros((1, tq), jnp.int32)
    for bit in reversed(range(COARSE_BITS)):
        cand = found | (1 << bit)
        cand_value = _value_of_key(lax.shift_left(cand - mid, shift)).astype(BF16)
        found = jnp.where(_count(coarse_ref[...] >= cand_value) >= top_k, cand, found)
    base = lax.shift_left(found - mid, shift) - (1 << (shift - 1))
    offset = jnp.zeros((1, tq), jnp.int32)
    for bit in reversed(range(FINE_BITS)):
        cand = offset | (1 << bit)
        offset = jnp.where(_count_f32(score_ref[...] >= _value_of_key(base + cand)) >= top_k, cand, offset)
    thr = _value_of_key(base + offset)
    need = top_k - _count_f32(score_ref[...] > thr)

    c = TIE_CHUNK
    lower_tri = jnp.where(lax.broadcasted_iota(jnp.int32, (c, c), 0)
                          >= lax.broadcasted_iota(jnp.int32, (c, c), 1), 1.0, 0.0).astype(BF16)
    remaining = need.astype(F32)
    for r0 in range(0, seq, c):
        x = score_ref[r0:r0 + c]
        passes = causal_bias(r0, c)
        ties = x == thr
        rank = jnp.dot(lower_tri, jnp.where(ties, 1.0, 0.0).astype(BF16), preferred_element_type=F32)
        bias_out_ref[r0:r0 + c] = jnp.where(
            x > thr, passes,
            jnp.where(ties, jnp.where(rank <= remaining, passes, -jnp.inf), -jnp.inf)).astype(BF16)
        remaining = remaining - rank[c - 1:c, :]


def _attend_heads(qT_ref, kp_ref, vT_ref, bias_ref, o_ref):
    tq = qT_ref.shape[2]
    half_rows = lax.broadcasted_iota(jnp.int32, (2 * HEAD_DIM, tq), 0) < HEAD_DIM
    outs = []
    for hd in range(N_HEADS):
        pair, e = divmod(hd, 2)
        qpair = qT_ref[0, 2 * HEAD_DIM * pair:2 * HEAD_DIM * (pair + 1), :]
        qz = jnp.where(half_rows if e == 0 else jnp.logical_not(half_rows), qpair,
                       jnp.zeros_like(qpair))
        logits = jnp.dot(kp_ref[0, pair], qz, preferred_element_type=F32).astype(BF16) + bias_ref[...]
        p = jnp.exp2(logits - jnp.max(logits, axis=0, keepdims=True))
        o_aug = jnp.dot(vT_ref[0, hd], p, preferred_element_type=F32)
        outs.append(o_aug[:HEAD_DIM] / o_aug[HEAD_DIM:HEAD_DIM + 1])
    o_ref[0] = jnp.concatenate(outs, axis=0).T.astype(BF16)


def _attn_kernel(qT_ref, kp_ref, vT_ref, qiT_ref, ki_ref, wiT_ref, _aliased_out_ref, o_ref,
                 score_ref, coarse_ref, bias_ref, *, top_k, first_block, n_blocks, n_steps):
    j = pl.program_id(0)
    seq = kp_ref.shape[2]
    tq = qT_ref.shape[2]
    sel_slot = j % 2
    att_slot = 1 - sel_slot

    @pl.when(j == 0)
    def _():
        bias_ref[1] = jnp.zeros((seq, tq), BF16)

    block = first_block + jnp.minimum(j, n_steps - 1) % n_blocks
    all_causal_rows = first_block * tq

    def causal(r0, rows):
        k_pos = r0 + lax.broadcasted_iota(jnp.int32, (rows, tq), 0)
        return k_pos <= block * tq + lax.broadcasted_iota(jnp.int32, (rows, tq), 1)

    def causal_bias(r0, rows):
        if r0 + rows <= all_causal_rows:
            return 0.0
        return jnp.where(causal(r0, rows), 0.0, -jnp.inf)

    wi = wiT_ref[0]
    score = jnp.zeros((seq, tq), F32)
    for hi in range(IDX_HEADS):
        pair, e = divmod(hi, 2)
        rel = jnp.dot(ki_ref[0, e], qiT_ref[0, 2 * IDX_DIM * pair:2 * IDX_DIM * (pair + 1), :],
                      preferred_element_type=F32)
        score = score + wi[hi:hi + 1, :] * jnp.maximum(rel, 0.0)
    tail = jnp.where(causal(all_causal_rows, seq - all_causal_rows), score[all_causal_rows:], -jnp.inf)
    score = jnp.concatenate([score[:all_causal_rows], tail], axis=0) if all_causal_rows else tail

    _attend_heads(qT_ref, kp_ref, vT_ref, bias_ref.at[att_slot], o_ref)
    _topk_bias(score, causal_bias, score_ref, coarse_ref, bias_ref.at[sel_slot], top_k)


def _attn_class(qT, kp, vT, qiT, ki, wiT, out, *, top_k, tq, first_block, n_blocks):
    batch = qT.shape[0]
    seq = (first_block + n_blocks) * tq
    n_steps = batch * n_blocks

    def sel(j):
        return jnp.minimum(j, n_steps - 1)

    def att(j):
        return jnp.maximum(j - 1, 0)

    return pl.pallas_call(
        functools.partial(_attn_kernel, top_k=top_k, first_block=first_block, n_blocks=n_blocks,
                          n_steps=n_steps),
        grid=(n_steps + 1,),
        in_specs=[
            pl.BlockSpec((1, ATT_WIDTH, tq),
                         lambda j: (att(j) // n_blocks, 0, first_block + att(j) % n_blocks)),
            pl.BlockSpec((1, N_HEADS // 2, seq, 2 * HEAD_DIM), lambda j: (att(j) // n_blocks, 0, 0, 0)),
            pl.BlockSpec((1, N_HEADS, V_ROWS, seq), lambda j: (att(j) // n_blocks, 0, 0, 0)),
            pl.BlockSpec((1, IDX_HEADS * IDX_DIM, tq),
                         lambda j: (sel(j) // n_blocks, 0, first_block + sel(j) % n_blocks)),
            pl.BlockSpec((1, 2, seq, 2 * IDX_DIM), lambda j: (sel(j) // n_blocks, 0, 0, 0)),
            pl.BlockSpec((1, WI_ROWS, tq),
                         lambda j: (sel(j) // n_blocks, 0, first_block + sel(j) % n_blocks)),
            pl.BlockSpec(memory_space=pl.ANY),
        ],
        out_specs=pl.BlockSpec((1, tq, ATT_WIDTH),
                               lambda j: (att(j) // n_blocks, first_block + att(j) % n_blocks, 0)),
        out_shape=jax.ShapeDtypeStruct(out.shape, out.dtype),
        input_output_aliases={6: 0},
        scratch_shapes=[pltpu.VMEM((seq, tq), F32), pltpu.VMEM((seq, tq), BF16),
                        pltpu.VMEM((2, seq, tq), BF16)],
        compiler_params=pltpu.CompilerParams(
            dimension_semantics=("arbitrary",),
            vmem_limit_bytes=VMEM_LIMIT_BYTES),
        name=f"attn_k{seq}",
    )(qT, kp, vT, qiT, ki, wiT, out)


def _attn(qT, kp, vT, qiT, ki, wiT, *, top_k, tq=512, blocks_per_class=1):
    batch, width, seq = qT.shape
    out = jnp.zeros((batch, seq, width), BF16)
    for fb in range(0, seq // tq, blocks_per_class):
        out = _attn_class(qT, kp, vT, qiT, ki, wiT, out, top_k=top_k, tq=tq, first_block=fb,
                          n_blocks=blocks_per_class)
    return out


def _pool_kernel(u_ref, pw_ref, ps_ref, o_ref):
    seq = u_ref.shape[1]
    t = lax.broadcasted_iota(jnp.int32, (seq, POOL_CH), 0)
    for g, w in enumerate(POOL_WINDOWS):
        ug = u_ref[0, :, g * POOL_CH:(g + 1) * POOL_CH]
        s, sh = ug, 1
        while sh < w:
            s = s + jnp.where(t >= sh, pltpu.roll(s, sh, axis=0), 0.0)
            sh *= 2
        pooled = s / jnp.minimum(t + 1, w).astype(F32) - ug
        mixed = jnp.dot(pooled.astype(BF16), pw_ref[g], preferred_element_type=F32)
        o_ref[0, :, g * POOL_CH:(g + 1) * POOL_CH] = (
            mixed * ps_ref[:, g * POOL_CH:(g + 1) * POOL_CH]).astype(BF16)


def _pool(u, pw, ps):
    batch, seq, width = u.shape
    return pl.pallas_call(
        _pool_kernel,
        grid=(batch,),
        in_specs=[
            pl.BlockSpec((1, seq, width), lambda b: (b, 0, 0)),
            pl.BlockSpec(pw.shape, lambda b: (0, 0, 0)),
            pl.BlockSpec((1, width), lambda b: (0, 0)),
        ],
        out_specs=pl.BlockSpec((1, seq, width), lambda b: (b, 0, 0)),
        out_shape=jax.ShapeDtypeStruct((batch, seq, width), BF16),
        compiler_params=pltpu.CompilerParams(
            dimension_semantics=("parallel",), vmem_limit_bytes=VMEM_LIMIT_BYTES),
        name="pool",
    )(u, pw, ps)


def _rope_tables(seq, rot_dim):
    half = rot_dim // 2
    inv = ROPE_THETA ** (-jnp.arange(half, dtype=F32) / half)
    ang = inv[:, None] * jnp.arange(seq, dtype=F32)[None, :]
    return jnp.cos(ang), jnp.sin(ang)


def kernel(x, ffn1_norm, ffn1_w_gate, ffn1_w_up, ffn1_w_down, mix_norm, w_in, q_norm, k_norm,
           pool_w, pool_scale, w_out, ffn2_norm, ffn2_w_gate, ffn2_w_up, ffn2_w_down):
    batch, seq, d = x.shape
    depth = w_in.shape[0]
    top_k = min(INDEX_TOPK, seq // 4)
    idx_scale = (IDX_HEADS ** -0.5) * (IDX_DIM ** -0.5)
    cos, sin = _rope_tables(seq, HEAD_DIM)
    cosi, sini = _rope_tables(seq, IDX_ROPE_DIM)
    n_feat = 3 * ATT_WIDTH + IDX_HEADS * IDX_DIM + IDX_DIM
    xf = x.reshape(batch * seq, d)

    for l in range(depth):
        xf = _ffn(xf, ffn1_norm[l][None], ffn1_w_gate[l], ffn1_w_up[l], ffn1_w_down[l])

        wt, wu = _prep_w_in(w_in[l].T, n_feat + IDX_HEADS)
        qT, kp, vT, qiT, ki, wiT, u = _proj(
            xf, mix_norm[l][None], wt, wu, q_norm[l][:, None], k_norm[l][:, None],
            cos, sin, cosi, sini, batch=batch, seq=seq, idx_scale=idx_scale)

        attn = _attn(qT, kp, vT, qiT, ki, wiT, top_k=top_k)
        pooled = _pool(u.reshape(batch, seq, POOL_WIDTH), pool_w[l].astype(BF16), pool_scale[l][None])

        mixer = (attn.reshape(batch * seq, ATT_WIDTH), pooled.reshape(batch * seq, POOL_WIDTH),
                 w_out[l].astype(BF16))
        xf = _ffn(xf, ffn2_norm[l][None], ffn2_w_gate[l], ffn2_w_up[l], ffn2_w_down[l], mixer)
    return xf.reshape(batch, seq, d)
```

```python
import functools
import math

import jax
import jax.numpy as jnp
import numpy as np
from jax import lax
from jax.experimental import pallas as pl
from jax.experimental.pallas import tpu as pltpu

HEAD_DIM = 64
N_HEADS = 8
ATT_WIDTH = N_HEADS * HEAD_DIM
IDX_HEADS = 4
IDX_DIM = 64
IDX_ROPE_DIM = 32
INDEX_TOPK = 256
POOL_WINDOWS = (2, 4, 8, 16)
POOL_CH = 128
POOL_WIDTH = POOL_CH * len(POOL_WINDOWS)
ROPE_THETA = 10000.0
NORM_EPS = 1e-6
WI_ROWS = 8
V_ROWS = HEAD_DIM + 16
PROJ_GROUP = 512
FEATURE_ROWS = 3 * ATT_WIDTH + IDX_HEADS * IDX_DIM + IDX_DIM + WI_ROWS

VMEM_LIMIT_BYTES = 56 * 1024 * 1024

F32 = jnp.float32
BF16 = jnp.bfloat16
NT_DIMS = (((1,), (1,)), ((), ()))


def _rms(x, g):
    ms = jnp.mean(x * x, axis=-1, keepdims=True)
    return x * lax.rsqrt(ms + NORM_EPS) * g


WEIGHT_CHUNKS = 8
STAGE_SLOTS = 4


def _load_weights_bf16(transfers):
    steps = []
    for src, dst, stage, sems in transfers:
        slots, rows = stage.shape[0], stage.shape[1]
        assert slots == STAGE_SLOTS and (src.shape[0] // rows) % slots == 0
        for c in range(src.shape[0] // rows):
            slot = c % slots
            copy = pltpu.make_async_copy(src.at[pl.ds(c * rows, rows), :], stage.at[slot], sems.at[slot])
            steps.append((copy, dst, c * rows, rows, stage, slot))
    ahead = STAGE_SLOTS - 1
    for copy, *_ in steps[:ahead]:
        copy.start()
    for k, (copy, dst, r0, rows, stage, slot) in enumerate(steps):
        copy.wait()
        if k + ahead < len(steps):
            steps[k + ahead][0].start()
        dst[r0:r0 + rows, :] = stage[slot].astype(BF16)


def _ffn_kernel(x_ref, g_ref, wg_hbm, wu_hbm, wd_hbm, *rest, tf):
    *mixer, o_ref, act_ref, wg_ref, wu_ref, wd_ref, stage_in, stage_out, sem_in, sem_out = rest

    @pl.when(pl.program_id(0) == 0)
    def _():
        _load_weights_bf16([(wg_hbm, wg_ref, stage_in, sem_in), (wu_hbm, wu_ref, stage_in, sem_in),
                            (wd_hbm, wd_ref, stage_out, sem_out)])

    x = x_ref[...]
    if mixer:
        attn_ref, pooled_ref, wo_ref = mixer
        mixed = jnp.concatenate([attn_ref[...], pooled_ref[...]], axis=1)
        x = x + jnp.dot(mixed, wo_ref[...], preferred_element_type=F32)
    h = _rms(x, g_ref[...]).astype(BF16)
    for c in range(wg_ref.shape[1] // tf):
        cols = slice(c * tf, (c + 1) * tf)
        gate = jnp.dot(h, wg_ref[:, cols], preferred_element_type=F32)
        up = jnp.dot(h, wu_ref[:, cols], preferred_element_type=F32)
        act_ref[:, cols] = (gate * jax.nn.sigmoid(gate) * up).astype(BF16)
    o_ref[...] = x + 0.5 * jnp.dot(act_ref[...], wd_ref[...], preferred_element_type=F32)


def _resident(shape):
    return pl.BlockSpec(shape, lambda *_: (0,) * len(shape), pipeline_mode=pl.Buffered(1))


def _ffn(x, g, wg, wu, wd, mixer=(), *, tm=512, tf=256):
    n, d = x.shape
    f = wg.shape[1]
    rows = lambda i: (i, 0)
    mixer_specs = []
    if mixer:
        attn, pooled, wo = mixer
        mixer_specs = [pl.BlockSpec((tm, attn.shape[1]), rows), pl.BlockSpec((tm, pooled.shape[1]), rows),
                       _resident(wo.shape)]
    return pl.pallas_call(
        functools.partial(_ffn_kernel, tf=tf),
        grid=(n // tm,),
        in_specs=[
            pl.BlockSpec((tm, d), rows),
            _resident((1, d)),
            pl.BlockSpec(memory_space=pl.ANY),
            pl.BlockSpec(memory_space=pl.ANY),
            pl.BlockSpec(memory_space=pl.ANY),
            *mixer_specs,
        ],
        out_specs=pl.BlockSpec((tm, d), rows),
        out_shape=jax.ShapeDtypeStruct((n, d), F32),
        scratch_shapes=[
            pltpu.VMEM((tm, f), BF16),
            pltpu.VMEM((d, f), BF16), pltpu.VMEM((d, f), BF16), pltpu.VMEM((f, d), BF16),
            pltpu.VMEM((STAGE_SLOTS, d // WEIGHT_CHUNKS, f), F32),
            pltpu.VMEM((STAGE_SLOTS, f // WEIGHT_CHUNKS, d), F32),
            pltpu.SemaphoreType.DMA((STAGE_SLOTS,)), pltpu.SemaphoreType.DMA((STAGE_SLOTS,)),
        ],
        compiler_params=pltpu.CompilerParams(
            dimension_semantics=("arbitrary",),
            vmem_limit_bytes=VMEM_LIMIT_BYTES),
        name="ffn_mix" if mixer else "ffn",
    )(x, g, wg, wu, wd, *mixer)


LANES = 128


def _prep_w_in_kernel(wT_ref, wt_ref, wu_ref, *, n_feat):
    wt_ref[...] = wT_ref[:wt_ref.shape[0], :].astype(BF16)
    wu_ref[...] = wT_ref[n_feat:, :].T.astype(BF16)


def _prep_w_in(wT, n_feat):
    cols, d = wT.shape
    rows = -(-n_feat // LANES) * LANES
    assert rows <= cols
    return pl.pallas_call(
        functools.partial(_prep_w_in_kernel, n_feat=n_feat),
        out_shape=(jax.ShapeDtypeStruct((rows, d), BF16), jax.ShapeDtypeStruct((d, cols - n_feat), BF16)),
        compiler_params=pltpu.CompilerParams(vmem_limit_bytes=VMEM_LIMIT_BYTES),
        name="prep_w_in",
    )(wT)


def _rope_rows(x, cos, sin, half):
    x1, x2 = x[:half], x[half:2 * half]
    return x1 * cos - x2 * sin, x2 * cos + x1 * sin


def _proj_kernel(x_ref, g_ref, wt_ref, wu_ref, gq_ref, gk_ref, cos_ref, sin_ref,
                 cosi_ref, sini_ref,
                 qT_ref, kp_ref, vT_ref, qiT_ref, ki_ref, wiT_ref, u_ref, *, idx_scale, q_scale):
    h = _rms(x_ref[...], g_ref[...]).astype(BF16)
    u_ref[...] = jnp.dot(h, wu_ref[...], preferred_element_type=F32)

    class _FeatureMajorProjection:
        shape = (FEATURE_ROWS, h.shape[0])

        def __init__(self):
            self._groups = {}

        def __getitem__(self, rows):
            g = rows.start // PROJ_GROUP
            assert (rows.stop - 1) // PROJ_GROUP == g
            if g not in self._groups:
                g0 = g * PROJ_GROUP
                g1 = min(g0 + PROJ_GROUP, self.shape[0])
                self._groups[g] = lax.dot_general(wt_ref[g0:g1, :], h, NT_DIMS,
                                                  preferred_element_type=F32)
            return self._groups[g][rows.start - g * PROJ_GROUP:rows.stop - g * PROJ_GROUP]

    pT = _FeatureMajorProjection()

    cos, sin = cos_ref[...], sin_ref[...]
    gq, gk = gq_ref[...], gk_ref[...]
    half = HEAD_DIM // 2

    def head_norm_rope(rows, gain):
        ms = jnp.mean(rows * rows, axis=0, keepdims=True)
        return _rope_rows(rows * lax.rsqrt(ms + NORM_EPS) * gain, cos, sin, half)

    for hd in range(N_HEADS):
        r0 = hd * HEAD_DIM
        a, b = head_norm_rope(pT[r0:r0 + HEAD_DIM], gq)
        qT_ref[0, r0:r0 + half, :] = (a * q_scale).astype(BF16)
        qT_ref[0, r0 + half:r0 + HEAD_DIM, :] = (b * q_scale).astype(BF16)

    k0 = ATT_WIDTH
    for pair in range(N_HEADS // 2):
        parts = []
        for e in range(2):
            r0 = k0 + (2 * pair + e) * HEAD_DIM
            parts.extend(head_norm_rope(pT[r0:r0 + HEAD_DIM], gk))
        kp_ref[0, pair] = jnp.concatenate(parts, axis=0).T.astype(BF16)

    v0 = 2 * ATT_WIDTH
    tt = pT.shape[1]
    ones_rows = jnp.where(lax.broadcasted_iota(jnp.int32, (V_ROWS - HEAD_DIM, tt), 0) == 0,
                          1.0, 0.0).astype(BF16)
    for hd in range(N_HEADS):
        r0 = v0 + hd * HEAD_DIM
        vT_ref[0, hd, :HEAD_DIM, :] = pT[r0:r0 + HEAD_DIM].astype(BF16)
        vT_ref[0, hd, HEAD_DIM:, :] = ones_rows

    cosi, sini = cosi_ref[...], sini_ref[...]
    ihalf = IDX_ROPE_DIM // 2
    qi0 = 3 * ATT_WIDTH
    for hi in range(IDX_HEADS):
        r0 = qi0 + hi * IDX_DIM
        a, b = _rope_rows(pT[r0:r0 + IDX_DIM], cosi, sini, ihalf)
        qiT_ref[0, r0 - qi0:r0 - qi0 + ihalf, :] = a.astype(BF16)
        qiT_ref[0, r0 - qi0 + ihalf:r0 - qi0 + IDX_ROPE_DIM, :] = b.astype(BF16)
        qiT_ref[0, r0 - qi0 + IDX_ROPE_DIM:r0 - qi0 + IDX_DIM, :] = (
            pT[r0 + IDX_ROPE_DIM:r0 + IDX_DIM].astype(BF16))

    ki0 = qi0 + IDX_HEADS * IDX_DIM
    a, b = _rope_rows(pT[ki0:ki0 + IDX_DIM], cosi, sini, ihalf)
    ki = jnp.concatenate([a, b, pT[ki0 + IDX_ROPE_DIM:ki0 + IDX_DIM]], axis=0)
    zeros = jnp.zeros_like(ki)
    ki_ref[0, 0] = jnp.concatenate([ki, zeros], axis=0).T.astype(BF16)
    ki_ref[0, 1] = jnp.concatenate([zeros, ki], axis=0).T.astype(BF16)

    wi0 = ki0 + IDX_DIM
    wiT_ref[0] = pT[wi0:wi0 + WI_ROWS] * idx_scale


def _proj(x1, g, wt, wu, gq, gk, cos, sin, cosi, sini, *, batch, seq, idx_scale, tt=512):
    n, d = x1.shape
    rows = wt.shape[0]
    nt = seq // tt
    tok = lambda b, t: (b * nt + t, 0)
    const = lambda b, t: (0, 0)
    featmaj = lambda b, t: (b, 0, t)
    out_shape = (
        jax.ShapeDtypeStruct((batch, ATT_WIDTH, seq), BF16),
        jax.ShapeDtypeStruct((batch, N_HEADS // 2, seq, 2 * HEAD_DIM), BF16),
        jax.ShapeDtypeStruct((batch, N_HEADS, V_ROWS, seq), BF16),
        jax.ShapeDtypeStruct((batch, IDX_HEADS * IDX_DIM, seq), BF16),
        jax.ShapeDtypeStruct((batch, 2, seq, 2 * IDX_DIM), BF16),
        jax.ShapeDtypeStruct((batch, WI_ROWS, seq), F32),
        jax.ShapeDtypeStruct((n, POOL_WIDTH), F32),
    )
    out_specs = (
        pl.BlockSpec((1, ATT_WIDTH, tt), featmaj),
        pl.BlockSpec((1, N_HEADS // 2, tt, 2 * HEAD_DIM), lambda b, t: (b, 0, t, 0)),
        pl.BlockSpec((1, N_HEADS, V_ROWS, tt), lambda b, t: (b, 0, 0, t)),
        pl.BlockSpec((1, IDX_HEADS * IDX_DIM, tt), featmaj),
        pl.BlockSpec((1, 2, tt, 2 * IDX_DIM), lambda b, t: (b, 0, t, 0)),
        pl.BlockSpec((1, WI_ROWS, tt), featmaj),
        pl.BlockSpec((tt, POOL_WIDTH), tok),
    )
    in_specs = [
        pl.BlockSpec((tt, d), tok),
        pl.BlockSpec((1, d), const),
        pl.BlockSpec((rows, d), const),
        pl.BlockSpec((d, POOL_WIDTH), const),
        pl.BlockSpec((HEAD_DIM, 1), const),
        pl.BlockSpec((HEAD_DIM, 1), const),
        pl.BlockSpec((HEAD_DIM // 2, tt), lambda b, t: (0, t)),
        pl.BlockSpec((HEAD_DIM // 2, tt), lambda b, t: (0, t)),
        pl.BlockSpec((IDX_ROPE_DIM // 2, tt), lambda b, t: (0, t)),
        pl.BlockSpec((IDX_ROPE_DIM // 2, tt), lambda b, t: (0, t)),
    ]
    return pl.pallas_call(
        functools.partial(_proj_kernel, idx_scale=idx_scale,
                          q_scale=HEAD_DIM ** -0.5 * math.log2(math.e)),
        grid=(batch, nt),
        in_specs=in_specs,
        out_specs=out_specs,
        out_shape=out_shape,
        compiler_params=pltpu.CompilerParams(
            dimension_semantics=("parallel", "parallel"),
            vmem_limit_bytes=VMEM_LIMIT_BYTES),
        name="proj",
    )(x1, g, wt, wu, gq, gk, cos, sin, cosi, sini)


INT_MIN = np.int32(-2 ** 31)
COUNT_ROWS = 64
COUNT_ROWS_F32 = 32
TIE_CHUNK = 256
COARSE_BITS = 16
FINE_BITS = 17


def _count(pred):
    ones = jnp.where(pred, jnp.bfloat16(1), jnp.bfloat16(0))
    rows, cols = ones.shape
    groups = ones.reshape(rows // COUNT_ROWS, COUNT_ROWS, cols)
    part = groups[0]
    for i in range(1, rows // COUNT_ROWS):
        part = part + groups[i]
    return part.astype(F32).sum(axis=0, keepdims=True).astype(jnp.int32)


def _count_f32(pred):
    ones = jnp.where(pred, 1.0, 0.0)
    rows, cols = ones.shape
    part = ones.reshape(rows // COUNT_ROWS_F32, COUNT_ROWS_F32, cols).sum(axis=0)
    return part.sum(axis=0, keepdims=True).astype(jnp.int32)


def _value_of_key(key):
    return pltpu.bitcast(jnp.where(key < 0, INT_MIN - key, key), F32)


def _topk_bias(score, causal_bias, score_ref, coarse_ref, bias_out_ref, top_k):
    seq, tq = score.shape
    score_ref[...] = score
    coarse_ref[...] = score.astype(BF16)

    shift, mid = 32 - COARSE_BITS, 1 << (COARSE_BITS - 1)
    found = jnp.zeros((1, tq), jnp.int32)
    for bit in reversed(range(COARSE_BITS)):
        cand = found | (1 << bit)
        cand_value = _value_of_key(lax.shift_left(cand - mid, shift)).astype(BF16)
        found = jnp.where(_count(coarse_ref[...] >= cand_value) >= top_k, cand, found)
    base = lax.shift_left(found - mid, shift) - (1 << (shift - 1))
    offset = jnp.zeros((1, tq), jnp.int32)
    for bit in reversed(range(FINE_BITS)):
        cand = offset | (1 << bit)
        offset = jnp.where(_count_f32(score_ref[...] >= _value_of_key(base + cand)) >= top_k, cand, offset)
    thr = _value_of_key(base + offset)
    need = top_k - _count_f32(score_ref[...] > thr)

    c = TIE_CHUNK
    lower_tri = jnp.where(lax.broadcasted_iota(jnp.int32, (c, c), 0)
                          >= lax.broadcasted_iota(jnp.int32, (c, c), 1), 1.0, 0.0).astype(BF16)
    remaining = need.astype(F32)
    for r0 in range(0, seq, c):
        x = score_ref[r0:r0 + c]
        passes = causal_bias(r0, c)
        ties = x == thr
        rank = jnp.dot(lower_tri, jnp.where(ties, 1.0, 0.0).astype(BF16), preferred_element_type=F32)
        bias_out_ref[r0:r0 + c] = jnp.where(
            x > thr, passes,
            jnp.where(ties, jnp.where(rank <= remaining, passes, -jnp.inf), -jnp.inf)).astype(BF16)
        remaining = remaining - rank[c - 1:c, :]


def _attend_heads(qT_ref, kp_ref, vT_ref, bias_ref, o_ref):
    tq = qT_ref.shape[2]
    half_rows = lax.broadcasted_iota(jnp.int32, (2 * HEAD_DIM, tq), 0) < HEAD_DIM
    outs = []
    for hd in range(N_HEADS):
        pair, e = divmod(hd, 2)
        qpair = qT_ref[0, 2 * HEAD_DIM * pair:2 * HEAD_DIM * (pair + 1), :]
        qz = jnp.where(half_rows if e == 0 else jnp.logical_not(half_rows), qpair,
                       jnp.zeros_like(qpair))
        logits = jnp.dot(kp_ref[0, pair], qz, preferred_element_type=F32).astype(BF16) + bias_ref[...]
        p = jnp.exp2(logits - jnp.max(logits, axis=0, keepdims=True))
        o_aug = jnp.dot(vT_ref[0, hd], p, preferred_element_type=F32)
        outs.append(o_aug[:HEAD_DIM] / o_aug[HEAD_DIM:HEAD_DIM + 1])
    o_ref[0] = jnp.concatenate(outs, axis=0).T.astype(BF16)


def _attn_kernel(qT_ref, kp_ref, vT_ref, qiT_ref, ki_ref, wiT_ref, _aliased_out_ref, o_ref,
                 score_ref, coarse_ref, bias_ref, *, top_k, first_block, n_blocks, n_steps):
    j = pl.program_id(0)
    seq = kp_ref.shape[2]
    tq = qT_ref.shape[2]
    sel_slot = j % 2
    att_slot = 1 - sel_slot

    @pl.when(j == 0)
    def _():
        bias_ref[1] = jnp.zeros((seq, tq), BF16)

    block = first_block + jnp.minimum(j, n_steps - 1) % n_blocks
    all_causal_rows = first_block * tq

    def causal(r0, rows):
        k_pos = r0 + lax.broadcasted_iota(jnp.int32, (rows, tq), 0)
        return k_pos <= block * tq + lax.broadcasted_iota(jnp.int32, (rows, tq), 1)

    def causal_bias(r0, rows):
        if r0 + rows <= all_causal_rows:
            return 0.0
        return jnp.where(causal(r0, rows), 0.0, -jnp.inf)

    wi = wiT_ref[0]
    score = jnp.zeros((seq, tq), F32)
    for hi in range(IDX_HEADS):
        pair, e = divmod(hi, 2)
        rel = jnp.dot(ki_ref[0, e], qiT_ref[0, 2 * IDX_DIM * pair:2 * IDX_DIM * (pair + 1), :],
                      preferred_element_type=F32)
        score = score + wi[hi:hi + 1, :] * jnp.maximum(rel, 0.0)
    tail = jnp.where(causal(all_causal_rows, seq - all_causal_rows), score[all_causal_rows:], -jnp.inf)
    score = jnp.concatenate([score[:all_causal_rows], tail], axis=0) if all_causal_rows else tail

    _attend_heads(qT_ref, kp_ref, vT_ref, bias_ref.at[att_slot], o_ref)
    _topk_bias(score, causal_bias, score_ref, coarse_ref, bias_ref.at[sel_slot], top_k)


def _attn_class(qT, kp, vT, qiT, ki, wiT, out, *, top_k, tq, first_block, n_blocks):
    batch = qT.shape[0]
    seq = (first_block + n_blocks) * tq
    n_steps = batch * n_blocks

    def sel(j):
        return jnp.minimum(j, n_steps - 1)

    def att(j):
        return jnp.maximum(j - 1, 0)

    return pl.pallas_call(
        functools.partial(_attn_kernel, top_k=top_k, first_block=first_block, n_blocks=n_blocks,
                          n_steps=n_steps),
        grid=(n_steps + 1,),
        in_specs=[
            pl.BlockSpec((1, ATT_WIDTH, tq),
                         lambda j: (att(j) // n_blocks, 0, first_block + att(j) % n_blocks)),
            pl.BlockSpec((1, N_HEADS // 2, seq, 2 * HEAD_DIM), lambda j: (att(j) // n_blocks, 0, 0, 0)),
            pl.BlockSpec((1, N_HEADS, V_ROWS, seq), lambda j: (att(j) // n_blocks, 0, 0, 0)),
            pl.BlockSpec((1, IDX_HEADS * IDX_DIM, tq),
                         lambda j: (sel(j) // n_blocks, 0, first_block + sel(j) % n_blocks)),
            pl.BlockSpec((1, 2, seq, 2 * IDX_DIM), lambda j: (sel(j) // n_blocks, 0, 0, 0)),
            pl.BlockSpec((1, WI_ROWS, tq),
                         lambda j: (sel(j) // n_blocks, 0, first_block + sel(j) % n_blocks)),
            pl.BlockSpec(memory_space=pl.ANY),
        ],
        out_specs=pl.BlockSpec((1, tq, ATT_WIDTH),
                               lambda j: (att(j) // n_blocks, first_block + att(j) % n_blocks, 0)),
        out_shape=jax.ShapeDtypeStruct(out.shape, out.dtype),
        input_output_aliases={6: 0},
        scratch_shapes=[pltpu.VMEM((seq, tq), F32), pltpu.VMEM((seq, tq), BF16),
                        pltpu.VMEM((2, seq, tq), BF16)],
        compiler_params=pltpu.CompilerParams(
            dimension_semantics=("arbitrary",),
            vmem_limit_bytes=VMEM_LIMIT_BYTES),
        name=f"attn_k{seq}",
    )(qT, kp, vT, qiT, ki, wiT, out)


def _attn(qT, kp, vT, qiT, ki, wiT, *, top_k, tq=512, blocks_per_class=1):
    batch, width, seq = qT.shape
    out = jnp.zeros((batch, seq, width), BF16)
    for fb in range(0, seq // tq, blocks_per_class):
        out = _attn_class(qT, kp, vT, qiT, ki, wiT, out, top_k=top_k, tq=tq, first_block=fb,
                          n_blocks=blocks_per_class)
    return out


def _pool_kernel(u_ref, pw_ref, ps_ref, o_ref):
    seq = u_ref.shape[1]
    t = lax.broadcasted_iota(jnp.int32, (seq, POOL_CH), 0)
    for g, w in enumerate(POOL_WINDOWS):
        ug = u_ref[0, :, g * POOL_CH:(g + 1) * POOL_CH]
        s, sh = ug, 1
        while sh < w:
            s = s + jnp.where(t >= sh, pltpu.roll(s, sh, axis=0), 0.0)
            sh *= 2
        pooled = s / jnp.minimum(t + 1, w).astype(F32) - ug
        mixed = jnp.dot(pooled.astype(BF16), pw_ref[g], preferred_element_type=F32)
        o_ref[0, :, g * POOL_CH:(g + 1) * POOL_CH] = (
            mixed * ps_ref[:, g * POOL_CH:(g + 1) * POOL_CH]).astype(BF16)


def _pool(u, pw, ps):
    batch, seq, width = u.shape
    return pl.pallas_call(
        _pool_kernel,
        grid=(batch,),
        in_specs=[
            pl.BlockSpec((1, seq, width), lambda b: (b, 0, 0)),
            pl.BlockSpec(pw.shape, lambda b: (0, 0, 0)),
            pl.BlockSpec((1, width), lambda b: (0, 0)),
        ],
        out_specs=pl.BlockSpec((1, seq, width), lambda b: (b, 0, 0)),
        out_shape=jax.ShapeDtypeStruct((batch, seq, width), BF16),
        compiler_params=pltpu.CompilerParams(
            dimension_semantics=("parallel",), vmem_limit_bytes=VMEM_LIMIT_BYTES),
        name="pool",
    )(u, pw, ps)


def _rope_tables(seq, rot_dim):
    half = rot_dim // 2
    inv = ROPE_THETA ** (-jnp.arange(half, dtype=F32) / half)
    ang = inv[:, None] * jnp.arange(seq, dtype=F32)[None, :]
    return jnp.cos(ang), jnp.sin(ang)


def kernel(x, ffn1_norm, ffn1_w_gate, ffn1_w_up, ffn1_w_down, mix_norm, w_in, q_norm, k_norm,
           pool_w, pool_scale, w_out, ffn2_norm, ffn2_w_gate, ffn2_w_up, ffn2_w_down):
    batch, seq, d = x.shape
    depth = w_in.shape[0]
    top_k = min(INDEX_TOPK, seq // 4)
    idx_scale = (IDX_HEADS ** -0.5) * (IDX_DIM ** -0.5)
    cos, sin = _rope_tables(seq, HEAD_DIM)
    cosi, sini = _rope_tables(seq, IDX_ROPE_DIM)
    n_feat = 3 * ATT_WIDTH + IDX_HEADS * IDX_DIM + IDX_DIM
    xf = x.reshape(batch * seq, d)

    for l in range(depth):
        xf = _ffn(xf, ffn1_norm[l][None], ffn1_w_gate[l], ffn1_w_up[l], ffn1_w_down[l])

        wt, wu = _prep_w_in(w_in[l].T, n_feat + IDX_HEADS)
        qT, kp, vT, qiT, ki, wiT, u = _proj(
            xf, mix_norm[l][None], wt, wu, q_norm[l][:, None], k_norm[l][:, None],
            cos, sin, cosi, sini, batch=batch, seq=seq, idx_scale=idx_scale)

        attn = _attn(qT, kp, vT, qiT, ki, wiT, top_k=top_k)
        pooled = _pool(u.reshape(batch, seq, POOL_WIDTH), pool_w[l].astype(BF16), pool_scale[l][None])

        mixer = (attn.reshape(batch * seq, ATT_WIDTH), pooled.reshape(batch * seq, POOL_WIDTH),
                 w_out[l].astype(BF16))
        xf = _ffn(xf, ffn2_norm[l][None], ffn2_w_gate[l], ffn2_w_up[l], ffn2_w_down[l], mixer)
    return xf.reshape(batch, seq, d)
```

```python
import functools
import math

import jax
import jax.numpy as jnp
import numpy as np
from jax import lax
from jax.experimental import pallas as pl
from jax.experimental.pallas import tpu as pltpu

HEAD_DIM = 64
N_HEADS = 8
ATT_WIDTH = N_HEADS * HEAD_DIM
IDX_HEADS = 4
IDX_DIM = 64
IDX_ROPE_DIM = 32
INDEX_TOPK = 256
POOL_WINDOWS = (2, 4, 8, 16)
POOL_CH = 128
POOL_WIDTH = POOL_CH * len(POOL_WINDOWS)
ROPE_THETA = 10000.0
NORM_EPS = 1e-6
WI_ROWS = 8
V_ROWS = HEAD_DIM + 16
PROJ_GROUP = 512
FEATURE_ROWS = 3 * ATT_WIDTH + IDX_HEADS * IDX_DIM + IDX_DIM + WI_ROWS

VMEM_LIMIT_BYTES = 56 * 1024 * 1024

F32 = jnp.float32
BF16 = jnp.bfloat16
NT_DIMS = (((1,), (1,)), ((), ()))


def _rms(x, g):
    ms = jnp.mean(x * x, axis=-1, keepdims=True)
    return x * lax.rsqrt(ms + NORM_EPS) * g


WEIGHT_CHUNKS = 8
STAGE_SLOTS = 4


def _load_weights_bf16(transfers):
    steps = []
    for src, dst, stage, sems in transfers:
        slots, rows = stage.shape[0], stage.shape[1]
        assert slots == STAGE_SLOTS and (src.shape[0] // rows) % slots == 0
        for c in range(src.shape[0] // rows):
            slot = c % slots
            copy = pltpu.make_async_copy(src.at[pl.ds(c * rows, rows), :], stage.at[slot], sems.at[slot])
            steps.append((copy, dst, c * rows, rows, stage, slot))
    ahead = STAGE_SLOTS - 1
    for copy, *_ in steps[:ahead]:
        copy.start()
    for k, (copy, dst, r0, rows, stage, slot) in enumerate(steps):
        copy.wait()
        if k + ahead < len(steps):
            steps[k + ahead][0].start()
        dst[r0:r0 + rows, :] = stage[slot].astype(BF16)


def _ffn_kernel(x_ref, g_ref, wg_hbm, wu_hbm, wd_hbm, *rest, tf):
    *mixer, o_ref, act_ref, wg_ref, wu_ref, wd_ref, stage_in, stage_out, sem_in, sem_out = rest

    @pl.when(pl.program_id(0) == 0)
    def _():
        _load_weights_bf16([(wg_hbm, wg_ref, stage_in, sem_in), (wu_hbm, wu_ref, stage_in, sem_in),
                            (wd_hbm, wd_ref, stage_out, sem_out)])

    x = x_ref[...]
    if mixer:
        attn_ref, pooled_ref, wo_ref = mixer
        mixed = jnp.concatenate([attn_ref[...], pooled_ref[...]], axis=1)
        x = x + jnp.dot(mixed, wo_ref[...], preferred_element_type=F32)
    h = _rms(x, g_ref[...]).astype(BF16)
    for c in range(wg_ref.shape[1] // tf):
        cols = slice(c * tf, (c + 1) * tf)
        gate = jnp.dot(h, wg_ref[:, cols], preferred_element_type=F32)
        up = jnp.dot(h, wu_ref[:, cols], preferred_element_type=F32)
        act_ref[:, cols] = (gate * jax.nn.sigmoid(gate) * up).astype(BF16)
    o_ref[...] = x + 0.5 * jnp.dot(act_ref[...], wd_ref[...], preferred_element_type=F32)


def _resident(shape):
    return pl.BlockSpec(shape, lambda *_: (0,) * len(shape), pipeline_mode=pl.Buffered(1))


def _ffn(x, g, wg, wu, wd, mixer=(), *, tm=512, tf=256):
    n, d = x.shape
    f = wg.shape[1]
    rows = lambda i: (i, 0)
    mixer_specs = []
    if mixer:
        attn, pooled, wo = mixer
        mixer_specs = [pl.BlockSpec((tm, attn.shape[1]), rows), pl.BlockSpec((tm, pooled.shape[1]), rows),
                       _resident(wo.shape)]
    return pl.pallas_call(
        functools.partial(_ffn_kernel, tf=tf),
        grid=(n // tm,),
        in_specs=[
            pl.BlockSpec((tm, d), rows),
            _resident((1, d)),
            pl.BlockSpec(memory_space=pl.ANY),
            pl.BlockSpec(memory_space=pl.ANY),
            pl.BlockSpec(memory_space=pl.ANY),
            *mixer_specs,
        ],
        out_specs=pl.BlockSpec((tm, d), rows),
        out_shape=jax.ShapeDtypeStruct((n, d), F32),
        scratch_shapes=[
            pltpu.VMEM((tm, f), BF16),
            pltpu.VMEM((d, f), BF16), pltpu.VMEM((d, f), BF16), pltpu.VMEM((f, d), BF16),
            pltpu.VMEM((STAGE_SLOTS, d // WEIGHT_CHUNKS, f), F32),
            pltpu.VMEM((STAGE_SLOTS, f // WEIGHT_CHUNKS, d), F32),
            pltpu.SemaphoreType.DMA((STAGE_SLOTS,)), pltpu.SemaphoreType.DMA((STAGE_SLOTS,)),
        ],
        compiler_params=pltpu.CompilerParams(
            dimension_semantics=("arbitrary",),
            vmem_limit_bytes=VMEM_LIMIT_BYTES),
        name="ffn_mix" if mixer else "ffn",
    )(x, g, wg, wu, wd, *mixer)


LANES = 128


def _prep_w_in_kernel(wT_ref, wt_ref, wu_ref, *, n_feat):
    wt_ref[...] = wT_ref[:wt_ref.shape[0], :].astype(BF16)
    wu_ref[...] = wT_ref[n_feat:, :].T.astype(BF16)


def _prep_w_in(wT, n_feat):
    cols, d = wT.shape
    rows = -(-n_feat // LANES) * LANES
    assert rows <= cols
    return pl.pallas_call(
        functools.partial(_prep_w_in_kernel, n_feat=n_feat),
        out_shape=(jax.ShapeDtypeStruct((rows, d), BF16), jax.ShapeDtypeStruct((d, cols - n_feat), BF16)),
        compiler_params=pltpu.CompilerParams(vmem_limit_bytes=VMEM_LIMIT_BYTES),
        name="prep_w_in",
    )(wT)


def _rope_rows(x, cos, sin, half):
    x1, x2 = x[:half], x[half:2 * half]
    return x1 * cos - x2 * sin, x2 * cos + x1 * sin


def _proj_kernel(x_ref, g_ref, wt_ref, wu_ref, gq_ref, gk_ref, cos_ref, sin_ref,
                 cosi_ref, sini_ref,
                 qT_ref, kp_ref, vT_ref, qiT_ref, ki_ref, wiT_ref, u_ref, *, idx_scale, q_scale):
    h = _rms(x_ref[...], g_ref[...]).astype(BF16)
    u_ref[...] = jnp.dot(h, wu_ref[...], preferred_element_type=F32)

    class _FeatureMajorProjection:
        shape = (FEATURE_ROWS, h.shape[0])

        def __init__(self):
            self._groups = {}

        def __getitem__(self, rows):
            g = rows.start // PROJ_GROUP
            assert (rows.stop - 1) // PROJ_GROUP == g
            if g not in self._groups:
                g0 = g * PROJ_GROUP
                g1 = min(g0 + PROJ_GROUP, self.shape[0])
                self._groups[g] = lax.dot_general(wt_ref[g0:g1, :], h, NT_DIMS,
                                                  preferred_element_type=F32)
            return self._groups[g][rows.start - g * PROJ_GROUP:rows.stop - g * PROJ_GROUP]

    pT = _FeatureMajorProjection()

    cos, sin = cos_ref[...], sin_ref[...]
    gq, gk = gq_ref[...], gk_ref[...]
    half = HEAD_DIM // 2

    def head_norm_rope(rows, gain):
        ms = jnp.mean(rows * rows, axis=0, keepdims=True)
        return _rope_rows(rows * lax.rsqrt(ms + NORM_EPS) * gain, cos, sin, half)

    for hd in range(N_HEADS):
        r0 = hd * HEAD_DIM
        a, b = head_norm_rope(pT[r0:r0 + HEAD_DIM], gq)
        qT_ref[0, r0:r0 + half, :] = (a * q_scale).astype(BF16)
        qT_ref[0, r0 + half:r0 + HEAD_DIM, :] = (b * q_scale).astype(BF16)

    k0 = ATT_WIDTH
    for pair in range(N_HEADS // 2):
        parts = []
        for e in range(2):
            r0 = k0 + (2 * pair + e) * HEAD_DIM
            parts.extend(head_norm_rope(pT[r0:r0 + HEAD_DIM], gk))
        kp_ref[0, pair] = jnp.concatenate(parts, axis=0).T.astype(BF16)

    v0 = 2 * ATT_WIDTH
    tt = pT.shape[1]
    ones_rows = jnp.where(lax.broadcasted_iota(jnp.int32, (V_ROWS - HEAD_DIM, tt), 0) == 0,
                          1.0, 0.0).astype(BF16)
    for hd in range(N_HEADS):
        r0 = v0 + hd * HEAD_DIM
        vT_ref[0, hd, :HEAD_DIM, :] = pT[r0:r0 + HEAD_DIM].astype(BF16)
        vT_ref[0, hd, HEAD_DIM:, :] = ones_rows

    cosi, sini = cosi_ref[...], sini_ref[...]
    ihalf = IDX_ROPE_DIM // 2
    qi0 = 3 * ATT_WIDTH
    for hi in range(IDX_HEADS):
        r0 = qi0 + hi * IDX_DIM
        a, b = _rope_rows(pT[r0:r0 + IDX_DIM], cosi, sini, ihalf)
        qiT_ref[0, r0 - qi0:r0 - qi0 + ihalf, :] = a.astype(BF16)
        qiT_ref[0, r0 - qi0 + ihalf:r0 - qi0 + IDX_ROPE_DIM, :] = b.astype(BF16)
        qiT_ref[0, r0 - qi0 + IDX_ROPE_DIM:r0 - qi0 + IDX_DIM, :] = (
            pT[r0 + IDX_ROPE_DIM:r0 + IDX_DIM].astype(BF16))

    ki0 = qi0 + IDX_HEADS * IDX_DIM
    a, b = _rope_rows(pT[ki0:ki0 + IDX_DIM], cosi, sini, ihalf)
    ki = jnp.concatenate([a, b, pT[ki0 + IDX_ROPE_DIM:ki0 + IDX_DIM]], axis=0)
    zeros = jnp.zeros_like(ki)
    ki_ref[0, 0] = jnp.concatenate([ki, zeros], axis=0).T.astype(BF16)
    ki_ref[0, 1] = jnp.concatenate([zeros, ki], axis=0).T.astype(BF16)

    wi0 = ki0 + IDX_DIM
    wiT_ref[0] = pT[wi0:wi0 + WI_ROWS] * idx_scale


def _proj(x1, g, wt, wu, gq, gk, cos, sin, cosi, sini, *, batch, seq, idx_scale, tt=1024):
    n, d = x1.shape
    rows = wt.shape[0]
    nt = seq // tt
    tok = lambda b, t: (b * nt + t, 0)
    const = lambda b, t: (0, 0)
    featmaj = lambda b, t: (b, 0, t)
    out_shape = (
        jax.ShapeDtypeStruct((batch, ATT_WIDTH, seq), BF16),
        jax.ShapeDtypeStruct((batch, N_HEADS // 2, seq, 2 * HEAD_DIM), BF16),
        jax.ShapeDtypeStruct((batch, N_HEADS, V_ROWS, seq), BF16),
        jax.ShapeDtypeStruct((batch, IDX_HEADS * IDX_DIM, seq), BF16),
        jax.ShapeDtypeStruct((batch, 2, seq, 2 * IDX_DIM), BF16),
        jax.ShapeDtypeStruct((batch, WI_ROWS, seq), F32),
        jax.ShapeDtypeStruct((n, POOL_WIDTH), F32),
    )
    out_specs = (
        pl.BlockSpec((1, ATT_WIDTH, tt), featmaj),
        pl.BlockSpec((1, N_HEADS // 2, tt, 2 * HEAD_DIM), lambda b, t: (b, 0, t, 0)),
        pl.BlockSpec((1, N_HEADS, V_ROWS, tt), lambda b, t: (b, 0, 0, t)),
        pl.BlockSpec((1, IDX_HEADS * IDX_DIM, tt), featmaj),
        pl.BlockSpec((1, 2, tt, 2 * IDX_DIM), lambda b, t: (b, 0, t, 0)),
        pl.BlockSpec((1, WI_ROWS, tt), featmaj),
        pl.BlockSpec((tt, POOL_WIDTH), tok),
    )
    in_specs = [
        pl.BlockSpec((tt, d), tok),
        pl.BlockSpec((1, d), const),
        pl.BlockSpec((rows, d), const),
        pl.BlockSpec((d, POOL_WIDTH), const),
        pl.BlockSpec((HEAD_DIM, 1), const),
        pl.BlockSpec((HEAD_DIM, 1), const),
        pl.BlockSpec((HEAD_DIM // 2, tt), lambda b, t: (0, t)),
        pl.BlockSpec((HEAD_DIM // 2, tt), lambda b, t: (0, t)),
        pl.BlockSpec((IDX_ROPE_DIM // 2, tt), lambda b, t: (0, t)),
        pl.BlockSpec((IDX_ROPE_DIM // 2, tt), lambda b, t: (0, t)),
    ]
    return pl.pallas_call(
        functools.partial(_proj_kernel, idx_scale=idx_scale,
                          q_scale=HEAD_DIM ** -0.5 * math.log2(math.e)),
        grid=(batch, nt),
        in_specs=in_specs,
        out_specs=out_specs,
        out_shape=out_shape,
        compiler_params=pltpu.CompilerParams(
            dimension_semantics=("parallel", "parallel"),
            vmem_limit_bytes=VMEM_LIMIT_BYTES),
        name="proj",
    )(x1, g, wt, wu, gq, gk, cos, sin, cosi, sini)


INT_MIN = np.int32(-2 ** 31)
COUNT_ROWS = 64
COUNT_ROWS_F32 = 32
TIE_CHUNK = 256
COARSE_BITS = 16
FINE_BITS = 17


def _count(pred):
    ones = jnp.where(pred, jnp.bfloat16(1), jnp.bfloat16(0))
    rows, cols = ones.shape
    groups = ones.reshape(rows // COUNT_ROWS, COUNT_ROWS, cols)
    part = groups[0]
    for i in range(1, rows // COUNT_ROWS):
        part = part + groups[i]
    return part.astype(F32).sum(axis=0, keepdims=True).astype(jnp.int32)


def _count_f32(pred):
    ones = jnp.where(pred, 1.0, 0.0)
    rows, cols = ones.shape
    part = ones.reshape(rows // COUNT_ROWS_F32, COUNT_ROWS_F32, cols).sum(axis=0)
    return part.sum(axis=0, keepdims=True).astype(jnp.int32)


def _value_of_key(key):
    return pltpu.bitcast(jnp.where(key < 0, INT_MIN - key, key), F32)


def _topk_bias(score, causal_bias, score_ref, coarse_ref, bias_out_ref, top_k):
    seq, tq = score.shape
    score_ref[...] = score
    coarse_ref[...] = score.astype(BF16)

    shift, mid = 32 - COARSE_BITS, 1 << (COARSE_BITS - 1)
    found = jnp.zeros((1, tq), jnp.int32)
    for bit in reversed(range(COARSE_BITS)):
        cand = found | (1 << bit)
        cand_value = _value_of_key(lax.shift_left(cand - mid, shift)).astype(BF16)
        found = jnp.where(_count(coarse_ref[...] >= cand_value) >= top_k, cand, found)
    base = lax.shift_left(found - mid, shift) - (1 << (shift - 1))
    offset = jnp.zeros((1, tq), jnp.int32)
    for bit in reversed(range(FINE_BITS)):
        cand = offset | (1 << bit)
        offset = jnp.where(_count_f32(score_ref[...] >= _value_of_key(base + cand)) >= top_k, cand, offset)
    thr = _value_of_key(base + offset)
    need = top_k - _count_f32(score_ref[...] > thr)

    c = TIE_CHUNK
    lower_tri = jnp.where(lax.broadcasted_iota(jnp.int32, (c, c), 0)
                          >= lax.broadcasted_iota(jnp.int32, (c, c), 1), 1.0, 0.0).astype(BF16)
    remaining = need.astype(F32)
    for r0 in range(0, seq, c):
        x = score_ref[r0:r0 + c]
        passes = causal_bias(r0, c)
        ties = x == thr
        rank = jnp.dot(lower_tri, jnp.where(ties, 1.0, 0.0).astype(BF16), preferred_element_type=F32)
        bias_out_ref[r0:r0 + c] = jnp.where(
            x > thr, passes,
            jnp.where(ties, jnp.where(rank <= remaining, passes, -jnp.inf), -jnp.inf)).astype(BF16)
        remaining = remaining - rank[c - 1:c, :]


def _attend_heads(qT_ref, kp_ref, vT_ref, bias_ref, o_ref):
    tq = qT_ref.shape[2]
    half_rows = lax.broadcasted_iota(jnp.int32, (2 * HEAD_DIM, tq), 0) < HEAD_DIM
    outs = []
    for hd in range(N_HEADS):
        pair, e = divmod(hd, 2)
        qpair = qT_ref[0, 2 * HEAD_DIM * pair:2 * HEAD_DIM * (pair + 1), :]
        qz = jnp.where(half_rows if e == 0 else jnp.logical_not(half_rows), qpair,
                       jnp.zeros_like(qpair))
        logits = jnp.dot(kp_ref[0, pair], qz, preferred_element_type=F32).astype(BF16) + bias_ref[...]
        p = jnp.exp2(logits - jnp.max(logits, axis=0, keepdims=True))
        o_aug = jnp.dot(vT_ref[0, hd], p, preferred_element_type=F32)
        outs.append(o_aug[:HEAD_DIM] / o_aug[HEAD_DIM:HEAD_DIM + 1])
    o_ref[0] = jnp.concatenate(outs, axis=0).T.astype(BF16)


def _attn_kernel(qT_ref, kp_ref, vT_ref, qiT_ref, ki_ref, wiT_ref, _aliased_out_ref, o_ref,
                 score_ref, coarse_ref, bias_ref, *, top_k, first_block, n_blocks, n_steps):
    j = pl.program_id(0)
    seq = kp_ref.shape[2]
    tq = qT_ref.shape[2]
    sel_slot = j % 2
    att_slot = 1 - sel_slot

    @pl.when(j == 0)
    def _():
        bias_ref[1] = jnp.zeros((seq, tq), BF16)

    block = first_block + jnp.minimum(j, n_steps - 1) % n_blocks
    all_causal_rows = first_block * tq

    def causal(r0, rows):
        k_pos = r0 + lax.broadcasted_iota(jnp.int32, (rows, tq), 0)
        return k_pos <= block * tq + lax.broadcasted_iota(jnp.int32, (rows, tq), 1)

    def causal_bias(r0, rows):
        if r0 + rows <= all_causal_rows:
            return 0.0
        return jnp.where(causal(r0, rows), 0.0, -jnp.inf)

    wi = wiT_ref[0]
    score = jnp.zeros((seq, tq), F32)
    for hi in range(IDX_HEADS):
        pair, e = divmod(hi, 2)
        rel = jnp.dot(ki_ref[0, e], qiT_ref[0, 2 * IDX_DIM * pair:2 * IDX_DIM * (pair + 1), :],
                      preferred_element_type=F32)
        score = score + wi[hi:hi + 1, :] * jnp.maximum(rel, 0.0)
    tail = jnp.where(causal(all_causal_rows, seq - all_causal_rows), score[all_causal_rows:], -jnp.inf)
    score = jnp.concatenate([score[:all_causal_rows], tail], axis=0) if all_causal_rows else tail

    _attend_heads(qT_ref, kp_ref, vT_ref, bias_ref.at[att_slot], o_ref)
    _topk_bias(score, causal_bias, score_ref, coarse_ref, bias_ref.at[sel_slot], top_k)


def _attn_class(qT, kp, vT, qiT, ki, wiT, out, *, top_k, tq, first_block, n_blocks):
    batch = qT.shape[0]
    seq = (first_block + n_blocks) * tq
    n_steps = batch * n_blocks

    def sel(j):
        return jnp.minimum(j, n_steps - 1)

    def att(j):
        return jnp.maximum(j - 1, 0)

    return pl.pallas_call(
        functools.partial(_attn_kernel, top_k=top_k, first_block=first_block, n_blocks=n_blocks,
                          n_steps=n_steps),
        grid=(n_steps + 1,),
        in_specs=[
            pl.BlockSpec((1, ATT_WIDTH, tq),
                         lambda j: (att(j) // n_blocks, 0, first_block + att(j) % n_blocks)),
            pl.BlockSpec((1, N_HEADS // 2, seq, 2 * HEAD_DIM), lambda j: (att(j) // n_blocks, 0, 0, 0)),
            pl.BlockSpec((1, N_HEADS, V_ROWS, seq), lambda j: (att(j) // n_blocks, 0, 0, 0)),
            pl.BlockSpec((1, IDX_HEADS * IDX_DIM, tq),
                         lambda j: (sel(j) // n_blocks, 0, first_block + sel(j) % n_blocks)),
            pl.BlockSpec((1, 2, seq, 2 * IDX_DIM), lambda j: (sel(j) // n_blocks, 0, 0, 0)),
            pl.BlockSpec((1, WI_ROWS, tq),
                         lambda j: (sel(j) // n_blocks, 0, first_block + sel(j) % n_blocks)),
            pl.BlockSpec(memory_space=pl.ANY),
        ],
        out_specs=pl.BlockSpec((1, tq, ATT_WIDTH),
                               lambda j: (att(j) // n_blocks, first_block + att(j) % n_blocks, 0)),
        out_shape=jax.ShapeDtypeStruct(out.shape, out.dtype),
        input_output_aliases={6: 0},
        scratch_shapes=[pltpu.VMEM((seq, tq), F32), pltpu.VMEM((seq, tq), BF16),
                        pltpu.VMEM((2, seq, tq), BF16)],
        compiler_params=pltpu.CompilerParams(
            dimension_semantics=("arbitrary",),
            vmem_limit_bytes=VMEM_LIMIT_BYTES),
        name=f"attn_k{seq}",
    )(qT, kp, vT, qiT, ki, wiT, out)


def _attn(qT, kp, vT, qiT, ki, wiT, *, top_k, tq=512, blocks_per_class=1):
    batch, width, seq = qT.shape
    out = jnp.zeros((batch, seq, width), BF16)
    for fb in range(0, seq // tq, blocks_per_class):
        out = _attn_class(qT, kp, vT, qiT, ki, wiT, out, top_k=top_k, tq=tq, first_block=fb,
                          n_blocks=blocks_per_class)
    return out


def _pool_kernel(u_ref, pw_ref, ps_ref, o_ref):
    seq = u_ref.shape[1]
    t = lax.broadcasted_iota(jnp.int32, (seq, POOL_CH), 0)
    for g, w in enumerate(POOL_WINDOWS):
        ug = u_ref[0, :, g * POOL_CH:(g + 1) * POOL_CH]
        s, sh = ug, 1
        while sh < w:
            s = s + jnp.where(t >= sh, pltpu.roll(s, sh, axis=0), 0.0)
            sh *= 2
        pooled = s / jnp.minimum(t + 1, w).astype(F32) - ug
        mixed = jnp.dot(pooled.astype(BF16), pw_ref[g], preferred_element_type=F32)
        o_ref[0, :, g * POOL_CH:(g + 1) * POOL_CH] = (
            mixed * ps_ref[:, g * POOL_CH:(g + 1) * POOL_CH]).astype(BF16)


def _pool(u, pw, ps):
    batch, seq, width = u.shape
    return pl.pallas_call(
        _pool_kernel,
        grid=(batch,),
        in_specs=[
            pl.BlockSpec((1, seq, width), lambda b: (b, 0, 0)),
            pl.BlockSpec(pw.shape, lambda b: (0, 0, 0)),
            pl.BlockSpec((1, width), lambda b: (0, 0)),
        ],
        out_specs=pl.BlockSpec((1, seq, width), lambda b: (b, 0, 0)),
        out_shape=jax.ShapeDtypeStruct((batch, seq, width), BF16),
        compiler_params=pltpu.CompilerParams(
            dimension_semantics=("parallel",), vmem_limit_bytes=VMEM_LIMIT_BYTES),
        name="pool",
    )(u, pw, ps)


def _rope_tables(seq, rot_dim):
    half = rot_dim // 2
    inv = ROPE_THETA ** (-jnp.arange(half, dtype=F32) / half)
    ang = inv[:, None] * jnp.arange(seq, dtype=F32)[None, :]
    return jnp.cos(ang), jnp.sin(ang)


def kernel(x, ffn1_norm, ffn1_w_gate, ffn1_w_up, ffn1_w_down, mix_norm, w_in, q_norm, k_norm,
           pool_w, pool_scale, w_out, ffn2_norm, ffn2_w_gate, ffn2_w_up, ffn2_w_down):
    batch, seq, d = x.shape
    depth = w_in.shape[0]
    top_k = min(INDEX_TOPK, seq // 4)
    idx_scale = (IDX_HEADS ** -0.5) * (IDX_DIM ** -0.5)
    cos, sin = _rope_tables(seq, HEAD_DIM)
    cosi, sini = _rope_tables(seq, IDX_ROPE_DIM)
    n_feat = 3 * ATT_WIDTH + IDX_HEADS * IDX_DIM + IDX_DIM
    xf = x.reshape(batch * seq, d)

    for l in range(depth):
        xf = _ffn(xf, ffn1_norm[l][None], ffn1_w_gate[l], ffn1_w_up[l], ffn1_w_down[l])

        wt, wu = _prep_w_in(w_in[l].T, n_feat + IDX_HEADS)
        qT, kp, vT, qiT, ki, wiT, u = _proj(
            xf, mix_norm[l][None], wt, wu, q_norm[l][:, None], k_norm[l][:, None],
            cos, sin, cosi, sini, batch=batch, seq=seq, idx_scale=idx_scale)

        attn = _attn(qT, kp, vT, qiT, ki, wiT, top_k=top_k)
        pooled = _pool(u.reshape(batch, seq, POOL_WIDTH), pool_w[l].astype(BF16), pool_scale[l][None])

        mixer = (attn.reshape(batch * seq, ATT_WIDTH), pooled.reshape(batch * seq, POOL_WIDTH),
                 w_out[l].astype(BF16))
        xf = _ffn(xf, ffn2_norm[l][None], ffn2_w_gate[l], ffn2_w_up[l], ffn2_w_down[l], mixer)
    return xf.reshape(batch, seq, d)
```

```python
import functools
import math

import jax
import jax.numpy as jnp
import numpy as np
from jax import lax
from jax.experimental import pallas as pl
from jax.experimental.pallas import tpu as pltpu

HEAD_DIM = 64
N_HEADS = 8
ATT_WIDTH = N_HEADS * HEAD_DIM
IDX_HEADS = 4
IDX_DIM = 64
IDX_ROPE_DIM = 32
INDEX_TOPK = 256
POOL_WINDOWS = (2, 4, 8, 16)
POOL_CH = 128
POOL_WIDTH = POOL_CH * len(POOL_WINDOWS)
ROPE_THETA = 10000.0
NORM_EPS = 1e-6
WI_ROWS = 8
V_ROWS = HEAD_DIM + 16
PROJ_GROUP = 512
FEATURE_ROWS = 3 * ATT_WIDTH + IDX_HEADS * IDX_DIM + IDX_DIM + WI_ROWS

VMEM_LIMIT_BYTES = 56 * 1024 * 1024

F32 = jnp.float32
BF16 = jnp.bfloat16
NT_DIMS = (((1,), (1,)), ((), ()))


def _rms(x, g):
    ms = jnp.mean(x * x, axis=-1, keepdims=True)
    return x * lax.rsqrt(ms + NORM_EPS) * g


WEIGHT_CHUNKS = 16
STAGE_SLOTS = 8


def _load_weights_bf16(transfers):
    steps = []
    for src, dst, stage, sems in transfers:
        slots, rows = stage.shape[0], stage.shape[1]
        assert slots == STAGE_SLOTS and (src.shape[0] // rows) % slots == 0
        for c in range(src.shape[0] // rows):
            slot = c % slots
            copy = pltpu.make_async_copy(src.at[pl.ds(c * rows, rows), :], stage.at[slot], sems.at[slot])
            steps.append((copy, dst, c * rows, rows, stage, slot))
    ahead = STAGE_SLOTS - 1
    for copy, *_ in steps[:ahead]:
        copy.start()
    for k, (copy, dst, r0, rows, stage, slot) in enumerate(steps):
        copy.wait()
        if k + ahead < len(steps):
            steps[k + ahead][0].start()
        dst[r0:r0 + rows, :] = stage[slot].astype(BF16)


def _ffn_kernel(x_ref, g_ref, wg_hbm, wu_hbm, wd_hbm, *rest, tf):
    *mixer, o_ref, act_ref, wg_ref, wu_ref, wd_ref, stage_in, stage_out, sem_in, sem_out = rest

    @pl.when(pl.program_id(0) == 0)
    def _():
        _load_weights_bf16([(wg_hbm, wg_ref, stage_in, sem_in), (wu_hbm, wu_ref, stage_in, sem_in),
                            (wd_hbm, wd_ref, stage_out, sem_out)])

    x = x_ref[...]
    if mixer:
        attn_ref, pooled_ref, wo_ref = mixer
        mixed = jnp.concatenate([attn_ref[...], pooled_ref[...]], axis=1)
        x = x + jnp.dot(mixed, wo_ref[...], preferred_element_type=F32)
    h = _rms(x, g_ref[...]).astype(BF16)
    for c in range(wg_ref.shape[1] // tf):
        cols = slice(c * tf, (c + 1) * tf)
        gate = jnp.dot(h, wg_ref[:, cols], preferred_element_type=F32)
        up = jnp.dot(h, wu_ref[:, cols], preferred_element_type=F32)
        act_ref[:, cols] = (gate * jax.nn.sigmoid(gate) * up).astype(BF16)
    o_ref[...] = x + 0.5 * jnp.dot(act_ref[...], wd_ref[...], preferred_element_type=F32)


def _resident(shape):
    return pl.BlockSpec(shape, lambda *_: (0,) * len(shape), pipeline_mode=pl.Buffered(1))


def _ffn(x, g, wg, wu, wd, mixer=(), *, tm=512, tf=256):
    n, d = x.shape
    f = wg.shape[1]
    rows = lambda i: (i, 0)
    mixer_specs = []
    if mixer:
        attn, pooled, wo = mixer
        mixer_specs = [pl.BlockSpec((tm, attn.shape[1]), rows), pl.BlockSpec((tm, pooled.shape[1]), rows),
                       _resident(wo.shape)]
    return pl.pallas_call(
        functools.partial(_ffn_kernel, tf=tf),
        grid=(n // tm,),
        in_specs=[
            pl.BlockSpec((tm, d), rows),
            _resident((1, d)),
            pl.BlockSpec(memory_space=pl.ANY),
            pl.BlockSpec(memory_space=pl.ANY),
            pl.BlockSpec(memory_space=pl.ANY),
            *mixer_specs,
        ],
        out_specs=pl.BlockSpec((tm, d), rows),
        out_shape=jax.ShapeDtypeStruct((n, d), F32),
        scratch_shapes=[
            pltpu.VMEM((tm, f), BF16),
            pltpu.VMEM((d, f), BF16), pltpu.VMEM((d, f), BF16), pltpu.VMEM((f, d), BF16),
            pltpu.VMEM((STAGE_SLOTS, d // WEIGHT_CHUNKS, f), F32),
            pltpu.VMEM((STAGE_SLOTS, f // WEIGHT_CHUNKS, d), F32),
            pltpu.SemaphoreType.DMA((STAGE_SLOTS,)), pltpu.SemaphoreType.DMA((STAGE_SLOTS,)),
        ],
        compiler_params=pltpu.CompilerParams(
            dimension_semantics=("arbitrary",),
            vmem_limit_bytes=VMEM_LIMIT_BYTES),
        name="ffn_mix" if mixer else "ffn",
    )(x, g, wg, wu, wd, *mixer)


LANES = 128


def _prep_w_in_kernel(wT_ref, wt_ref, wu_ref, *, n_feat):
    wt_ref[...] = wT_ref[:wt_ref.shape[0], :].astype(BF16)
    wu_ref[...] = wT_ref[n_feat:, :].T.astype(BF16)


def _prep_w_in(wT, n_feat):
    cols, d = wT.shape
    rows = -(-n_feat // LANES) * LANES
    assert rows <= cols
    return pl.pallas_call(
        functools.partial(_prep_w_in_kernel, n_feat=n_feat),
        out_shape=(jax.ShapeDtypeStruct((rows, d), BF16), jax.ShapeDtypeStruct((d, cols - n_feat), BF16)),
        compiler_params=pltpu.CompilerParams(vmem_limit_bytes=VMEM_LIMIT_BYTES),
        name="prep_w_in",
    )(wT)


def _rope_rows(x, cos, sin, half):
    x1, x2 = x[:half], x[half:2 * half]
    return x1 * cos - x2 * sin, x2 * cos + x1 * sin


def _proj_kernel(x_ref, g_ref, wt_ref, wu_ref, gq_ref, gk_ref, cos_ref, sin_ref,
                 cosi_ref, sini_ref,
                 qT_ref, kp_ref, vT_ref, qiT_ref, ki_ref, wiT_ref, u_ref, *, idx_scale, q_scale):
    h = _rms(x_ref[...], g_ref[...]).astype(BF16)
    u_ref[...] = jnp.dot(h, wu_ref[...], preferred_element_type=F32)

    class _FeatureMajorProjection:
        shape = (FEATURE_ROWS, h.shape[0])

        def __init__(self):
            self._groups = {}

        def __getitem__(self, rows):
            g = rows.start // PROJ_GROUP
            assert (rows.stop - 1) // PROJ_GROUP == g
            if g not in self._groups:
                g0 = g * PROJ_GROUP
                g1 = min(g0 + PROJ_GROUP, self.shape[0])
                self._groups[g] = lax.dot_general(wt_ref[g0:g1, :], h, NT_DIMS,
                                                  preferred_element_type=F32)
            return self._groups[g][rows.start - g * PROJ_GROUP:rows.stop - g * PROJ_GROUP]

    pT = _FeatureMajorProjection()

    cos, sin = cos_ref[...], sin_ref[...]
    gq, gk = gq_ref[...], gk_ref[...]
    half = HEAD_DIM // 2

    def head_norm_rope(rows, gain):
        ms = jnp.mean(rows * rows, axis=0, keepdims=True)
        return _rope_rows(rows * lax.rsqrt(ms + NORM_EPS) * gain, cos, sin, half)

    for hd in range(N_HEADS):
        r0 = hd * HEAD_DIM
        a, b = head_norm_rope(pT[r0:r0 + HEAD_DIM], gq)
        qT_ref[0, r0:r0 + half, :] = (a * q_scale).astype(BF16)
        qT_ref[0, r0 + half:r0 + HEAD_DIM, :] = (b * q_scale).astype(BF16)

    k0 = ATT_WIDTH
    for pair in range(N_HEADS // 2):
        parts = []
        for e in range(2):
            r0 = k0 + (2 * pair + e) * HEAD_DIM
            parts.extend(head_norm_rope(pT[r0:r0 + HEAD_DIM], gk))
        kp_ref[0, pair] = jnp.concatenate(parts, axis=0).T.astype(BF16)

    v0 = 2 * ATT_WIDTH
    tt = pT.shape[1]
    ones_rows = jnp.where(lax.broadcasted_iota(jnp.int32, (V_ROWS - HEAD_DIM, tt), 0) == 0,
                          1.0, 0.0).astype(BF16)
    for hd in range(N_HEADS):
        r0 = v0 + hd * HEAD_DIM
        vT_ref[0, hd, :HEAD_DIM, :] = pT[r0:r0 + HEAD_DIM].astype(BF16)
        vT_ref[0, hd, HEAD_DIM:, :] = ones_rows

    cosi, sini = cosi_ref[...], sini_ref[...]
    ihalf = IDX_ROPE_DIM // 2
    qi0 = 3 * ATT_WIDTH
    for hi in range(IDX_HEADS):
        r0 = qi0 + hi * IDX_DIM
        a, b = _rope_rows(pT[r0:r0 + IDX_DIM], cosi, sini, ihalf)
        qiT_ref[0, r0 - qi0:r0 - qi0 + ihalf, :] = a.astype(BF16)
        qiT_ref[0, r0 - qi0 + ihalf:r0 - qi0 + IDX_ROPE_DIM, :] = b.astype(BF16)
        qiT_ref[0, r0 - qi0 + IDX_ROPE_DIM:r0 - qi0 + IDX_DIM, :] = (
            pT[r0 + IDX_ROPE_DIM:r0 + IDX_DIM].astype(BF16))

    ki0 = qi0 + IDX_HEADS * IDX_DIM
    a, b = _rope_rows(pT[ki0:ki0 + IDX_DIM], cosi, sini, ihalf)
    ki = jnp.concatenate([a, b, pT[ki0 + IDX_ROPE_DIM:ki0 + IDX_DIM]], axis=0)
    zeros = jnp.zeros_like(ki)
    ki_ref[0, 0] = jnp.concatenate([ki, zeros], axis=0).T.astype(BF16)
    ki_ref[0, 1] = jnp.concatenate([zeros, ki], axis=0).T.astype(BF16)

    wi0 = ki0 + IDX_DIM
    wiT_ref[0] = pT[wi0:wi0 + WI_ROWS] * idx_scale


def _proj(x1, g, wt, wu, gq, gk, cos, sin, cosi, sini, *, batch, seq, idx_scale, tt=1024):
    n, d = x1.shape
    rows = wt.shape[0]
    nt = seq // tt
    tok = lambda b, t: (b * nt + t, 0)
    const = lambda b, t: (0, 0)
    featmaj = lambda b, t: (b, 0, t)
    out_shape = (
        jax.ShapeDtypeStruct((batch, ATT_WIDTH, seq), BF16),
        jax.ShapeDtypeStruct((batch, N_HEADS // 2, seq, 2 * HEAD_DIM), BF16),
        jax.ShapeDtypeStruct((batch, N_HEADS, V_ROWS, seq), BF16),
        jax.ShapeDtypeStruct((batch, IDX_HEADS * IDX_DIM, seq), BF16),
        jax.ShapeDtypeStruct((batch, 2, seq, 2 * IDX_DIM), BF16),
        jax.ShapeDtypeStruct((batch, WI_ROWS, seq), F32),
        jax.ShapeDtypeStruct((n, POOL_WIDTH), F32),
    )
    out_specs = (
        pl.BlockSpec((1, ATT_WIDTH, tt), featmaj),
        pl.BlockSpec((1, N_HEADS // 2, tt, 2 * HEAD_DIM), lambda b, t: (b, 0, t, 0)),
        pl.BlockSpec((1, N_HEADS, V_ROWS, tt), lambda b, t: (b, 0, 0, t)),
        pl.BlockSpec((1, IDX_HEADS * IDX_DIM, tt), featmaj),
        pl.BlockSpec((1, 2, tt, 2 * IDX_DIM), lambda b, t: (b, 0, t, 0)),
        pl.BlockSpec((1, WI_ROWS, tt), featmaj),
        pl.BlockSpec((tt, POOL_WIDTH), tok),
    )
    in_specs = [
        pl.BlockSpec((tt, d), tok),
        pl.BlockSpec((1, d), const),
        pl.BlockSpec((rows, d), const),
        pl.BlockSpec((d, POOL_WIDTH), const),
        pl.BlockSpec((HEAD_DIM, 1), const),
        pl.BlockSpec((HEAD_DIM, 1), const),
        pl.BlockSpec((HEAD_DIM // 2, tt), lambda b, t: (0, t)),
        pl.BlockSpec((HEAD_DIM // 2, tt), lambda b, t: (0, t)),
        pl.BlockSpec((IDX_ROPE_DIM // 2, tt), lambda b, t: (0, t)),
        pl.BlockSpec((IDX_ROPE_DIM // 2, tt), lambda b, t: (0, t)),
    ]
    return pl.pallas_call(
        functools.partial(_proj_kernel, idx_scale=idx_scale,
                          q_scale=HEAD_DIM ** -0.5 * math.log2(math.e)),
        grid=(batch, nt),
        in_specs=in_specs,
        out_specs=out_specs,
        out_shape=out_shape,
        compiler_params=pltpu.CompilerParams(
            dimension_semantics=("parallel", "parallel"),
            vmem_limit_bytes=VMEM_LIMIT_BYTES),
        name="proj",
    )(x1, g, wt, wu, gq, gk, cos, sin, cosi, sini)


INT_MIN = np.int32(-2 ** 31)
COUNT_ROWS = 64
COUNT_ROWS_F32 = 32
TIE_CHUNK = 256
COARSE_BITS = 16
FINE_BITS = 17


def _count(pred):
    ones = jnp.where(pred, jnp.bfloat16(1), jnp.bfloat16(0))
    rows, cols = ones.shape
    groups = ones.reshape(rows // COUNT_ROWS, COUNT_ROWS, cols)
    part = groups[0]
    for i in range(1, rows // COUNT_ROWS):
        part = part + groups[i]
    return part.astype(F32).sum(axis=0, keepdims=True).astype(jnp.int32)


def _count_f32(pred):
    ones = jnp.where(pred, 1.0, 0.0)
    rows, cols = ones.shape
    part = ones.reshape(rows // COUNT_ROWS_F32, COUNT_ROWS_F32, cols).sum(axis=0)
    return part.sum(axis=0, keepdims=True).astype(jnp.int32)


def _value_of_key(key):
    return pltpu.bitcast(jnp.where(key < 0, INT_MIN - key, key), F32)


def _topk_bias(score, causal_bias, score_ref, coarse_ref, bias_out_ref, top_k):
    seq, tq = score.shape
    score_ref[...] = score
    coarse_ref[...] = score.astype(BF16)

    shift, mid = 32 - COARSE_BITS, 1 << (COARSE_BITS - 1)
    found = jnp.zeros((1, tq), jnp.int32)
    for bit in reversed(range(COARSE_BITS)):
        cand = found | (1 << bit)
        cand_value = _value_of_key(lax.shift_left(cand - mid, shift)).astype(BF16)
        found = jnp.where(_count(coarse_ref[...] >= cand_value) >= top_k, cand, found)
    base = lax.shift_left(found - mid, shift) - (1 << (shift - 1))
    offset = jnp.zeros((1, tq), jnp.int32)
    for bit in reversed(range(FINE_BITS)):
        cand = offset | (1 << bit)
        offset = jnp.where(_count_f32(score_ref[...] >= _value_of_key(base + cand)) >= top_k, cand, offset)
    thr = _value_of_key(base + offset)
    need = top_k - _count_f32(score_ref[...] > thr)

    c = TIE_CHUNK
    lower_tri = jnp.where(lax.broadcasted_iota(jnp.int32, (c, c), 0)
                          >= lax.broadcasted_iota(jnp.int32, (c, c), 1), 1.0, 0.0).astype(BF16)
    remaining = need.astype(F32)
    for r0 in range(0, seq, c):
        x = score_ref[r0:r0 + c]
        passes = causal_bias(r0, c)
        ties = x == thr
        rank = jnp.dot(lower_tri, jnp.where(ties, 1.0, 0.0).astype(BF16), preferred_element_type=F32)
        bias_out_ref[r0:r0 + c] = jnp.where(
            x > thr, passes,
            jnp.where(ties, jnp.where(rank <= remaining, passes, -jnp.inf), -jnp.inf)).astype(BF16)
        remaining = remaining - rank[c - 1:c, :]


def _attend_heads(qT_ref, kp_ref, vT_ref, bias_ref, o_ref):
    tq = qT_ref.shape[2]
    half_rows = lax.broadcasted_iota(jnp.int32, (2 * HEAD_DIM, tq), 0) < HEAD_DIM
    outs = []
    for hd in range(N_HEADS):
        pair, e = divmod(hd, 2)
        qpair = qT_ref[0, 2 * HEAD_DIM * pair:2 * HEAD_DIM * (pair + 1), :]
        qz = jnp.where(half_rows if e == 0 else jnp.logical_not(half_rows), qpair,
                       jnp.zeros_like(qpair))
        logits = jnp.dot(kp_ref[0, pair], qz, preferred_element_type=F32).astype(BF16) + bias_ref[...]
        p = jnp.exp2(logits - jnp.max(logits, axis=0, keepdims=True))
        o_aug = jnp.dot(vT_ref[0, hd], p, preferred_element_type=F32)
        outs.append(o_aug[:HEAD_DIM] / o_aug[HEAD_DIM:HEAD_DIM + 1])
    o_ref[0] = jnp.concatenate(outs, axis=0).T.astype(BF16)


def _attn_kernel(qT_ref, kp_ref, vT_ref, qiT_ref, ki_ref, wiT_ref, _aliased_out_ref, o_ref,
                 score_ref, coarse_ref, bias_ref, *, top_k, first_block, n_blocks, n_steps):
    j = pl.program_id(0)
    seq = kp_ref.shape[2]
    tq = qT_ref.shape[2]
    sel_slot = j % 2
    att_slot = 1 - sel_slot

    @pl.when(j == 0)
    def _():
        bias_ref[1] = jnp.zeros((seq, tq), BF16)

    block = first_block + jnp.minimum(j, n_steps - 1) % n_blocks
    all_causal_rows = first_block * tq

    def causal(r0, rows):
        k_pos = r0 + lax.broadcasted_iota(jnp.int32, (rows, tq), 0)
        return k_pos <= block * tq + lax.broadcasted_iota(jnp.int32, (rows, tq), 1)

    def causal_bias(r0, rows):
        if r0 + rows <= all_causal_rows:
            return 0.0
        return jnp.where(causal(r0, rows), 0.0, -jnp.inf)

    wi = wiT_ref[0]
    score = jnp.zeros((seq, tq), F32)
    for hi in range(IDX_HEADS):
        pair, e = divmod(hi, 2)
        rel = jnp.dot(ki_ref[0, e], qiT_ref[0, 2 * IDX_DIM * pair:2 * IDX_DIM * (pair + 1), :],
                      preferred_element_type=F32)
        score = score + wi[hi:hi + 1, :] * jnp.maximum(rel, 0.0)
    tail = jnp.where(causal(all_causal_rows, seq - all_causal_rows), score[all_causal_rows:], -jnp.inf)
    score = jnp.concatenate([score[:all_causal_rows], tail], axis=0) if all_causal_rows else tail

    _attend_heads(qT_ref, kp_ref, vT_ref, bias_ref.at[att_slot], o_ref)
    _topk_bias(score, causal_bias, score_ref, coarse_ref, bias_ref.at[sel_slot], top_k)


def _attn_class(qT, kp, vT, qiT, ki, wiT, out, *, top_k, tq, first_block, n_blocks):
    batch = qT.shape[0]
    seq = (first_block + n_blocks) * tq
    n_steps = batch * n_blocks

    def sel(j):
        return jnp.minimum(j, n_steps - 1)

    def att(j):
        return jnp.maximum(j - 1, 0)

    return pl.pallas_call(
        functools.partial(_attn_kernel, top_k=top_k, first_block=first_block, n_blocks=n_blocks,
                          n_steps=n_steps),
        grid=(n_steps + 1,),
        in_specs=[
            pl.BlockSpec((1, ATT_WIDTH, tq),
                         lambda j: (att(j) // n_blocks, 0, first_block + att(j) % n_blocks)),
            pl.BlockSpec((1, N_HEADS // 2, seq, 2 * HEAD_DIM), lambda j: (att(j) // n_blocks, 0, 0, 0)),
            pl.BlockSpec((1, N_HEADS, V_ROWS, seq), lambda j: (att(j) // n_blocks, 0, 0, 0)),
            pl.BlockSpec((1, IDX_HEADS * IDX_DIM, tq),
                         lambda j: (sel(j) // n_blocks, 0, first_block + sel(j) % n_blocks)),
            pl.BlockSpec((1, 2, seq, 2 * IDX_DIM), lambda j: (sel(j) // n_blocks, 0, 0, 0)),
            pl.BlockSpec((1, WI_ROWS, tq),
                         lambda j: (sel(j) // n_blocks, 0, first_block + sel(j) % n_blocks)),
            pl.BlockSpec(memory_space=pl.ANY),
        ],
        out_specs=pl.BlockSpec((1, tq, ATT_WIDTH),
                               lambda j: (att(j) // n_blocks, first_block + att(j) % n_blocks, 0)),
        out_shape=jax.ShapeDtypeStruct(out.shape, out.dtype),
        input_output_aliases={6: 0},
        scratch_shapes=[pltpu.VMEM((seq, tq), F32), pltpu.VMEM((seq, tq), BF16),
                        pltpu.VMEM((2, seq, tq), BF16)],
        compiler_params=pltpu.CompilerParams(
            dimension_semantics=("arbitrary",),
            vmem_limit_bytes=VMEM_LIMIT_BYTES),
        name=f"attn_k{seq}",
    )(qT, kp, vT, qiT, ki, wiT, out)


def _attn(qT, kp, vT, qiT, ki, wiT, *, top_k, tq=512, blocks_per_class=1):
    batch, width, seq = qT.shape
    out = jnp.zeros((batch, seq, width), BF16)
    for fb in range(0, seq // tq, blocks_per_class):
        out = _attn_class(qT, kp, vT, qiT, ki, wiT, out, top_k=top_k, tq=tq, first_block=fb,
                          n_blocks=blocks_per_class)
    return out


def _pool_kernel(u_ref, pw_ref, ps_ref, o_ref):
    seq = u_ref.shape[1]
    t = lax.broadcasted_iota(jnp.int32, (seq, POOL_CH), 0)
    for g, w in enumerate(POOL_WINDOWS):
        ug = u_ref[0, :, g * POOL_CH:(g + 1) * POOL_CH]
        s, sh = ug, 1
        while sh < w:
            s = s + jnp.where(t >= sh, pltpu.roll(s, sh, axis=0), 0.0)
            sh *= 2
        pooled = s / jnp.minimum(t + 1, w).astype(F32) - ug
        mixed = jnp.dot(pooled.astype(BF16), pw_ref[g], preferred_element_type=F32)
        o_ref[0, :, g * POOL_CH:(g + 1) * POOL_CH] = (
            mixed * ps_ref[:, g * POOL_CH:(g + 1) * POOL_CH]).astype(BF16)


def _pool(u, pw, ps):
    batch, seq, width = u.shape
    return pl.pallas_call(
        _pool_kernel,
        grid=(batch,),
        in_specs=[
            pl.BlockSpec((1, seq, width), lambda b: (b, 0, 0)),
            pl.BlockSpec(pw.shape, lambda b: (0, 0, 0)),
            pl.BlockSpec((1, width), lambda b: (0, 0)),
        ],
        out_specs=pl.BlockSpec((1, seq, width), lambda b: (b, 0, 0)),
        out_shape=jax.ShapeDtypeStruct((batch, seq, width), BF16),
        compiler_params=pltpu.CompilerParams(
            dimension_semantics=("parallel",), vmem_limit_bytes=VMEM_LIMIT_BYTES),
        name="pool",
    )(u, pw, ps)


def _rope_tables(seq, rot_dim):
    half = rot_dim // 2
    inv = ROPE_THETA ** (-jnp.arange(half, dtype=F32) / half)
    ang = inv[:, None] * jnp.arange(seq, dtype=F32)[None, :]
    return jnp.cos(ang), jnp.sin(ang)


def kernel(x, ffn1_norm, ffn1_w_gate, ffn1_w_up, ffn1_w_down, mix_norm, w_in, q_norm, k_norm,
           pool_w, pool_scale, w_out, ffn2_norm, ffn2_w_gate, ffn2_w_up, ffn2_w_down):
    batch, seq, d = x.shape
    depth = w_in.shape[0]
    top_k = min(INDEX_TOPK, seq // 4)
    idx_scale = (IDX_HEADS ** -0.5) * (IDX_DIM ** -0.5)
    cos, sin = _rope_tables(seq, HEAD_DIM)
    cosi, sini = _rope_tables(seq, IDX_ROPE_DIM)
    n_feat = 3 * ATT_WIDTH + IDX_HEADS * IDX_DIM + IDX_DIM
    xf = x.reshape(batch * seq, d)

    for l in range(depth):
        xf = _ffn(xf, ffn1_norm[l][None], ffn1_w_gate[l], ffn1_w_up[l], ffn1_w_down[l])

        wt, wu = _prep_w_in(w_in[l].T, n_feat + IDX_HEADS)
        qT, kp, vT, qiT, ki, wiT, u = _proj(
            xf, mix_norm[l][None], wt, wu, q_norm[l][:, None], k_norm[l][:, None],
            cos, sin, cosi, sini, batch=batch, seq=seq, idx_scale=idx_scale)

        attn = _attn(qT, kp, vT, qiT, ki, wiT, top_k=top_k)
        pooled = _pool(u.reshape(batch, seq, POOL_WIDTH), pool_w[l].astype(BF16), pool_scale[l][None])

        mixer = (attn.reshape(batch * seq, ATT_WIDTH), pooled.reshape(batch * seq, POOL_WIDTH),
                 w_out[l].astype(BF16))
        xf = _ffn(xf, ffn2_norm[l][None], ffn2_w_gate[l], ffn2_w_up[l], ffn2_w_down[l], mixer)
    return xf.reshape(batch, seq, d)
```

```python
import functools
import math

import jax
import jax.numpy as jnp
import numpy as np
from jax import lax
from jax.experimental import pallas as pl
from jax.experimental.pallas import tpu as pltpu

HEAD_DIM = 64
N_HEADS = 8
ATT_WIDTH = N_HEADS * HEAD_DIM
IDX_HEADS = 4
IDX_DIM = 64
IDX_ROPE_DIM = 32
INDEX_TOPK = 256
POOL_WINDOWS = (2, 4, 8, 16)
POOL_CH = 128
POOL_WIDTH = POOL_CH * len(POOL_WINDOWS)
ROPE_THETA = 10000.0
NORM_EPS = 1e-6
WI_ROWS = 8
V_ROWS = HEAD_DIM + 16
PROJ_GROUP = 512
FEATURE_ROWS = 3 * ATT_WIDTH + IDX_HEADS * IDX_DIM + IDX_DIM + WI_ROWS

VMEM_LIMIT_BYTES = 56 * 1024 * 1024

F32 = jnp.float32
BF16 = jnp.bfloat16
NT_DIMS = (((1,), (1,)), ((), ()))


def _rms(x, g):
    ms = jnp.mean(x * x, axis=-1, keepdims=True)
    return x * lax.rsqrt(ms + NORM_EPS) * g


WEIGHT_CHUNKS = 16
STAGE_SLOTS = 8


def _load_weights_bf16(transfers):
    steps = []
    for src, dst, stage, sems in transfers:
        slots, rows = stage.shape[0], stage.shape[1]
        assert slots == STAGE_SLOTS and (src.shape[0] // rows) % slots == 0
        for c in range(src.shape[0] // rows):
            slot = c % slots
            copy = pltpu.make_async_copy(src.at[pl.ds(c * rows, rows), :], stage.at[slot], sems.at[slot])
            steps.append((copy, dst, c * rows, rows, stage, slot))
    ahead = STAGE_SLOTS - 1
    for copy, *_ in steps[:ahead]:
        copy.start()
    for k, (copy, dst, r0, rows, stage, slot) in enumerate(steps):
        copy.wait()
        if k + ahead < len(steps):
            steps[k + ahead][0].start()
        dst[r0:r0 + rows, :] = stage[slot].astype(BF16)


POOL_HALO = 16


def _pool_tile(u, halo, pw_ref, ps_ref, first_pos):
    rows = u.shape[0] + POOL_HALO
    pos = first_pos - POOL_HALO + lax.broadcasted_iota(jnp.int32, (rows, POOL_CH), 0)
    outs = []
    for g, w in enumerate(POOL_WINDOWS):
        cols = slice(g * POOL_CH, (g + 1) * POOL_CH)
        ext = jnp.concatenate([halo[:, cols], u[:, cols]], axis=0)
        s, sh = ext, 1
        while sh < w:
            s = s + pltpu.roll(s, sh, axis=0)
            sh *= 2
        pooled = (s / jnp.clip(pos + 1, 1, w).astype(F32) - ext)[POOL_HALO:]
        mixed = jnp.dot(pooled.astype(BF16), pw_ref[g], preferred_element_type=F32)
        outs.append((mixed * ps_ref[:, cols]).astype(BF16))
    return jnp.concatenate(outs, axis=1)


def _ffn_kernel(x_ref, g_ref, wg_hbm, wu_hbm, wd_hbm, *rest, tf, tiles_per_seq):
    *mixer, o_ref, act_ref, wg_ref, wu_ref, wd_ref, stage_in, stage_out, sem_in, sem_out = rest

    @pl.when(pl.program_id(0) == 0)
    def _():
        _load_weights_bf16([(wg_hbm, wg_ref, stage_in, sem_in), (wu_hbm, wu_ref, stage_in, sem_in),
                            (wd_hbm, wd_ref, stage_out, sem_out)])

    x = x_ref[...]
    if mixer:
        attn_ref, u_ref, halo_ref, pw_ref, ps_ref, wo_ref = mixer
        tile_in_seq = pl.program_id(0) % tiles_per_seq
        pooled = _pool_tile(u_ref[...], jnp.where(tile_in_seq == 0, 0.0, halo_ref[...]),
                            pw_ref, ps_ref, tile_in_seq * x.shape[0])
        mixed = jnp.concatenate([attn_ref[...], pooled], axis=1)
        x = x + jnp.dot(mixed, wo_ref[...], preferred_element_type=F32)
    h = _rms(x, g_ref[...]).astype(BF16)
    for c in range(wg_ref.shape[1] // tf):
        cols = slice(c * tf, (c + 1) * tf)
        gate = jnp.dot(h, wg_ref[:, cols], preferred_element_type=F32)
        up = jnp.dot(h, wu_ref[:, cols], preferred_element_type=F32)
        act_ref[:, cols] = (gate * jax.nn.sigmoid(gate) * up).astype(BF16)
    o_ref[...] = x + 0.5 * jnp.dot(act_ref[...], wd_ref[...], preferred_element_type=F32)


def _resident(shape):
    return pl.BlockSpec(shape, lambda *_: (0,) * len(shape), pipeline_mode=pl.Buffered(1))


def _ffn(x, g, wg, wu, wd, mixer=(), *, seq=None, tm=512, tf=256):
    n, d = x.shape
    f = wg.shape[1]
    rows = lambda i: (i, 0)
    mixer_specs = []
    if mixer:
        attn, u, pw, ps, wo = mixer
        halo_blocks_per_tile = tm // POOL_HALO
        mixer_specs = [pl.BlockSpec((tm, attn.shape[1]), rows), pl.BlockSpec((tm, u.shape[1]), rows),
                       pl.BlockSpec((POOL_HALO, u.shape[1]),
                                    lambda i: (jnp.maximum(i * halo_blocks_per_tile - 1, 0), 0)),
                       _resident(pw.shape), _resident(ps.shape), _resident(wo.shape)]
        mixer = (attn, u, u, pw, ps, wo)
    return pl.pallas_call(
        functools.partial(_ffn_kernel, tf=tf, tiles_per_seq=(seq // tm if mixer else None)),
        grid=(n // tm,),
        in_specs=[
            pl.BlockSpec((tm, d), rows),
            _resident((1, d)),
            pl.BlockSpec(memory_space=pl.ANY),
            pl.BlockSpec(memory_space=pl.ANY),
            pl.BlockSpec(memory_space=pl.ANY),
            *mixer_specs,
        ],
        out_specs=pl.BlockSpec((tm, d), rows),
        out_shape=jax.ShapeDtypeStruct((n, d), F32),
        scratch_shapes=[
            pltpu.VMEM((tm, f), BF16),
            pltpu.VMEM((d, f), BF16), pltpu.VMEM((d, f), BF16), pltpu.VMEM((f, d), BF16),
            pltpu.VMEM((STAGE_SLOTS, d // WEIGHT_CHUNKS, f), F32),
            pltpu.VMEM((STAGE_SLOTS, f // WEIGHT_CHUNKS, d), F32),
            pltpu.SemaphoreType.DMA((STAGE_SLOTS,)), pltpu.SemaphoreType.DMA((STAGE_SLOTS,)),
        ],
        compiler_params=pltpu.CompilerParams(
            dimension_semantics=("arbitrary",),
            vmem_limit_bytes=VMEM_LIMIT_BYTES),
        name="ffn_mix" if mixer else "ffn",
    )(x, g, wg, wu, wd, *mixer)


LANES = 128


def _prep_w_in_kernel(wT_ref, wt_ref, wu_ref, *, n_feat):
    wt_ref[...] = wT_ref[:wt_ref.shape[0], :].astype(BF16)
    wu_ref[...] = wT_ref[n_feat:, :].T.astype(BF16)


def _prep_w_in(wT, n_feat):
    cols, d = wT.shape
    rows = -(-n_feat // LANES) * LANES
    assert rows <= cols
    return pl.pallas_call(
        functools.partial(_prep_w_in_kernel, n_feat=n_feat),
        out_shape=(jax.ShapeDtypeStruct((rows, d), BF16), jax.ShapeDtypeStruct((d, cols - n_feat), BF16)),
        compiler_params=pltpu.CompilerParams(vmem_limit_bytes=VMEM_LIMIT_BYTES),
        name="prep_w_in",
    )(wT)


def _rope_rows(x, cos, sin, half):
    x1, x2 = x[:half], x[half:2 * half]
    return x1 * cos - x2 * sin, x2 * cos + x1 * sin


def _proj_kernel(x_ref, g_ref, wt_ref, wu_ref, gq_ref, gk_ref, cos_ref, sin_ref,
                 cosi_ref, sini_ref,
                 qT_ref, kp_ref, vT_ref, qiT_ref, ki_ref, wiT_ref, u_ref, *, idx_scale, q_scale):
    h = _rms(x_ref[...], g_ref[...]).astype(BF16)
    u_ref[...] = jnp.dot(h, wu_ref[...], preferred_element_type=F32)

    class _FeatureMajorProjection:
        shape = (FEATURE_ROWS, h.shape[0])

        def __init__(self):
            self._groups = {}

        def __getitem__(self, rows):
            g = rows.start // PROJ_GROUP
            assert (rows.stop - 1) // PROJ_GROUP == g
            if g not in self._groups:
                g0 = g * PROJ_GROUP
                g1 = min(g0 + PROJ_GROUP, self.shape[0])
                self._groups[g] = lax.dot_general(wt_ref[g0:g1, :], h, NT_DIMS,
                                                  preferred_element_type=F32)
            return self._groups[g][rows.start - g * PROJ_GROUP:rows.stop - g * PROJ_GROUP]

    pT = _FeatureMajorProjection()

    cos, sin = cos_ref[...], sin_ref[...]
    gq, gk = gq_ref[...], gk_ref[...]
    half = HEAD_DIM // 2

    def head_norm_rope(rows, gain):
        ms = jnp.mean(rows * rows, axis=0, keepdims=True)
        return _rope_rows(rows * lax.rsqrt(ms + NORM_EPS) * gain, cos, sin, half)

    for hd in range(N_HEADS):
        r0 = hd * HEAD_DIM
        a, b = head_norm_rope(pT[r0:r0 + HEAD_DIM], gq)
        qT_ref[0, r0:r0 + half, :] = (a * q_scale).astype(BF16)
        qT_ref[0, r0 + half:r0 + HEAD_DIM, :] = (b * q_scale).astype(BF16)

    k0 = ATT_WIDTH
    for pair in range(N_HEADS // 2):
        parts = []
        for e in range(2):
            r0 = k0 + (2 * pair + e) * HEAD_DIM
            parts.extend(head_norm_rope(pT[r0:r0 + HEAD_DIM], gk))
        kp_ref[0, pair] = jnp.concatenate(parts, axis=0).T.astype(BF16)

    v0 = 2 * ATT_WIDTH
    tt = pT.shape[1]
    ones_rows = jnp.where(lax.broadcasted_iota(jnp.int32, (V_ROWS - HEAD_DIM, tt), 0) == 0,
                          1.0, 0.0).astype(BF16)
    for hd in range(N_HEADS):
        r0 = v0 + hd * HEAD_DIM
        vT_ref[0, hd, :HEAD_DIM, :] = pT[r0:r0 + HEAD_DIM].astype(BF16)
        vT_ref[0, hd, HEAD_DIM:, :] = ones_rows

    cosi, sini = cosi_ref[...], sini_ref[...]
    ihalf = IDX_ROPE_DIM // 2
    qi0 = 3 * ATT_WIDTH
    for hi in range(IDX_HEADS):
        r0 = qi0 + hi * IDX_DIM
        a, b = _rope_rows(pT[r0:r0 + IDX_DIM], cosi, sini, ihalf)
        qiT_ref[0, r0 - qi0:r0 - qi0 + ihalf, :] = a.astype(BF16)
        qiT_ref[0, r0 - qi0 + ihalf:r0 - qi0 + IDX_ROPE_DIM, :] = b.astype(BF16)
        qiT_ref[0, r0 - qi0 + IDX_ROPE_DIM:r0 - qi0 + IDX_DIM, :] = (
            pT[r0 + IDX_ROPE_DIM:r0 + IDX_DIM].astype(BF16))

    ki0 = qi0 + IDX_HEADS * IDX_DIM
    a, b = _rope_rows(pT[ki0:ki0 + IDX_DIM], cosi, sini, ihalf)
    ki = jnp.concatenate([a, b, pT[ki0 + IDX_ROPE_DIM:ki0 + IDX_DIM]], axis=0)
    zeros = jnp.zeros_like(ki)
    ki_ref[0, 0] = jnp.concatenate([ki, zeros], axis=0).T.astype(BF16)
    ki_ref[0, 1] = jnp.concatenate([zeros, ki], axis=0).T.astype(BF16)

    wi0 = ki0 + IDX_DIM
    wiT_ref[0] = pT[wi0:wi0 + WI_ROWS] * idx_scale


def _proj(x1, g, wt, wu, gq, gk, cos, sin, cosi, sini, *, batch, seq, idx_scale, tt=1024):
    n, d = x1.shape
    rows = wt.shape[0]
    nt = seq // tt
    tok = lambda b, t: (b * nt + t, 0)
    const = lambda b, t: (0, 0)
    featmaj = lambda b, t: (b, 0, t)
    out_shape = (
        jax.ShapeDtypeStruct((batch, ATT_WIDTH, seq), BF16),
        jax.ShapeDtypeStruct((batch, N_HEADS // 2, seq, 2 * HEAD_DIM), BF16),
        jax.ShapeDtypeStruct((batch, N_HEADS, V_ROWS, seq), BF16),
        jax.ShapeDtypeStruct((batch, IDX_HEADS * IDX_DIM, seq), BF16),
        jax.ShapeDtypeStruct((batch, 2, seq, 2 * IDX_DIM), BF16),
        jax.ShapeDtypeStruct((batch, WI_ROWS, seq), F32),
        jax.ShapeDtypeStruct((n, POOL_WIDTH), F32),
    )
    out_specs = (
        pl.BlockSpec((1, ATT_WIDTH, tt), featmaj),
        pl.BlockSpec((1, N_HEADS // 2, tt, 2 * HEAD_DIM), lambda b, t: (b, 0, t, 0)),
        pl.BlockSpec((1, N_HEADS, V_ROWS, tt), lambda b, t: (b, 0, 0, t)),
        pl.BlockSpec((1, IDX_HEADS * IDX_DIM, tt), featmaj),
        pl.BlockSpec((1, 2, tt, 2 * IDX_DIM), lambda b, t: (b, 0, t, 0)),
        pl.BlockSpec((1, WI_ROWS, tt), featmaj),
        pl.BlockSpec((tt, POOL_WIDTH), tok),
    )
    in_specs = [
        pl.BlockSpec((tt, d), tok),
        pl.BlockSpec((1, d), const),
        pl.BlockSpec((rows, d), const),
        pl.BlockSpec((d, POOL_WIDTH), const),
        pl.BlockSpec((HEAD_DIM, 1), const),
        pl.BlockSpec((HEAD_DIM, 1), const),
        pl.BlockSpec((HEAD_DIM // 2, tt), lambda b, t: (0, t)),
        pl.BlockSpec((HEAD_DIM // 2, tt), lambda b, t: (0, t)),
        pl.BlockSpec((IDX_ROPE_DIM // 2, tt), lambda b, t: (0, t)),
        pl.BlockSpec((IDX_ROPE_DIM // 2, tt), lambda b, t: (0, t)),
    ]
    return pl.pallas_call(
        functools.partial(_proj_kernel, idx_scale=idx_scale,
                          q_scale=HEAD_DIM ** -0.5 * math.log2(math.e)),
        grid=(batch, nt),
        in_specs=in_specs,
        out_specs=out_specs,
        out_shape=out_shape,
        compiler_params=pltpu.CompilerParams(
            dimension_semantics=("parallel", "parallel"),
            vmem_limit_bytes=VMEM_LIMIT_BYTES),
        name="proj",
    )(x1, g, wt, wu, gq, gk, cos, sin, cosi, sini)


INT_MIN = np.int32(-2 ** 31)
COUNT_ROWS = 64
COUNT_ROWS_F32 = 32
TIE_CHUNK = 256
COARSE_BITS = 16
FINE_BITS = 17


def _count(pred):
    ones = jnp.where(pred, jnp.bfloat16(1), jnp.bfloat16(0))
    rows, cols = ones.shape
    groups = ones.reshape(rows // COUNT_ROWS, COUNT_ROWS, cols)
    part = groups[0]
    for i in range(1, rows // COUNT_ROWS):
        part = part + groups[i]
    return part.astype(F32).sum(axis=0, keepdims=True).astype(jnp.int32)


def _count_f32(pred):
    ones = jnp.where(pred, 1.0, 0.0)
    rows, cols = ones.shape
    part = ones.reshape(rows // COUNT_ROWS_F32, COUNT_ROWS_F32, cols).sum(axis=0)
    return part.sum(axis=0, keepdims=True).astype(jnp.int32)


def _value_of_key(key):
    return pltpu.bitcast(jnp.where(key < 0, INT_MIN - key, key), F32)


def _topk_bias(score, causal_bias, score_ref, coarse_ref, bias_out_ref, top_k):
    seq, tq = score.shape
    score_ref[...] = score
    coarse_ref[...] = score.astype(BF16)

    shift, mid = 32 - COARSE_BITS, 1 << (COARSE_BITS - 1)
    found = jnp.zeros((1, tq), jnp.int32)
    for bit in reversed(range(COARSE_BITS)):
        cand = found | (1 << bit)
        cand_value = _value_of_key(lax.shift_left(cand - mid, shift)).astype(BF16)
        found = jnp.where(_count(coarse_ref[...] >= cand_value) >= top_k, cand, found)
    base = lax.shift_left(found - mid, shift) - (1 << (shift - 1))
    offset = jnp.zeros((1, tq), jnp.int32)
    for bit in reversed(range(FINE_BITS)):
        cand = offset | (1 << bit)
        offset = jnp.where(_count_f32(score_ref[...] >= _value_of_key(base + cand)) >= top_k, cand, offset)
    thr = _value_of_key(base + offset)
    need = top_k - _count_f32(score_ref[...] > thr)

    c = TIE_CHUNK
    lower_tri = jnp.where(lax.broadcasted_iota(jnp.int32, (c, c), 0)
                          >= lax.broadcasted_iota(jnp.int32, (c, c), 1), 1.0, 0.0).astype(BF16)
    remaining = need.astype(F32)
    for r0 in range(0, seq, c):
        x = score_ref[r0:r0 + c]
        passes = causal_bias(r0, c)
        ties = x == thr
        rank = jnp.dot(lower_tri, jnp.where(ties, 1.0, 0.0).astype(BF16), preferred_element_type=F32)
        bias_out_ref[r0:r0 + c] = jnp.where(
            x > thr, passes,
            jnp.where(ties, jnp.where(rank <= remaining, passes, -jnp.inf), -jnp.inf)).astype(BF16)
        remaining = remaining - rank[c - 1:c, :]


def _attend_heads(qT_ref, kp_ref, vT_ref, bias_ref, o_ref):
    tq = qT_ref.shape[2]
    half_rows = lax.broadcasted_iota(jnp.int32, (2 * HEAD_DIM, tq), 0) < HEAD_DIM
    outs = []
    for hd in range(N_HEADS):
        pair, e = divmod(hd, 2)
        qpair = qT_ref[0, 2 * HEAD_DIM * pair:2 * HEAD_DIM * (pair + 1), :]
        qz = jnp.where(half_rows if e == 0 else jnp.logical_not(half_rows), qpair,
                       jnp.zeros_like(qpair))
        logits = jnp.dot(kp_ref[0, pair], qz, preferred_element_type=F32).astype(BF16) + bias_ref[...]
        p = jnp.exp2(logits - jnp.max(logits, axis=0, keepdims=True))
        o_aug = jnp.dot(vT_ref[0, hd], p, preferred_element_type=F32)
        outs.append(o_aug[:HEAD_DIM] / o_aug[HEAD_DIM:HEAD_DIM + 1])
    o_ref[0] = jnp.concatenate(outs, axis=0).T.astype(BF16)


def _attn_kernel(qT_ref, kp_ref, vT_ref, qiT_ref, ki_ref, wiT_ref, _aliased_out_ref, o_ref,
                 score_ref, coarse_ref, bias_ref, *, top_k, first_block, n_blocks, n_steps):
    j = pl.program_id(0)
    seq = kp_ref.shape[2]
    tq = qT_ref.shape[2]
    sel_slot = j % 2
    att_slot = 1 - sel_slot

    @pl.when(j == 0)
    def _():
        bias_ref[1] = jnp.zeros((seq, tq), BF16)

    block = first_block + jnp.minimum(j, n_steps - 1) % n_blocks
    all_causal_rows = first_block * tq

    def causal(r0, rows):
        k_pos = r0 + lax.broadcasted_iota(jnp.int32, (rows, tq), 0)
        return k_pos <= block * tq + lax.broadcasted_iota(jnp.int32, (rows, tq), 1)

    def causal_bias(r0, rows):
        if r0 + rows <= all_causal_rows:
            return 0.0
        return jnp.where(causal(r0, rows), 0.0, -jnp.inf)

    wi = wiT_ref[0]
    score = jnp.zeros((seq, tq), F32)
    for hi in range(IDX_HEADS):
        pair, e = divmod(hi, 2)
        rel = jnp.dot(ki_ref[0, e], qiT_ref[0, 2 * IDX_DIM * pair:2 * IDX_DIM * (pair + 1), :],
                      preferred_element_type=F32)
        score = score + wi[hi:hi + 1, :] * jnp.maximum(rel, 0.0)
    tail = jnp.where(causal(all_causal_rows, seq - all_causal_rows), score[all_causal_rows:], -jnp.inf)
    score = jnp.concatenate([score[:all_causal_rows], tail], axis=0) if all_causal_rows else tail

    _attend_heads(qT_ref, kp_ref, vT_ref, bias_ref.at[att_slot], o_ref)
    _topk_bias(score, causal_bias, score_ref, coarse_ref, bias_ref.at[sel_slot], top_k)


def _attn_class(qT, kp, vT, qiT, ki, wiT, out, *, top_k, tq, first_block, n_blocks):
    batch = qT.shape[0]
    seq = (first_block + n_blocks) * tq
    n_steps = batch * n_blocks

    def sel(j):
        return jnp.minimum(j, n_steps - 1)

    def att(j):
        return jnp.maximum(j - 1, 0)

    return pl.pallas_call(
        functools.partial(_attn_kernel, top_k=top_k, first_block=first_block, n_blocks=n_blocks,
                          n_steps=n_steps),
        grid=(n_steps + 1,),
        in_specs=[
            pl.BlockSpec((1, ATT_WIDTH, tq),
                         lambda j: (att(j) // n_blocks, 0, first_block + att(j) % n_blocks)),
            pl.BlockSpec((1, N_HEADS // 2, seq, 2 * HEAD_DIM), lambda j: (att(j) // n_blocks, 0, 0, 0)),
            pl.BlockSpec((1, N_HEADS, V_ROWS, seq), lambda j: (att(j) // n_blocks, 0, 0, 0)),
            pl.BlockSpec((1, IDX_HEADS * IDX_DIM, tq),
                         lambda j: (sel(j) // n_blocks, 0, first_block + sel(j) % n_blocks)),
            pl.BlockSpec((1, 2, seq, 2 * IDX_DIM), lambda j: (sel(j) // n_blocks, 0, 0, 0)),
            pl.BlockSpec((1, WI_ROWS, tq),
                         lambda j: (sel(j) // n_blocks, 0, first_block + sel(j) % n_blocks)),
            pl.BlockSpec(memory_space=pl.ANY),
        ],
        out_specs=pl.BlockSpec((1, tq, ATT_WIDTH),
                               lambda j: (att(j) // n_blocks, first_block + att(j) % n_blocks, 0)),
        out_shape=jax.ShapeDtypeStruct(out.shape, out.dtype),
        input_output_aliases={6: 0},
        scratch_shapes=[pltpu.VMEM((seq, tq), F32), pltpu.VMEM((seq, tq), BF16),
                        pltpu.VMEM((2, seq, tq), BF16)],
        compiler_params=pltpu.CompilerParams(
            dimension_semantics=("arbitrary",),
            vmem_limit_bytes=VMEM_LIMIT_BYTES),
        name=f"attn_k{seq}",
    )(qT, kp, vT, qiT, ki, wiT, out)


def _attn(qT, kp, vT, qiT, ki, wiT, *, top_k, tq=512, blocks_per_class=1):
    batch, width, seq = qT.shape
    out = jnp.zeros((batch, seq, width), BF16)
    for fb in range(0, seq // tq, blocks_per_class):
        out = _attn_class(qT, kp, vT, qiT, ki, wiT, out, top_k=top_k, tq=tq, first_block=fb,
                          n_blocks=blocks_per_class)
    return out


def _rope_tables(seq, rot_dim):
    half = rot_dim // 2
    inv = ROPE_THETA ** (-jnp.arange(half, dtype=F32) / half)
    ang = inv[:, None] * jnp.arange(seq, dtype=F32)[None, :]
    return jnp.cos(ang), jnp.sin(ang)


def kernel(x, ffn1_norm, ffn1_w_gate, ffn1_w_up, ffn1_w_down, mix_norm, w_in, q_norm, k_norm,
           pool_w, pool_scale, w_out, ffn2_norm, ffn2_w_gate, ffn2_w_up, ffn2_w_down):
    batch, seq, d = x.shape
    depth = w_in.shape[0]
    top_k = min(INDEX_TOPK, seq // 4)
    idx_scale = (IDX_HEADS ** -0.5) * (IDX_DIM ** -0.5)
    cos, sin = _rope_tables(seq, HEAD_DIM)
    cosi, sini = _rope_tables(seq, IDX_ROPE_DIM)
    n_feat = 3 * ATT_WIDTH + IDX_HEADS * IDX_DIM + IDX_DIM
    xf = x.reshape(batch * seq, d)

    for l in range(depth):
        xf = _ffn(xf, ffn1_norm[l][None], ffn1_w_gate[l], ffn1_w_up[l], ffn1_w_down[l])

        wt, wu = _prep_w_in(w_in[l].T, n_feat + IDX_HEADS)
        qT, kp, vT, qiT, ki, wiT, u = _proj(
            xf, mix_norm[l][None], wt, wu, q_norm[l][:, None], k_norm[l][:, None],
            cos, sin, cosi, sini, batch=batch, seq=seq, idx_scale=idx_scale)

        attn = _attn(qT, kp, vT, qiT, ki, wiT, top_k=top_k)
        mixer = (attn.reshape(batch * seq, ATT_WIDTH), u, pool_w[l].astype(BF16), pool_scale[l][None],
                 w_out[l].astype(BF16))
        xf = _ffn(xf, ffn2_norm[l][None], ffn2_w_gate[l], ffn2_w_up[l], ffn2_w_down[l], mixer, seq=seq)
    return xf.reshape(batch, seq, d)
```

```python
import functools
import math

import jax
import jax.numpy as jnp
import numpy as np
from jax import lax
from jax.experimental import pallas as pl
from jax.experimental.pallas import tpu as pltpu

HEAD_DIM = 64
N_HEADS = 8
ATT_WIDTH = N_HEADS * HEAD_DIM
IDX_HEADS = 4
IDX_DIM = 64
IDX_ROPE_DIM = 32
INDEX_TOPK = 256
POOL_WINDOWS = (2, 4, 8, 16)
POOL_CH = 128
POOL_WIDTH = POOL_CH * len(POOL_WINDOWS)
ROPE_THETA = 10000.0
NORM_EPS = 1e-6
WI_ROWS = 8
V_ROWS = HEAD_DIM + 16
PROJ_GROUP = 512
FEATURE_ROWS = 3 * ATT_WIDTH + IDX_HEADS * IDX_DIM + IDX_DIM + WI_ROWS

VMEM_LIMIT_BYTES = 56 * 1024 * 1024

F32 = jnp.float32
BF16 = jnp.bfloat16
NT_DIMS = (((1,), (1,)), ((), ()))


def _rms(x, g):
    ms = jnp.mean(x * x, axis=-1, keepdims=True)
    return x * lax.rsqrt(ms + NORM_EPS) * g


WEIGHT_CHUNKS = 16
STAGE_SLOTS = 8


def _load_weights_bf16(transfers):
    steps = []
    for src, dst, stage, sems in transfers:
        slots, rows = stage.shape[0], stage.shape[1]
        assert slots == STAGE_SLOTS and (src.shape[0] // rows) % slots == 0
        for c in range(src.shape[0] // rows):
            slot = c % slots
            copy = pltpu.make_async_copy(src.at[pl.ds(c * rows, rows), :], stage.at[slot], sems.at[slot])
            steps.append((copy, dst, c * rows, rows, stage, slot))
    ahead = STAGE_SLOTS - 1
    for copy, *_ in steps[:ahead]:
        copy.start()
    for k, (copy, dst, r0, rows, stage, slot) in enumerate(steps):
        copy.wait()
        if k + ahead < len(steps):
            steps[k + ahead][0].start()
        dst[r0:r0 + rows, :] = stage[slot].astype(BF16)


POOL_HALO = 16


def _pool_tile(u, halo, pw_ref, ps_ref, first_pos):
    rows = u.shape[0] + POOL_HALO
    pos = first_pos - POOL_HALO + lax.broadcasted_iota(jnp.int32, (rows, POOL_CH), 0)
    outs = []
    for g, w in enumerate(POOL_WINDOWS):
        cols = slice(g * POOL_CH, (g + 1) * POOL_CH)
        ext = jnp.concatenate([halo[:, cols], u[:, cols]], axis=0)
        s, sh = ext, 1
        while sh < w:
            s = s + pltpu.roll(s, sh, axis=0)
            sh *= 2
        pooled = (s / jnp.clip(pos + 1, 1, w).astype(F32) - ext)[POOL_HALO:]
        mixed = jnp.dot(pooled.astype(BF16), pw_ref[g], preferred_element_type=F32)
        outs.append((mixed * ps_ref[:, cols]).astype(BF16))
    return jnp.concatenate(outs, axis=1)


def _ffn_kernel(x_ref, g_ref, wg_hbm, wu_hbm, wd_hbm, *rest, tf, tiles_per_seq):
    *mixer, o_ref, act_ref, wg_ref, wu_ref, wd_ref, stage_in, stage_out, sem_in, sem_out = rest

    @pl.when(pl.program_id(0) == 0)
    def _():
        _load_weights_bf16([(wg_hbm, wg_ref, stage_in, sem_in), (wu_hbm, wu_ref, stage_in, sem_in),
                            (wd_hbm, wd_ref, stage_out, sem_out)])

    x = x_ref[...]
    if mixer:
        attn_ref, u_ref, halo_ref, pw_ref, ps_ref, wo_ref = mixer
        tile_in_seq = pl.program_id(0) % tiles_per_seq
        pooled = _pool_tile(u_ref[...], jnp.where(tile_in_seq == 0, 0.0, halo_ref[...]),
                            pw_ref, ps_ref, tile_in_seq * x.shape[0])
        a = attn_ref.shape[1]
        x = x + jnp.dot(attn_ref[...], wo_ref[:a, :], preferred_element_type=F32)
        x = x + jnp.dot(pooled, wo_ref[a:, :], preferred_element_type=F32)
    h = _rms(x, g_ref[...]).astype(BF16)
    for c in range(wg_ref.shape[1] // tf):
        cols = slice(c * tf, (c + 1) * tf)
        gate = jnp.dot(h, wg_ref[:, cols], preferred_element_type=F32)
        up = jnp.dot(h, wu_ref[:, cols], preferred_element_type=F32)
        act_ref[:, cols] = (gate * jax.nn.sigmoid(gate) * up).astype(BF16)
    o_ref[...] = x + 0.5 * jnp.dot(act_ref[...], wd_ref[...], preferred_element_type=F32)


def _resident(shape):
    return pl.BlockSpec(shape, lambda *_: (0,) * len(shape), pipeline_mode=pl.Buffered(1))


def _ffn(x, g, wg, wu, wd, mixer=(), *, seq=None, tm=512, tf=256):
    n, d = x.shape
    f = wg.shape[1]
    rows = lambda i: (i, 0)
    mixer_specs = []
    if mixer:
        attn, u, pw, ps, wo = mixer
        halo_blocks_per_tile = tm // POOL_HALO
        mixer_specs = [pl.BlockSpec((tm, attn.shape[1]), rows), pl.BlockSpec((tm, u.shape[1]), rows),
                       pl.BlockSpec((POOL_HALO, u.shape[1]),
                                    lambda i: (jnp.maximum(i * halo_blocks_per_tile - 1, 0), 0)),
                       _resident(pw.shape), _resident(ps.shape), _resident(wo.shape)]
        mixer = (attn, u, u, pw, ps, wo)
    return pl.pallas_call(
        functools.partial(_ffn_kernel, tf=tf, tiles_per_seq=(seq // tm if mixer else None)),
        grid=(n // tm,),
        in_specs=[
            pl.BlockSpec((tm, d), rows),
            _resident((1, d)),
            pl.BlockSpec(memory_space=pl.ANY),
            pl.BlockSpec(memory_space=pl.ANY),
            pl.BlockSpec(memory_space=pl.ANY),
            *mixer_specs,
        ],
        out_specs=pl.BlockSpec((tm, d), rows),
        out_shape=jax.ShapeDtypeStruct((n, d), F32),
        scratch_shapes=[
            pltpu.VMEM((tm, f), BF16),
            pltpu.VMEM((d, f), BF16), pltpu.VMEM((d, f), BF16), pltpu.VMEM((f, d), BF16),
            pltpu.VMEM((STAGE_SLOTS, d // WEIGHT_CHUNKS, f), F32),
            pltpu.VMEM((STAGE_SLOTS, f // WEIGHT_CHUNKS, d), F32),
            pltpu.SemaphoreType.DMA((STAGE_SLOTS,)), pltpu.SemaphoreType.DMA((STAGE_SLOTS,)),
        ],
        compiler_params=pltpu.CompilerParams(
            dimension_semantics=("arbitrary",),
            vmem_limit_bytes=VMEM_LIMIT_BYTES),
        name="ffn_mix" if mixer else "ffn",
    )(x, g, wg, wu, wd, *mixer)


LANES = 128


def _prep_w_in_kernel(wT_ref, wt_ref, wu_ref, *, n_feat):
    wt_ref[...] = wT_ref[:wt_ref.shape[0], :].astype(BF16)
    wu_ref[...] = wT_ref[n_feat:, :].T.astype(BF16)


def _prep_w_in(wT, n_feat):
    cols, d = wT.shape
    rows = -(-n_feat // LANES) * LANES
    assert rows <= cols
    return pl.pallas_call(
        functools.partial(_prep_w_in_kernel, n_feat=n_feat),
        out_shape=(jax.ShapeDtypeStruct((rows, d), BF16), jax.ShapeDtypeStruct((d, cols - n_feat), BF16)),
        compiler_params=pltpu.CompilerParams(vmem_limit_bytes=VMEM_LIMIT_BYTES),
        name="prep_w_in",
    )(wT)


def _rope_rows(x, cos, sin, half):
    x1, x2 = x[:half], x[half:2 * half]
    return x1 * cos - x2 * sin, x2 * cos + x1 * sin


def _proj_kernel(x_ref, g_ref, wt_ref, wu_ref, gq_ref, gk_ref, cos_ref, sin_ref,
                 cosi_ref, sini_ref,
                 qT_ref, kp_ref, vT_ref, qiT_ref, ki_ref, wiT_ref, u_ref, *, idx_scale, q_scale):
    h = _rms(x_ref[...], g_ref[...]).astype(BF16)
    u_ref[...] = jnp.dot(h, wu_ref[...], preferred_element_type=F32)

    class _FeatureMajorProjection:
        shape = (FEATURE_ROWS, h.shape[0])

        def __init__(self):
            self._groups = {}

        def __getitem__(self, rows):
            g = rows.start // PROJ_GROUP
            assert (rows.stop - 1) // PROJ_GROUP == g
            if g not in self._groups:
                g0 = g * PROJ_GROUP
                g1 = min(g0 + PROJ_GROUP, self.shape[0])
                self._groups[g] = lax.dot_general(wt_ref[g0:g1, :], h, NT_DIMS,
                                                  preferred_element_type=F32)
            return self._groups[g][rows.start - g * PROJ_GROUP:rows.stop - g * PROJ_GROUP]

    pT = _FeatureMajorProjection()

    cos, sin = cos_ref[...], sin_ref[...]
    gq, gk = gq_ref[...], gk_ref[...]
    half = HEAD_DIM // 2

    def head_norm_rope(rows, gain):
        ms = jnp.mean(rows * rows, axis=0, keepdims=True)
        return _rope_rows(rows * lax.rsqrt(ms + NORM_EPS) * gain, cos, sin, half)

    for hd in range(N_HEADS):
        r0 = hd * HEAD_DIM
        a, b = head_norm_rope(pT[r0:r0 + HEAD_DIM], gq)
        qT_ref[0, r0:r0 + half, :] = (a * q_scale).astype(BF16)
        qT_ref[0, r0 + half:r0 + HEAD_DIM, :] = (b * q_scale).astype(BF16)

    k0 = ATT_WIDTH
    for pair in range(N_HEADS // 2):
        parts = []
        for e in range(2):
            r0 = k0 + (2 * pair + e) * HEAD_DIM
            parts.extend(head_norm_rope(pT[r0:r0 + HEAD_DIM], gk))
        kp_ref[0, pair] = jnp.concatenate(parts, axis=0).T.astype(BF16)

    v0 = 2 * ATT_WIDTH
    tt = pT.shape[1]
    ones_rows = jnp.where(lax.broadcasted_iota(jnp.int32, (V_ROWS - HEAD_DIM, tt), 0) == 0,
                          1.0, 0.0).astype(BF16)
    for hd in range(N_HEADS):
        r0 = v0 + hd * HEAD_DIM
        vT_ref[0, hd, :HEAD_DIM, :] = pT[r0:r0 + HEAD_DIM].astype(BF16)
        vT_ref[0, hd, HEAD_DIM:, :] = ones_rows

    cosi, sini = cosi_ref[...], sini_ref[...]
    ihalf = IDX_ROPE_DIM // 2
    qi0 = 3 * ATT_WIDTH
    for hi in range(IDX_HEADS):
        r0 = qi0 + hi * IDX_DIM
        a, b = _rope_rows(pT[r0:r0 + IDX_DIM], cosi, sini, ihalf)
        qiT_ref[0, r0 - qi0:r0 - qi0 + ihalf, :] = a.astype(BF16)
        qiT_ref[0, r0 - qi0 + ihalf:r0 - qi0 + IDX_ROPE_DIM, :] = b.astype(BF16)
        qiT_ref[0, r0 - qi0 + IDX_ROPE_DIM:r0 - qi0 + IDX_DIM, :] = (
            pT[r0 + IDX_ROPE_DIM:r0 + IDX_DIM].astype(BF16))

    ki0 = qi0 + IDX_HEADS * IDX_DIM
    a, b = _rope_rows(pT[ki0:ki0 + IDX_DIM], cosi, sini, ihalf)
    ki = jnp.concatenate([a, b, pT[ki0 + IDX_ROPE_DIM:ki0 + IDX_DIM]], axis=0)
    zeros = jnp.zeros_like(ki)
    ki_ref[0, 0] = jnp.concatenate([ki, zeros], axis=0).T.astype(BF16)
    ki_ref[0, 1] = jnp.concatenate([zeros, ki], axis=0).T.astype(BF16)

    wi0 = ki0 + IDX_DIM
    wiT_ref[0] = pT[wi0:wi0 + WI_ROWS] * idx_scale


def _proj(x1, g, wt, wu, gq, gk, cos, sin, cosi, sini, *, batch, seq, idx_scale, tt=1024):
    n, d = x1.shape
    rows = wt.shape[0]
    nt = seq // tt
    tok = lambda b, t: (b * nt + t, 0)
    const = lambda b, t: (0, 0)
    featmaj = lambda b, t: (b, 0, t)
    out_shape = (
        jax.ShapeDtypeStruct((batch, ATT_WIDTH, seq), BF16),
        jax.ShapeDtypeStruct((batch, N_HEADS // 2, seq, 2 * HEAD_DIM), BF16),
        jax.ShapeDtypeStruct((batch, N_HEADS, V_ROWS, seq), BF16),
        jax.ShapeDtypeStruct((batch, IDX_HEADS * IDX_DIM, seq), BF16),
        jax.ShapeDtypeStruct((batch, 2, seq, 2 * IDX_DIM), BF16),
        jax.ShapeDtypeStruct((batch, WI_ROWS, seq), F32),
        jax.ShapeDtypeStruct((n, POOL_WIDTH), F32),
    )
    out_specs = (
        pl.BlockSpec((1, ATT_WIDTH, tt), featmaj),
        pl.BlockSpec((1, N_HEADS // 2, tt, 2 * HEAD_DIM), lambda b, t: (b, 0, t, 0)),
        pl.BlockSpec((1, N_HEADS, V_ROWS, tt), lambda b, t: (b, 0, 0, t)),
        pl.BlockSpec((1, IDX_HEADS * IDX_DIM, tt), featmaj),
        pl.BlockSpec((1, 2, tt, 2 * IDX_DIM), lambda b, t: (b, 0, t, 0)),
        pl.BlockSpec((1, WI_ROWS, tt), featmaj),
        pl.BlockSpec((tt, POOL_WIDTH), tok),
    )
    in_specs = [
        pl.BlockSpec((tt, d), tok),
        pl.BlockSpec((1, d), const),
        pl.BlockSpec((rows, d), const),
        pl.BlockSpec((d, POOL_WIDTH), const),
        pl.BlockSpec((HEAD_DIM, 1), const),
        pl.BlockSpec((HEAD_DIM, 1), const),
        pl.BlockSpec((HEAD_DIM // 2, tt), lambda b, t: (0, t)),
        pl.BlockSpec((HEAD_DIM // 2, tt), lambda b, t: (0, t)),
        pl.BlockSpec((IDX_ROPE_DIM // 2, tt), lambda b, t: (0, t)),
        pl.BlockSpec((IDX_ROPE_DIM // 2, tt), lambda b, t: (0, t)),
    ]
    return pl.pallas_call(
        functools.partial(_proj_kernel, idx_scale=idx_scale,
                          q_scale=HEAD_DIM ** -0.5 * math.log2(math.e)),
        grid=(batch, nt),
        in_specs=in_specs,
        out_specs=out_specs,
        out_shape=out_shape,
        compiler_params=pltpu.CompilerParams(
            dimension_semantics=("parallel", "parallel"),
            vmem_limit_bytes=VMEM_LIMIT_BYTES),
        name="proj",
    )(x1, g, wt, wu, gq, gk, cos, sin, cosi, sini)


INT_MIN = np.int32(-2 ** 31)
COUNT_ROWS = 64
COUNT_ROWS_F32 = 32
TIE_CHUNK = 256
COARSE_BITS = 16
FINE_BITS = 17


def _count(pred):
    ones = jnp.where(pred, jnp.bfloat16(1), jnp.bfloat16(0))
    rows, cols = ones.shape
    groups = ones.reshape(rows // COUNT_ROWS, COUNT_ROWS, cols)
    part = groups[0]
    for i in range(1, rows // COUNT_ROWS):
        part = part + groups[i]
    return part.astype(F32).sum(axis=0, keepdims=True).astype(jnp.int32)


def _count_f32(pred):
    ones = jnp.where(pred, 1.0, 0.0)
    rows, cols = ones.shape
    part = ones.reshape(rows // COUNT_ROWS_F32, COUNT_ROWS_F32, cols).sum(axis=0)
    return part.sum(axis=0, keepdims=True).astype(jnp.int32)


def _value_of_key(key):
    return pltpu.bitcast(jnp.where(key < 0, INT_MIN - key, key), F32)


def _topk_bias(score, causal_bias, score_ref, coarse_ref, bias_out_ref, top_k):
    seq, tq = score.shape
    score_ref[...] = score
    coarse_ref[...] = score.astype(BF16)

    shift, mid = 32 - COARSE_BITS, 1 << (COARSE_BITS - 1)
    found = jnp.zeros((1, tq), jnp.int32)
    for bit in reversed(range(COARSE_BITS)):
        cand = found | (1 << bit)
        cand_value = _value_of_key(lax.shift_left(cand - mid, shift)).astype(BF16)
        found = jnp.where(_count(coarse_ref[...] >= cand_value) >= top_k, cand, found)
    base = lax.shift_left(found - mid, shift) - (1 << (shift - 1))
    offset = jnp.zeros((1, tq), jnp.int32)
    for bit in reversed(range(FINE_BITS)):
        cand = offset | (1 << bit)
        offset = jnp.where(_count_f32(score_ref[...] >= _value_of_key(base + cand)) >= top_k, cand, offset)
    thr = _value_of_key(base + offset)
    need = top_k - _count_f32(score_ref[...] > thr)

    c = TIE_CHUNK
    lower_tri = jnp.where(lax.broadcasted_iota(jnp.int32, (c, c), 0)
                          >= lax.broadcasted_iota(jnp.int32, (c, c), 1), 1.0, 0.0).astype(BF16)
    remaining = need.astype(F32)
    for r0 in range(0, seq, c):
        x = score_ref[r0:r0 + c]
        passes = causal_bias(r0, c)
        ties = x == thr
        rank = jnp.dot(lower_tri, jnp.where(ties, 1.0, 0.0).astype(BF16), preferred_element_type=F32)
        bias_out_ref[r0:r0 + c] = jnp.where(
            x > thr, passes,
            jnp.where(ties, jnp.where(rank <= remaining, passes, -jnp.inf), -jnp.inf)).astype(BF16)
        remaining = remaining - rank[c - 1:c, :]


def _attend_heads(qT_ref, kp_ref, vT_ref, bias_ref, o_ref):
    tq = qT_ref.shape[2]
    half_rows = lax.broadcasted_iota(jnp.int32, (2 * HEAD_DIM, tq), 0) < HEAD_DIM
    outs = []
    for hd in range(N_HEADS):
        pair, e = divmod(hd, 2)
        qpair = qT_ref[0, 2 * HEAD_DIM * pair:2 * HEAD_DIM * (pair + 1), :]
        qz = jnp.where(half_rows if e == 0 else jnp.logical_not(half_rows), qpair,
                       jnp.zeros_like(qpair))
        logits = jnp.dot(kp_ref[0, pair], qz, preferred_element_type=F32).astype(BF16) + bias_ref[...]
        p = jnp.exp2(logits - jnp.max(logits, axis=0, keepdims=True))
        o_aug = jnp.dot(vT_ref[0, hd], p, preferred_element_type=F32)
        outs.append(o_aug[:HEAD_DIM] / o_aug[HEAD_DIM:HEAD_DIM + 1])
    o_ref[0] = jnp.concatenate(outs, axis=0).T.astype(BF16)


def _attn_kernel(qT_ref, kp_ref, vT_ref, qiT_ref, ki_ref, wiT_ref, _aliased_out_ref, o_ref,
                 score_ref, coarse_ref, bias_ref, *, top_k, first_block, n_blocks, n_steps):
    j = pl.program_id(0)
    seq = kp_ref.shape[2]
    tq = qT_ref.shape[2]
    sel_slot = j % 2
    att_slot = 1 - sel_slot

    @pl.when(j == 0)
    def _():
        bias_ref[1] = jnp.zeros((seq, tq), BF16)

    block = first_block + jnp.minimum(j, n_steps - 1) % n_blocks
    all_causal_rows = first_block * tq

    def causal(r0, rows):
        k_pos = r0 + lax.broadcasted_iota(jnp.int32, (rows, tq), 0)
        return k_pos <= block * tq + lax.broadcasted_iota(jnp.int32, (rows, tq), 1)

    def causal_bias(r0, rows):
        if r0 + rows <= all_causal_rows:
            return 0.0
        return jnp.where(causal(r0, rows), 0.0, -jnp.inf)

    wi = wiT_ref[0]
    score = jnp.zeros((seq, tq), F32)
    for hi in range(IDX_HEADS):
        pair, e = divmod(hi, 2)
        rel = jnp.dot(ki_ref[0, e], qiT_ref[0, 2 * IDX_DIM * pair:2 * IDX_DIM * (pair + 1), :],
                      preferred_element_type=F32)
        score = score + wi[hi:hi + 1, :] * jnp.maximum(rel, 0.0)
    tail = jnp.where(causal(all_causal_rows, seq - all_causal_rows), score[all_causal_rows:], -jnp.inf)
    score = jnp.concatenate([score[:all_causal_rows], tail], axis=0) if all_causal_rows else tail

    _attend_heads(qT_ref, kp_ref, vT_ref, bias_ref.at[att_slot], o_ref)
    _topk_bias(score, causal_bias, score_ref, coarse_ref, bias_ref.at[sel_slot], top_k)


def _attn_class(qT, kp, vT, qiT, ki, wiT, out, *, top_k, tq, first_block, n_blocks):
    batch = qT.shape[0]
    seq = (first_block + n_blocks) * tq
    n_steps = batch * n_blocks

    def sel(j):
        return jnp.minimum(j, n_steps - 1)

    def att(j):
        return jnp.maximum(j - 1, 0)

    return pl.pallas_call(
        functools.partial(_attn_kernel, top_k=top_k, first_block=first_block, n_blocks=n_blocks,
                          n_steps=n_steps),
        grid=(n_steps + 1,),
        in_specs=[
            pl.BlockSpec((1, ATT_WIDTH, tq),
                         lambda j: (att(j) // n_blocks, 0, first_block + att(j) % n_blocks)),
            pl.BlockSpec((1, N_HEADS // 2, seq, 2 * HEAD_DIM), lambda j: (att(j) // n_blocks, 0, 0, 0)),
            pl.BlockSpec((1, N_HEADS, V_ROWS, seq), lambda j: (att(j) // n_blocks, 0, 0, 0)),
            pl.BlockSpec((1, IDX_HEADS * IDX_DIM, tq),
                         lambda j: (sel(j) // n_blocks, 0, first_block + sel(j) % n_blocks)),
            pl.BlockSpec((1, 2, seq, 2 * IDX_DIM), lambda j: (sel(j) // n_blocks, 0, 0, 0)),
            pl.BlockSpec((1, WI_ROWS, tq),
                         lambda j: (sel(j) // n_blocks, 0, first_block + sel(j) % n_blocks)),
            pl.BlockSpec(memory_space=pl.ANY),
        ],
        out_specs=pl.BlockSpec((1, tq, ATT_WIDTH),
                               lambda j: (att(j) // n_blocks, first_block + att(j) % n_blocks, 0)),
        out_shape=jax.ShapeDtypeStruct(out.shape, out.dtype),
        input_output_aliases={6: 0},
        scratch_shapes=[pltpu.VMEM((seq, tq), F32), pltpu.VMEM((seq, tq), BF16),
                        pltpu.VMEM((2, seq, tq), BF16)],
        compiler_params=pltpu.CompilerParams(
            dimension_semantics=("arbitrary",),
            vmem_limit_bytes=VMEM_LIMIT_BYTES),
        name=f"attn_k{seq}",
    )(qT, kp, vT, qiT, ki, wiT, out)


def _attn(qT, kp, vT, qiT, ki, wiT, *, top_k, tq=512, blocks_per_class=1):
    batch, width, seq = qT.shape
    out = jnp.zeros((batch, seq, width), BF16)
    for fb in range(0, seq // tq, blocks_per_class):
        out = _attn_class(qT, kp, vT, qiT, ki, wiT, out, top_k=top_k, tq=tq, first_block=fb,
                          n_blocks=blocks_per_class)
    return out


def _rope_tables(seq, rot_dim):
    half = rot_dim // 2
    inv = ROPE_THETA ** (-jnp.arange(half, dtype=F32) / half)
    ang = inv[:, None] * jnp.arange(seq, dtype=F32)[None, :]
    return jnp.cos(ang), jnp.sin(ang)


def kernel(x, ffn1_norm, ffn1_w_gate, ffn1_w_up, ffn1_w_down, mix_norm, w_in, q_norm, k_norm,
           pool_w, pool_scale, w_out, ffn2_norm, ffn2_w_gate, ffn2_w_up, ffn2_w_down):
    batch, seq, d = x.shape
    depth = w_in.shape[0]
    top_k = min(INDEX_TOPK, seq // 4)
    idx_scale = (IDX_HEADS ** -0.5) * (IDX_DIM ** -0.5)
    cos, sin = _rope_tables(seq, HEAD_DIM)
    cosi, sini = _rope_tables(seq, IDX_ROPE_DIM)
    n_feat = 3 * ATT_WIDTH + IDX_HEADS * IDX_DIM + IDX_DIM
    xf = x.reshape(batch * seq, d)

    for l in range(depth):
        xf = _ffn(xf, ffn1_norm[l][None], ffn1_w_gate[l], ffn1_w_up[l], ffn1_w_down[l])

        wt, wu = _prep_w_in(w_in[l].T, n_feat + IDX_HEADS)
        qT, kp, vT, qiT, ki, wiT, u = _proj(
            xf, mix_norm[l][None], wt, wu, q_norm[l][:, None], k_norm[l][:, None],
            cos, sin, cosi, sini, batch=batch, seq=seq, idx_scale=idx_scale)

        attn = _attn(qT, kp, vT, qiT, ki, wiT, top_k=top_k)
        mixer = (attn.reshape(batch * seq, ATT_WIDTH), u, pool_w[l].astype(BF16), pool_scale[l][None],
                 w_out[l].astype(BF16))
        xf = _ffn(xf, ffn2_norm[l][None], ffn2_w_gate[l], ffn2_w_up[l], ffn2_w_down[l], mixer, seq=seq)
    return xf.reshape(batch, seq, d)
```
